```python
import jax, jax.numpy as jnp
from jax import lax
import numpy as np

D_MODEL = 2048
BATCH = 1
SEQ = 8192
DEPTH = 1

CTX_LEN = 256
GRID_W = 64
MIX_WIDTH = D_MODEL
A_WIDTH = MIX_WIDTH // 2
A_HEADS = 8
A_HEAD_DIM = A_WIDTH // A_HEADS
A_CHUNK = 128
B_WIDTH = MIX_WIDTH - A_WIDTH
B_HEADS = 8
B_HEAD_DIM = B_WIDTH // B_HEADS
HALF_B = B_HEADS // 2
B_CHUNK = 64
D_FF = 4 * D_MODEL
IN_COLS = 2 * A_WIDTH + 5 * B_WIDTH
EPS = 1e-6

kernel_name = "hybrid_gmlp_hgrn2_dit_layer"


def rms_norm(x, gain):
    xf = x.astype(jnp.float32)
    y = xf * lax.rsqrt(jnp.mean(xf * xf, axis=-1, keepdims=True) + EPS)
    return (y * gain.astype(jnp.float32)).astype(x.dtype)


def modulate(h, shift, scale):
    return h * (1 + scale[:, None, :]) + shift[:, None, :]


def adaln(cond, w, b):
    return jnp.split(jax.nn.silu(cond) @ w + b, 6, axis=-1)


def to_col_major(a, rows):
    return a.reshape(a.shape[0], rows, GRID_W, *a.shape[2:]).swapaxes(1, 2).reshape(a.shape)


def from_col_major(a, rows):
    return a.reshape(a.shape[0], GRID_W, rows, *a.shape[2:]).swapaxes(1, 2).reshape(a.shape)


def scan_order(a, rows):
    return jnp.concatenate([a[:, :, :HALF_B], to_col_major(a[:, :, HALF_B:], rows)], axis=2)


def raster_order(a, rows):
    return jnp.concatenate([a[:, :, :HALF_B], from_col_major(a[:, :, HALF_B:], rows)], axis=2)


def gmlp_group(u, v, w_s, b_s, v_gain):
    n, t, _ = u.shape
    u = jax.nn.gelu(u).reshape(n, t, A_HEADS, A_HEAD_DIM)
    v = rms_norm(jax.nn.gelu(v).reshape(n, t, A_HEADS, A_HEAD_DIM), v_gain)
    vc = v.reshape(n, t // A_CHUNK, A_CHUNK, A_HEADS, A_HEAD_DIM)
    mixed = jnp.einsum('hts,bnshd->bnthd', w_s, vc) + b_s.T[None, None, :, :, None]
    return (u * mixed.reshape(u.shape)).reshape(n, t, A_WIDTH)


def hgrn2_chunk_scan(q, k, v, log_f):
    n, h, t, dk = q.shape
    dv = v.shape[-1]
    nc = t // B_CHUNK

    def to_chunks(a):
        return jnp.moveaxis(a.reshape(n, h, nc, B_CHUNK, a.shape[-1]), 2, 0)

    within_dir = jnp.tril(jnp.ones((B_CHUNK, B_CHUNK), bool))

    def step(state, inp):
        qi, ki, vi, gi = inp
        b = jnp.cumsum(gi, axis=2)
        o_inter = jnp.einsum('nhtk,nhkv->nhtv', qi * jnp.exp(b), state)
        diff = jnp.where(within_dir[None, None, :, :, None],
                         b[:, :, :, None, :] - b[:, :, None, :, :], -jnp.inf)
        scores = jnp.einsum('nhtk,nhsk,nhtsk->nhts', qi, ki, jnp.exp(diff))
        o_intra = jnp.einsum('nhts,nhsv->nhtv', scores, vi)
        b_last = b[:, :, -1:, :]
        new_state = (jnp.exp(b_last[:, :, 0, :])[..., None] * state
                     + jnp.einsum('nhsk,nhsv->nhkv', ki * jnp.exp(b_last - b), vi))
        return new_state, o_inter + o_intra

    s0 = jnp.zeros((n, h, dk, dv), jnp.float32)
    _, out = lax.scan(step, s0, tuple(map(to_chunks, (q, k, v, log_f))))
    return jnp.moveaxis(out, 0, 2).reshape(n, h, t, dv)


def hgrn2_group(lat, cx, lb, o_gain, rows, with_ctx):
    def heads(a):
        return a.reshape(a.shape[0], a.shape[1], B_HEADS, B_HEAD_DIM).astype(jnp.float32)

    flip = lambda a: jnp.flip(a, axis=1)
    q_l, i_l, ff_l, fb_l = (scan_order(heads(a), rows) for a in lat[:4])
    q_c, i_c, ff_c, fb_c = (heads(a) for a in cx[:4])
    n_ctx = q_c.shape[1]
    bsz = q_l.shape[0]
    fwd = [jnp.concatenate([a, b], axis=1) for a, b in ((q_c, q_l), (i_c, i_l), (ff_c, ff_l))]
    bwd = [jnp.concatenate([flip(a), flip(b)], axis=1) for a, b in ((q_c, q_l), (i_c, i_l), (fb_c, fb_l))]
    q, v, z = (jnp.stack([f_, b_]) for f_, b_ in zip(fwd, bwd))
    t = q.shape[2]
    lbd = lb.astype(jnp.float32).reshape(2, 1, 1, B_HEADS, B_HEAD_DIM)
    f = lbd + (1 - lbd) * jax.nn.sigmoid(z)
    k = 1 - f

    def to_scan(a):
        return jnp.moveaxis(a.reshape(2 * bsz, t, B_HEADS, a.shape[-1]), 1, 2)

    out = hgrn2_chunk_scan(to_scan(q), to_scan(k), to_scan(v), to_scan(jnp.log(f)))
    out = jnp.moveaxis(out, 2, 1).reshape(2, bsz, t, B_HEADS, B_HEAD_DIM)
    o_fwd, o_bwd = out[0], out[1]

    def readout(o, g):
        y = rms_norm(o, o_gain) * jax.nn.silu(heads(g))
        return y.reshape(y.shape[0], y.shape[1], B_WIDTH).astype(g.dtype)

    o_lat = raster_order(o_fwd[:, n_ctx:] + flip(o_bwd[:, n_ctx:]), rows)
    y_lat = readout(o_lat, lat[4])
    y_ctx = readout(o_fwd[:, :n_ctx] + flip(o_bwd[:, :n_ctx]), cx[4]) if with_ctx else None
    return y_lat, y_ctx


def token_mixer(h_lat, h_ctx, w_in, w_s, b_s, v_gain, lb, o_gain, rows, with_ctx):
    cuts = [A_WIDTH, 2 * A_WIDTH, 2 * A_WIDTH + B_WIDTH, 2 * A_WIDTH + 2 * B_WIDTH,
            2 * A_WIDTH + 3 * B_WIDTH, 2 * A_WIDTH + 4 * B_WIDTH]
    lat = jnp.split(h_lat @ w_in, cuts, axis=-1)
    cx = jnp.split(h_ctx @ w_in, cuts, axis=-1)
    a_lat = gmlp_group(lat[0], lat[1], w_s, b_s, v_gain)
    b_lat, b_ctx = hgrn2_group(lat[2:], cx[2:], lb, o_gain, rows, with_ctx)
    mix_lat = jnp.concatenate([a_lat, b_lat], axis=-1)
    if not with_ctx:
        return mix_lat, None
    a_ctx = gmlp_group(cx[0], cx[1], w_s, b_s, v_gain)
    return mix_lat, jnp.concatenate([a_ctx, b_ctx], axis=-1)


def squared_relu_mlp(h, w1, w2):
    return jnp.square(jax.nn.relu(h @ w1)) @ w2


def setup_inputs(seed: int = 0) -> dict:
    key = jax.random.key(seed)
    ks = jax.random.split(key, 19)
    nrm = jax.random.normal
    d = D_MODEL
    return {
        "x": nrm(ks[0], (BATCH, SEQ, d), jnp.float32),
        "c": nrm(ks[1], (BATCH, d), jnp.float32),
        "ctx": nrm(ks[2], (BATCH, CTX_LEN, d), jnp.float32),
        "c_ctx": nrm(ks[3], (d,), jnp.float32),
        "w_ada": nrm(ks[4], (DEPTH, d, 6 * d), jnp.float32) * d ** -0.5,
        "b_ada": 0.02 * nrm(ks[5], (DEPTH, 6 * d), jnp.float32),
        "norm1_gain": 1.0 + 0.05 * nrm(ks[6], (DEPTH, d), jnp.float32),
        "w_in": nrm(ks[7], (DEPTH, d, IN_COLS), jnp.float32) * d ** -0.5,
        "gmlp_w_s": nrm(ks[8], (DEPTH, A_HEADS, A_CHUNK, A_CHUNK), jnp.float32) * A_CHUNK ** -0.5,
        "gmlp_b_s": 1.0 + 0.05 * nrm(ks[9], (DEPTH, A_HEADS, A_CHUNK), jnp.float32),
        "gmlp_v_gain": 1.0 + 0.05 * nrm(ks[10], (DEPTH, A_HEADS, A_HEAD_DIM), jnp.float32),
        "hgrn_lb_logits": 0.1 * nrm(ks[11], (DEPTH + 1, 2, B_WIDTH), jnp.float32),
        "hgrn_o_gain": 1.0 + 0.05 * nrm(ks[12], (DEPTH, B_HEADS, B_HEAD_DIM), jnp.float32),
        "w_out": nrm(ks[13], (DEPTH, MIX_WIDTH, d), jnp.float32) * MIX_WIDTH ** -0.5,
        "norm2_gain": 1.0 + 0.05 * nrm(ks[14], (DEPTH, d), jnp.float32),
        "w_ff1": nrm(ks[15], (DEPTH, d, D_FF), jnp.float32) * d ** -0.5,
        "w_ff2": nrm(ks[16], (DEPTH, D_FF, d), jnp.float32) * D_FF ** -0.5,
        "final_gain": 1.0 + 0.05 * nrm(ks[17], (d,), jnp.float32),
    }


def reference(x, c, ctx, c_ctx, w_ada, b_ada, norm1_gain, w_in, gmlp_w_s, gmlp_b_s, gmlp_v_gain,
              hgrn_lb_logits, hgrn_o_gain, w_out, norm2_gain, w_ff1, w_ff2, final_gain):
    rows = x.shape[1] // GRID_W
    cum = jnp.cumsum(jax.nn.softmax(hgrn_lb_logits.astype(jnp.float32), axis=0), axis=0)
    lb_all = cum[1:] - cum[:1]
    for layer in range(DEPTH):
        with_ctx = layer < DEPTH - 1
        sh1, sc1, g1, sh2, sc2, g2 = adaln(c, w_ada[layer], b_ada[layer])
        csh1, csc1, cg1, csh2, csc2, cg2 = adaln(c_ctx[None, :], w_ada[layer], b_ada[layer])
        h_lat = modulate(rms_norm(x, norm1_gain[layer]), sh1, sc1)
        h_ctx = modulate(rms_norm(ctx, norm1_gain[layer]), csh1, csc1)
        mix_lat, mix_ctx = token_mixer(h_lat, h_ctx, w_in[layer], gmlp_w_s[layer], gmlp_b_s[layer],
                                       gmlp_v_gain[layer], lb_all[layer], hgrn_o_gain[layer],
                                       rows, with_ctx)
        x = x + g1[:, None, :] * (mix_lat @ w_out[layer])
        h = modulate(rms_norm(x, norm2_gain[layer]), sh2, sc2)
        x = x + g2[:, None, :] * squared_relu_mlp(h, w_ff1[layer], w_ff2[layer])
        if with_ctx:
            ctx = ctx + cg1[:, None, :] * (mix_ctx @ w_out[layer])
            hc = modulate(rms_norm(ctx, norm2_gain[layer]), csh2, csc2)
            ctx = ctx + cg2[:, None, :] * squared_relu_mlp(hc, w_ff1[layer], w_ff2[layer])
    return rms_norm(x, final_gain)
```

```python
import functools

import jax
import jax.numpy as jnp
from jax import lax
from jax.experimental import pallas as pl
from jax.experimental.pallas import tpu as pltpu

EPS = 1e-6
GRID_W = 64
HEADS = 8
HALF_HEADS = HEADS // 2
HEAD_DIM = 128
GROUP_W = HEADS * HEAD_DIM
HALF_W = HALF_HEADS * HEAD_DIM
GMLP_CHUNK = 128
SCAN_CHUNK = 64
LANES = 128
VMEM_LIMIT = 56 * 1024 * 1024

F32 = jnp.float32
BF16 = jnp.bfloat16


def _params(n_axes, vmem=VMEM_LIMIT):
    return pltpu.CompilerParams(dimension_semantics=("arbitrary",) * n_axes, vmem_limit_bytes=vmem)


def _sigmoid(x):
    return 1.0 / (1.0 + jnp.exp(-x))


def _gelu_tanh(x):
    return x * (0.5 * (1.0 + jnp.tanh(0.7978845608028654 * (x + 0.044715 * (x * x * x)))))


def _rms_scale(x):
    return lax.rsqrt(jnp.mean(x * x, axis=-1, keepdims=True) + EPS)


def _adaln_kernel(cond_t_ref, w_ref, b_ref, out_ref, s_ref):
    @pl.when(pl.program_id(0) == 0)
    def _():
        ct = cond_t_ref[...]
        s = ct * _sigmoid(ct)
        for r in range(2):
            s_ref[r] = jnp.broadcast_to(s[:, r:r + 1], s_ref.shape[1:])

    tn = w_ref.shape[1]
    for cb in range(tn // LANES):
        w = w_ref[:, cb * LANES:(cb + 1) * LANES]
        rows = [jnp.sum(w * s_ref[r], axis=0, keepdims=True) for r in range(2)]
        out_ref[:, cb * LANES:(cb + 1) * LANES] = (
            jnp.concatenate(rows, axis=0) + b_ref[:, cb * LANES:(cb + 1) * LANES])


def _adaln(cond_t, w_ada, b_ada):
    d, n = w_ada.shape
    tn = 1024
    return pl.pallas_call(
        _adaln_kernel,
        grid=(n // tn,),
        in_specs=[pl.BlockSpec((d, 2), lambda j: (0, 0)),
                  pl.BlockSpec((d, tn), lambda j: (0, j)),
                  pl.BlockSpec((1, tn), lambda j: (0, j))],
        out_specs=pl.BlockSpec((2, tn), lambda j: (0, j)),
        out_shape=jax.ShapeDtypeStruct((2, n), F32),
        scratch_shapes=[pltpu.VMEM((2, d, LANES), F32)],
        compiler_params=_params(1),
        name="adaln",
    )(cond_t, w_ada, b_ada)


def _cast_kernel(w_ref, o_ref):
    o_ref[...] = w_ref[...].astype(o_ref.dtype)


def _cast_bf16(w, name):
    k, n = w.shape
    tk = 256
    return pl.pallas_call(
        _cast_kernel,
        grid=(k // tk,),
        in_specs=[pl.BlockSpec((tk, n), lambda i: (i, 0))],
        out_specs=pl.BlockSpec((tk, n), lambda i: (i, 0)),
        out_shape=jax.ShapeDtypeStruct((k, n), BF16),
        compiler_params=_params(1),
        name=name,
    )(w)


def _lower_bounds(lbl_ref):
    l0, l1 = lbl_ref[0], lbl_ref[1]
    m = jnp.maximum(l0, l1)
    e0, e1 = jnp.exp(l0 - m), jnp.exp(l1 - m)
    return e1 / (e0 + e1)


def _gate_outputs(z, lb, lf_ref, k_ref):
    f = lb + (1.0 - lb) * _sigmoid(z)
    lf_ref[...] = jnp.log(f)
    k_ref[...] = (1.0 - f).astype(k_ref.dtype)


def _normed_input(x_ref, gain_ref, sh_ref, sc_ref, row):
    x = x_ref[...]
    h = x * _rms_scale(x) * gain_ref[...]
    h = h * (1.0 + sc_ref[row:row + 1, :]) + sh_ref[row:row + 1, :]
    return h.astype(BF16)


def _inproj_latent_kernel(x_ref, sh_ref, sc_ref, gain_ref, w_ref, ws_ref, bs_ref, vg_ref, lbl_ref,
                          a_ref, q_ref, i_ref, kf_ref, kb_ref, lff_ref, lfb_ref, sg_ref):
    hb = _normed_input(x_ref, gain_ref, sh_ref, sc_ref, 0)

    def proj(g):
        return jnp.dot(hb, w_ref[:, g * GROUP_W:(g + 1) * GROUP_W], preferred_element_type=F32)

    u = _gelu_tanh(proj(0))
    v = _gelu_tanh(proj(1))
    tm = u.shape[0]
    for h in range(HEADS):
        cs = slice(h * HEAD_DIM, (h + 1) * HEAD_DIM)
        vh = v[:, cs]
        vn = (vh * _rms_scale(vh) * vg_ref[h:h + 1, :]).astype(BF16)
        w_s = ws_ref[h].astype(BF16)
        for c in range(tm // GMLP_CHUNK):
            rs = slice(c * GMLP_CHUNK, (c + 1) * GMLP_CHUNK)
            mixed = jnp.dot(w_s, vn[rs], preferred_element_type=F32) + bs_ref[h]
            a_ref[rs, cs] = (u[rs, cs] * mixed).astype(a_ref.dtype)

    q_ref[...] = proj(2).astype(q_ref.dtype)
    i_ref[...] = proj(3).astype(i_ref.dtype)
    lb = _lower_bounds(lbl_ref)
    _gate_outputs(proj(4), lb[0:1, :], lff_ref, kf_ref)
    _gate_outputs(proj(5), lb[1:2, :], lfb_ref, kb_ref)
    g = proj(6)
    sg_ref[...] = (g * _sigmoid(g)).astype(sg_ref.dtype)


def _inproj_latent(x, mod, gain, w_bf, w_s, b_s_b, v_gain, lb_logits):
    t, d = x.shape
    tm = 256
    row = lambda i: (i, 0)
    const2 = lambda i: (0, 0)
    const3 = lambda i: (0, 0, 0)
    bf = lambda: jax.ShapeDtypeStruct((t, GROUP_W), BF16)
    f32 = lambda: jax.ShapeDtypeStruct((t, GROUP_W), F32)
    out_block = pl.BlockSpec((tm, GROUP_W), row)
    return pl.pallas_call(
        _inproj_latent_kernel,
        grid=(t // tm,),
        in_specs=[pl.BlockSpec((tm, d), row),
                  pl.BlockSpec((2, d), lambda i: (0, 0)),
                  pl.BlockSpec((2, d), lambda i: (0, 1)),
                  pl.BlockSpec((1, d), const2),
                  pl.BlockSpec(w_bf.shape, const2, pipeline_mode=pl.Buffered(1)),
                  pl.BlockSpec(w_s.shape, const3),
                  pl.BlockSpec(b_s_b.shape, const3),
                  pl.BlockSpec(v_gain.shape, const2),
                  pl.BlockSpec(lb_logits.shape, const3)],
        out_specs=[out_block] * 8,
        out_shape=[bf(), bf(), bf(), bf(), bf(), f32(), f32(), bf()],
        compiler_params=_params(1),
        name="inproj_latent",
    )(x, mod, mod, gain, w_bf, w_s, b_s_b, v_gain, lb_logits)


def _inproj_ctx_kernel(x_ref, sh_ref, sc_ref, gain_ref, w_ref, lbl_ref,
                       i_ref, kf_ref, kb_ref, lff_ref, lfb_ref):
    hb = _normed_input(x_ref, gain_ref, sh_ref, sc_ref, 1)

    def proj(g):
        return jnp.dot(hb, w_ref[:, g * GROUP_W:(g + 1) * GROUP_W], preferred_element_type=F32)

    i_ref[...] = proj(0).astype(i_ref.dtype)
    lb = _lower_bounds(lbl_ref)
    _gate_outputs(proj(1), lb[0:1, :], lff_ref, kf_ref)
    _gate_outputs(proj(2), lb[1:2, :], lfb_ref, kb_ref)


def _inproj_ctx(ctx, mod, gain, w_bf, lb_logits):
    t, d = ctx.shape
    const2 = lambda i: (0, 0)
    bf = lambda: jax.ShapeDtypeStruct((t, GROUP_W), BF16)
    f32 = lambda: jax.ShapeDtypeStruct((t, GROUP_W), F32)
    out_block = pl.BlockSpec((t, GROUP_W), const2)
    return pl.pallas_call(
        _inproj_ctx_kernel,
        grid=(1,),
        in_specs=[pl.BlockSpec((t, d), const2),
                  pl.BlockSpec((2, d), lambda i: (0, 0)),
                  pl.BlockSpec((2, d), lambda i: (0, 1)),
                  pl.BlockSpec((1, d), const2),
                  pl.BlockSpec((d, 3 * GROUP_W), lambda i: (0, 1)),
                  pl.BlockSpec(lb_logits.shape, lambda i: (0, 0, 0))],
        out_specs=[out_block] * 5,
        out_shape=[bf(), bf(), bf(), f32(), f32()],
        compiler_params=_params(1),
        name="inproj_ctx",
    )(ctx, mod, mod, gain, w_bf, lb_logits)


def _split3(x):
    hi = x.astype(BF16)
    r = x - hi.astype(F32)
    mid = r.astype(BF16)
    lo = (r - mid.astype(F32)).astype(BF16)
    return hi, mid, lo


def _scan_group(q_ref, v_ref, k_ref, lf_ref, state_ref, state_base, backward, o_ref):
    c = SCAN_CHUNK
    ti = lax.broadcasted_iota(jnp.int32, (c, c), 0)
    si = lax.broadcasted_iota(jnp.int32, (c, c), 1)
    visible = (si >= ti) if backward else (si <= ti)
    ones_tri = visible.astype(BF16)
    b = sum(jnp.dot(ones_tri, p, preferred_element_type=F32) for p in _split3(lf_ref[...]))
    mid_row = c // 2 - 1 if backward else c // 2
    last_row = 0 if backward else c - 1
    outs = []
    for h in range(HALF_HEADS):
        cs = slice(h * HEAD_DIM, (h + 1) * HEAD_DIM)
        bh = b[:, cs]
        b_mid = bh[mid_row:mid_row + 1, :]
        b_last = bh[last_row:last_row + 1, :]
        qh = q_ref[:, cs].astype(F32)
        kh = k_ref[:, cs].astype(F32)
        vh = v_ref[:, cs]
        q_in = (qh * jnp.exp(bh - b_mid)).astype(BF16)
        k_in = (kh * jnp.exp(b_mid - bh)).astype(BF16)
        q_st = (qh * jnp.exp(bh)).astype(BF16)
        k_st = (kh * jnp.exp(b_last - bh)).astype(BF16)
        scores = lax.dot_general(q_in, k_in, (((1,), (1,)), ((), ())), preferred_element_type=F32)
        scores = jnp.where(visible, scores, 0.0).astype(BF16)
        st = state_ref[state_base + h]
        o = jnp.dot(scores, vh, preferred_element_type=F32)
        o = o + lax.dot_general(q_st, st.astype(BF16), (((1,), (1,)), ((), ())),
                                preferred_element_type=F32)
        state_ref[state_base + h] = st * jnp.exp(b_last) + lax.dot_general(
            vh, k_st, (((0,), (0,)), ((), ())), preferred_element_type=F32)
        outs.append(o)
    if o_ref is not None:
        o_ref[...] = jnp.concatenate(outs, axis=1).astype(o_ref.dtype)


def _scan_kernel(*refs, emit_outputs):
    ins, rest = refs[:16], refs[16:]
    s0_ref = rest[0]
    if emit_outputs:
        o_refs, state_ref = rest[1:5], rest[5]
    else:
        o_refs, state_ref = (None,) * 4, rest[1]

    @pl.when(pl.program_id(0) == 0)
    def _():
        state_ref[...] = s0_ref[...]

    for g in range(4):
        q_ref, v_ref, k_ref, lf_ref = ins[4 * g:4 * g + 4]
        _scan_group(q_ref, v_ref, k_ref, lf_ref, state_ref, g * HALF_HEADS, g >= 2, o_refs[g])


def _scan(q, i, kf, kb, lff, lfb, s0, maps, n_steps, emit_outputs, view_shapes, name):
    blk = (SCAN_CHUNK, HALF_W)
    args, in_specs = [], []
    for g in range(4):
        k, lf = (kf, lff) if g < 2 else (kb, lfb)
        for a in (q, i, k, lf):
            args.append(a.reshape(view_shapes[g]))
            in_specs.append(pl.BlockSpec(blk, maps[g]))
    args.append(s0)
    state_spec = pl.BlockSpec(s0.shape, lambda s: (0, 0, 0))
    in_specs.append(state_spec)
    out_specs, out_shape = [], []
    if emit_outputs:
        t = q.shape[0]
        for g in range(4):
            rows = view_shapes[g][0]
            out_shape.append(jax.ShapeDtypeStruct((rows, t * HALF_W // rows), F32))
            out_specs.append(pl.BlockSpec(blk, maps[4 + g]))
    out_shape.append(jax.ShapeDtypeStruct(s0.shape, F32))
    out_specs.append(state_spec)
    return pl.pallas_call(
        functools.partial(_scan_kernel, emit_outputs=emit_outputs),
        grid=(n_steps,),
        in_specs=in_specs,
        out_specs=out_specs,
        out_shape=out_shape,
        compiler_params=_params(1),
        name=name,
    )(*args)


def _scan_ctx(i, kf, kb, lff, lfb):
    t = i.shape[0]
    n = t // SCAN_CHUNK
    maps = [lambda s: (s, 0), lambda s: (s, 1), lambda s: (n - 1 - s, 0), lambda s: (n - 1 - s, 1)]
    s0 = jnp.zeros((2 * HEADS, HEAD_DIM, HEAD_DIM), F32)
    (state,) = _scan(i, i, kf, kb, lff, lfb, s0, maps, n, False, [(t, GROUP_W)] * 4, "scan_ctx")
    return state


def _scan_latent(q, i, kf, kb, lff, lfb, s0):
    t = q.shape[0]
    rows = t // GRID_W
    n = t // SCAN_CHUNK
    rb = rows // SCAN_CHUNK
    col_in = lambda c: (c % rb, (c // rb) * 2 + 1)
    col_out = lambda c: (c % rb, c // rb)
    maps = [lambda s: (s, 0), lambda s: col_in(s),
            lambda s: (n - 1 - s, 0), lambda s: col_in(n - 1 - s),
            lambda s: (s, 0), lambda s: col_out(s),
            lambda s: (n - 1 - s, 0), lambda s: col_out(n - 1 - s)]
    views = [(t, GROUP_W), (rows, GRID_W * GROUP_W)] * 2
    o_fr, o_fc, o_br, o_bc, _ = _scan(q, i, kf, kb, lff, lfb, s0, maps, n, True, views, "scan_latent")
    return o_fr, o_fc.reshape(t, HALF_W), o_br, o_bc.reshape(t, HALF_W)


def _outproj_kernel(a_ref, ofr_ref, ofc_ref, obr_ref, obc_ref, sg_ref, x_ref, w_ref,
                    g1_ref, sh2_ref, sc2_ref, og_ref, n2g_ref, x1_ref, h2_ref):
    o = jnp.concatenate([ofr_ref[...] + obr_ref[...], ofc_ref[...] + obc_ref[...]], axis=1)
    ys = []
    for h in range(HEADS):
        cs = slice(h * HEAD_DIM, (h + 1) * HEAD_DIM)
        oh = o[:, cs]
        ys.append((oh * _rms_scale(oh) * og_ref[:, cs] * sg_ref[:, cs].astype(F32)).astype(BF16))
    y = jnp.concatenate(ys, axis=1)
    proj = jnp.dot(a_ref[...], w_ref[:GROUP_W, :], preferred_element_type=F32)
    proj = proj + jnp.dot(y, w_ref[GROUP_W:, :], preferred_element_type=F32)
    x1 = x_ref[...] + g1_ref[0:1, :] * proj
    x1_ref[...] = x1
    h2 = x1 * _rms_scale(x1) * n2g_ref[...]
    h2_ref[...] = (h2 * (1.0 + sc2_ref[0:1, :]) + sh2_ref[0:1, :]).astype(h2_ref.dtype)


def _outproj(a, o_fr, o_fc, o_br, o_bc, sg, x, w_bf, mod, o_gain, n2_gain):
    t, d = x.shape
    tm = 512
    row = lambda i: (i, 0)
    const2 = lambda i: (0, 0)
    half = pl.BlockSpec((tm, HALF_W), row)
    full = pl.BlockSpec((tm, GROUP_W), row)
    mod_col = lambda c: pl.BlockSpec((2, d), lambda i: (0, c))
    return pl.pallas_call(
        _outproj_kernel,
        grid=(t // tm,),
        in_specs=[full, half, half, half, half, full,
                  pl.BlockSpec((tm, d), row),
                  pl.BlockSpec(w_bf.shape, const2, pipeline_mode=pl.Buffered(1)),
                  mod_col(2), mod_col(3), mod_col(4),
                  pl.BlockSpec((1, GROUP_W), const2),
                  pl.BlockSpec((1, d), const2)],
        out_specs=[pl.BlockSpec((tm, d), row), pl.BlockSpec((tm, d), row)],
        out_shape=[jax.ShapeDtypeStruct((t, d), F32), jax.ShapeDtypeStruct((t, d), BF16)],
        compiler_params=_params(1),
        name="outproj",
    )(a, o_fr, o_fc, o_br, o_bc, sg, x, w_bf, mod, mod, mod, o_gain, n2_gain)


def _ffn_kernel(h_ref, x1_ref, w1_ref, w2_ref, g2_ref, fg_ref, out_ref):
    f = pl.program_id(1)
    @pl.when(f == 0)
    def _():
        out_ref[...] = jnp.zeros_like(out_ref)

    hid = jnp.dot(h_ref[...], w1_ref[...].astype(BF16), preferred_element_type=F32)
    hid = jnp.square(jnp.maximum(hid, 0.0)).astype(BF16)
    out_ref[...] += jnp.dot(hid, w2_ref[...].astype(BF16), preferred_element_type=F32)

    @pl.when(f == pl.num_programs(1) - 1)
    def _():
        x2 = x1_ref[...] + g2_ref[0:1, :] * out_ref[...]
        out_ref[...] = x2 * _rms_scale(x2) * fg_ref[...]


def _ffn(h2, x1, w1, w2, mod, final_gain):
    t, d = x1.shape
    d_ff = w1.shape[1]
    tm, tf = 1024, 512
    row = lambda i, f: (i, 0)
    return pl.pallas_call(
        _ffn_kernel,
        grid=(t // tm, d_ff // tf),
        in_specs=[pl.BlockSpec((tm, d), row),
                  pl.BlockSpec((tm, d), row, pipeline_mode=pl.Buffered(1)),
                  pl.BlockSpec((d, tf), lambda i, f: (0, f)),
                  pl.BlockSpec((tf, d), lambda i, f: (f, 0)),
                  pl.BlockSpec((2, d), lambda i, f: (0, 5)),
                  pl.BlockSpec((1, d), lambda i, f: (0, 0))],
        out_specs=pl.BlockSpec((tm, d), row, pipeline_mode=pl.Buffered(1)),
        out_shape=jax.ShapeDtypeStruct((t, d), F32),
        compiler_params=_params(2),
        name="ffn",
    )(h2, x1, w1, w2, mod, final_gain)


def kernel(x, c, ctx, c_ctx, w_ada, b_ada, norm1_gain, w_in, gmlp_w_s, gmlp_b_s, gmlp_v_gain,
           hgrn_lb_logits, hgrn_o_gain, w_out, norm2_gain, w_ff1, w_ff2, final_gain):
    bsz, t, d = x.shape
    assert bsz == 1 and w_ada.shape[0] == 1, "single sample, single layer"
    assert w_in.shape[2] == 7 * GROUP_W and t % (GRID_W * SCAN_CHUNK) == 0
    n_ctx = ctx.shape[1]

    cond_t = jnp.stack([c[0], c_ctx], axis=1)
    mod = _adaln(cond_t, w_ada[0], b_ada)
    w_in_bf = _cast_bf16(w_in[0], "cast_w_in")
    w_out_bf = _cast_bf16(w_out[0], "cast_w_out")

    b_s_b = jnp.broadcast_to(gmlp_b_s[0][:, :, None], (HEADS, GMLP_CHUNK, HEAD_DIM))
    a, q, i, kf, kb, lff, lfb, sg = _inproj_latent(
        x[0], mod, norm1_gain, w_in_bf, gmlp_w_s[0], b_s_b, gmlp_v_gain[0], hgrn_lb_logits)
    ci, ckf, ckb, clff, clfb = _inproj_ctx(ctx[0], mod, norm1_gain, w_in_bf, hgrn_lb_logits)

    s0 = _scan_ctx(ci, ckf, ckb, clff, clfb)
    o_fr, o_fc, o_br, o_bc = _scan_latent(q, i, kf, kb, lff, lfb, s0)

    x1, h2 = _outproj(a, o_fr, o_fc, o_br, o_bc, sg, x[0], w_out_bf, mod,
                      hgrn_o_gain.reshape(1, GROUP_W), norm2_gain)
    out = _ffn(h2, x1, w_ff1[0], w_ff2[0], mod, final_gain.reshape(1, d))
    del n_ctx
    return out[None]
```

```python
import functools

import jax
import jax.numpy as jnp
from jax import lax
from jax.experimental import pallas as pl
from jax.experimental.pallas import tpu as pltpu

EPS = 1e-6
GRID_W = 64
HEADS = 8
HALF_HEADS = HEADS // 2
HEAD_DIM = 128
GROUP_W = HEADS * HEAD_DIM
HALF_W = HALF_HEADS * HEAD_DIM
GMLP_CHUNK = 128
SCAN_CHUNK = 64
LANES = 128
SUBLANES = 8
VMEM_LIMIT = 56 * 1024 * 1024

F32 = jnp.float32
BF16 = jnp.bfloat16


def _params(n_axes, vmem=VMEM_LIMIT):
    return pltpu.CompilerParams(dimension_semantics=("arbitrary",) * n_axes, vmem_limit_bytes=vmem)


def _sigmoid(x):
    return 1.0 / (1.0 + jnp.exp(-x))


def _gelu_tanh(x):
    return x * (0.5 * (1.0 + jnp.tanh(0.7978845608028654 * (x + 0.044715 * (x * x * x)))))


def _rms_scale(x):
    return lax.rsqrt(jnp.mean(x * x, axis=-1, keepdims=True) + EPS)


def _adaln_kernel(cond_t_ref, w_ref, b_ref, out_ref, s_ref):
    @pl.when(pl.program_id(0) == 0)
    def _():
        ct = cond_t_ref[...]
        s = ct * _sigmoid(ct)
        for r in range(2):
            s_ref[r] = jnp.broadcast_to(s[:, r:r + 1], s_ref.shape[1:])

    tn = w_ref.shape[1]
    for cb in range(tn // LANES):
        w = w_ref[:, cb * LANES:(cb + 1) * LANES]
        rows = [jnp.sum(w * s_ref[r], axis=0, keepdims=True) for r in range(2)]
        out_ref[:, cb * LANES:(cb + 1) * LANES] = (
            jnp.concatenate(rows, axis=0) + b_ref[:, cb * LANES:(cb + 1) * LANES])


def _adaln(cond_t, w_ada, b_ada):
    d, n = w_ada.shape
    tn = 1024
    return pl.pallas_call(
        _adaln_kernel,
        grid=(n // tn,),
        in_specs=[pl.BlockSpec((d, 2), lambda j: (0, 0)),
                  pl.BlockSpec((d, tn), lambda j: (0, j)),
                  pl.BlockSpec((1, tn), lambda j: (0, j))],
        out_specs=pl.BlockSpec((2, tn), lambda j: (0, j)),
        out_shape=jax.ShapeDtypeStruct((2, n), F32),
        scratch_shapes=[pltpu.VMEM((2, d, LANES), F32)],
        compiler_params=_params(1),
        name="adaln",
    )(cond_t, w_ada, b_ada)


def _cast_kernel(w_ref, o_ref):
    o_ref[...] = w_ref[...].astype(o_ref.dtype)


def _cast_bf16(w, name):
    k, n = w.shape
    tk = 256
    return pl.pallas_call(
        _cast_kernel,
        grid=(k // tk,),
        in_specs=[pl.BlockSpec((tk, n), lambda i: (i, 0))],
        out_specs=pl.BlockSpec((tk, n), lambda i: (i, 0)),
        out_shape=jax.ShapeDtypeStruct((k, n), BF16),
        compiler_params=_params(1),
        name=name,
    )(w)


def _lower_bounds(lbl_ref):
    l0, l1 = lbl_ref[0], lbl_ref[1]
    m = jnp.maximum(l0, l1)
    e0, e1 = jnp.exp(l0 - m), jnp.exp(l1 - m)
    return e1 / (e0 + e1)


def _gate_outputs(z, lb, lf_ref, k_ref):
    f = lb + (1.0 - lb) * _sigmoid(z)
    lf_ref[...] = jnp.log(f)
    k_ref[...] = (1.0 - f).astype(k_ref.dtype)


def _normed_input(x, gain_ref, sh_ref, sc_ref, row):
    h = x * _rms_scale(x) * gain_ref[...]
    h = h * (1.0 + sc_ref[row:row + 1, :]) + sh_ref[row:row + 1, :]
    return h.astype(BF16)


def _inproj_raster_kernel(x_ref, sh_ref, sc_ref, gain_ref, w_ref, ws_ref, bs_ref, vg_ref, lbl_ref,
                          a_ref, q_ref, i_ref, kf_ref, kb_ref, lff_ref, lfb_ref, sg_ref):
    hb = _normed_input(x_ref[...], gain_ref, sh_ref, sc_ref, 0)

    def proj(col, width):
        return jnp.dot(hb, w_ref[:, col:col + width], preferred_element_type=F32)

    u = _gelu_tanh(proj(0, GROUP_W))
    v = _gelu_tanh(proj(GROUP_W, GROUP_W))
    tm = u.shape[0]
    for h in range(HEADS):
        cs = slice(h * HEAD_DIM, (h + 1) * HEAD_DIM)
        vh = v[:, cs]
        vn = (vh * _rms_scale(vh) * vg_ref[h:h + 1, :]).astype(BF16)
        w_s = ws_ref[h].astype(BF16)
        for c in range(tm // GMLP_CHUNK):
            rs = slice(c * GMLP_CHUNK, (c + 1) * GMLP_CHUNK)
            mixed = jnp.dot(w_s, vn[rs], preferred_element_type=F32) + bs_ref[h]
            a_ref[rs, cs] = (u[rs, cs] * mixed).astype(a_ref.dtype)

    q_ref[...] = proj(2 * GROUP_W, HALF_W).astype(q_ref.dtype)
    i_ref[...] = proj(3 * GROUP_W, HALF_W).astype(i_ref.dtype)
    lb = _lower_bounds(lbl_ref)
    _gate_outputs(proj(4 * GROUP_W, HALF_W), lb[0:1, :HALF_W], lff_ref, kf_ref)
    _gate_outputs(proj(5 * GROUP_W, HALF_W), lb[1:2, :HALF_W], lfb_ref, kb_ref)
    g = proj(6 * GROUP_W, GROUP_W)
    sg_ref[...] = (g * _sigmoid(g)).astype(sg_ref.dtype)


def _inproj_raster(x, mod, gain, w_bf, w_s, b_s_b, v_gain, lb_logits):
    t, d = x.shape
    tm = 256
    row = lambda i: (i, 0)
    const2 = lambda i: (0, 0)
    const3 = lambda i: (0, 0, 0)
    shape = lambda w, dt: jax.ShapeDtypeStruct((t, w), dt)
    full = pl.BlockSpec((tm, GROUP_W), row)
    half = pl.BlockSpec((tm, HALF_W), row)
    return pl.pallas_call(
        _inproj_raster_kernel,
        grid=(t // tm,),
        in_specs=[pl.BlockSpec((tm, d), row),
                  pl.BlockSpec((2, d), lambda i: (0, 0)),
                  pl.BlockSpec((2, d), lambda i: (0, 1)),
                  pl.BlockSpec((1, d), const2),
                  pl.BlockSpec(w_bf.shape, const2, pipeline_mode=pl.Buffered(1)),
                  pl.BlockSpec(w_s.shape, const3),
                  pl.BlockSpec(b_s_b.shape, const3),
                  pl.BlockSpec(v_gain.shape, const2),
                  pl.BlockSpec(lb_logits.shape, const3)],
        out_specs=[full, half, half, half, half, half, half, full],
        out_shape=[shape(GROUP_W, BF16), shape(HALF_W, BF16), shape(HALF_W, BF16), shape(HALF_W, BF16),
                   shape(HALF_W, BF16), shape(HALF_W, F32), shape(HALF_W, F32), shape(GROUP_W, BF16)],
        compiler_params=_params(1),
        name="inproj_raster",
    )(x, mod, mod, gain, w_bf, w_s, b_s_b, v_gain, lb_logits)


COLS_PER_STEP = 8


def _gather_copies(x_hbm_ref, buf_ref, sem_ref, slot, step):
    rows = x_hbm_ref.shape[0]
    return [pltpu.make_async_copy(x_hbm_ref.at[:, step * COLS_PER_STEP + wi, :],
                                  buf_ref.at[slot, pl.ds(wi * rows, rows), :],
                                  sem_ref.at[slot, wi])
            for wi in range(COLS_PER_STEP)]


def _inproj_colmajor_kernel(x_hbm_ref, sh_ref, sc_ref, gain_ref, wq_ref, wi_ref, wf_ref, wb_ref, lbl_ref,
                            q_ref, i_ref, kf_ref, kb_ref, lff_ref, lfb_ref, xbuf_ref, sem_ref):
    step, n_steps = pl.program_id(0), pl.num_programs(0)
    slot = step % 2
    copies = functools.partial(_gather_copies, x_hbm_ref, xbuf_ref, sem_ref)

    @pl.when(step == 0)
    def _():
        for cp in copies(slot, step):
            cp.start()

    @pl.when(step + 1 < n_steps)
    def _():
        for cp in copies(1 - slot, step + 1):
            cp.start()

    for cp in copies(slot, step):
        cp.wait()
    hb = _normed_input(xbuf_ref[slot], gain_ref, sh_ref, sc_ref, 0)
    q_ref[...] = jnp.dot(hb, wq_ref[...], preferred_element_type=F32).astype(q_ref.dtype)
    i_ref[...] = jnp.dot(hb, wi_ref[...], preferred_element_type=F32).astype(i_ref.dtype)
    lb = _lower_bounds(lbl_ref)
    _gate_outputs(jnp.dot(hb, wf_ref[...], preferred_element_type=F32), lb[0:1, HALF_W:], lff_ref, kf_ref)
    _gate_outputs(jnp.dot(hb, wb_ref[...], preferred_element_type=F32), lb[1:2, HALF_W:], lfb_ref, kb_ref)


def _inproj_colmajor(x, mod, gain, w_bf, lb_logits):
    t, d = x.shape
    rows = t // GRID_W
    tm = COLS_PER_STEP * rows
    row = lambda i: (i, 0)
    const2 = lambda i: (0, 0)
    shape = lambda dt: jax.ShapeDtypeStruct((t, HALF_W), dt)
    w_spec = lambda g: pl.BlockSpec((d, HALF_W), lambda i: (0, 2 * g + 1), pipeline_mode=pl.Buffered(1))
    out = pl.BlockSpec((tm, HALF_W), row)
    return pl.pallas_call(
        _inproj_colmajor_kernel,
        grid=(GRID_W // COLS_PER_STEP,),
        in_specs=[pl.BlockSpec(memory_space=pl.ANY),
                  pl.BlockSpec((2, d), lambda i: (0, 0)),
                  pl.BlockSpec((2, d), lambda i: (0, 1)),
                  pl.BlockSpec((1, d), const2),
                  w_spec(2), w_spec(3), w_spec(4), w_spec(5),
                  pl.BlockSpec(lb_logits.shape, lambda i: (0, 0, 0))],
        out_specs=[out] * 6,
        out_shape=[shape(BF16), shape(BF16), shape(BF16), shape(BF16), shape(F32), shape(F32)],
        scratch_shapes=[pltpu.VMEM((2, tm, d), F32), pltpu.SemaphoreType.DMA((2, COLS_PER_STEP))],
        compiler_params=_params(1),
        name="inproj_colmajor",
    )(x.reshape(rows, GRID_W, d), mod, mod, gain, w_bf, w_bf, w_bf, w_bf, lb_logits)


def _inproj_ctx_kernel(x_ref, sh_ref, sc_ref, gain_ref, w_ref, lbl_ref,
                       i_ref, kf_ref, kb_ref, lff_ref, lfb_ref):
    hb = _normed_input(x_ref[...], gain_ref, sh_ref, sc_ref, 1)

    def proj(g):
        return jnp.dot(hb, w_ref[:, g * GROUP_W:(g + 1) * GROUP_W], preferred_element_type=F32)

    i_ref[...] = proj(0).astype(i_ref.dtype)
    lb = _lower_bounds(lbl_ref)
    _gate_outputs(proj(1), lb[0:1, :], lff_ref, kf_ref)
    _gate_outputs(proj(2), lb[1:2, :], lfb_ref, kb_ref)


def _inproj_ctx(ctx, mod, gain, w_bf, lb_logits):
    t, d = ctx.shape
    const2 = lambda i: (0, 0)
    bf = lambda: jax.ShapeDtypeStruct((t, GROUP_W), BF16)
    f32 = lambda: jax.ShapeDtypeStruct((t, GROUP_W), F32)
    out_block = pl.BlockSpec((t, GROUP_W), const2)
    return pl.pallas_call(
        _inproj_ctx_kernel,
        grid=(1,),
        in_specs=[pl.BlockSpec((t, d), const2),
                  pl.BlockSpec((2, d), lambda i: (0, 0)),
                  pl.BlockSpec((2, d), lambda i: (0, 1)),
                  pl.BlockSpec((1, d), const2),
                  pl.BlockSpec((d, 3 * GROUP_W), lambda i: (0, 1)),
                  pl.BlockSpec(lb_logits.shape, lambda i: (0, 0, 0))],
        out_specs=[out_block] * 5,
        out_shape=[bf(), bf(), bf(), f32(), f32()],
        compiler_params=_params(1),
        name="inproj_ctx",
    )(ctx, mod, mod, gain, w_bf, lb_logits)


def _split3(x):
    hi = x.astype(BF16)
    r = x - hi.astype(F32)
    mid = r.astype(BF16)
    lo = (r - mid.astype(F32)).astype(BF16)
    return hi, mid, lo


def _scan_group(q_ref, v_ref, k_ref, lf_ref, state_ref, state_base, backward, o_ref):
    c = SCAN_CHUNK
    ti = lax.broadcasted_iota(jnp.int32, (c, c), 0)
    si = lax.broadcasted_iota(jnp.int32, (c, c), 1)
    visible = (si >= ti) if backward else (si <= ti)
    ones_tri = visible.astype(BF16)
    b = sum(jnp.dot(ones_tri, p, preferred_element_type=F32) for p in _split3(lf_ref[...]))
    mid_row = c // 2 - 1 if backward else c // 2
    last_row = 0 if backward else c - 1
    outs = []
    for h in range(HALF_HEADS):
        cs = slice(h * HEAD_DIM, (h + 1) * HEAD_DIM)
        bh = b[:, cs]
        b_mid = bh[mid_row:mid_row + 1, :]
        b_last = bh[last_row:last_row + 1, :]
        qh = q_ref[:, cs].astype(F32)
        kh = k_ref[:, cs].astype(F32)
        vh = v_ref[:, cs]
        q_in = (qh * jnp.exp(bh - b_mid)).astype(BF16)
        k_in = (kh * jnp.exp(b_mid - bh)).astype(BF16)
        q_st = (qh * jnp.exp(bh)).astype(BF16)
        k_st = (kh * jnp.exp(b_last - bh)).astype(BF16)
        scores = lax.dot_general(q_in, k_in, (((1,), (1,)), ((), ())), preferred_element_type=F32)
        scores = jnp.where(visible, scores, 0.0).astype(BF16)
        st = state_ref[state_base + h]
        o = jnp.dot(scores, vh, preferred_element_type=F32)
        o = o + lax.dot_general(q_st, st.astype(BF16), (((1,), (1,)), ((), ())),
                                preferred_element_type=F32)
        state_ref[state_base + h] = st * jnp.exp(b_last) + lax.dot_general(
            vh, k_st, (((0,), (0,)), ((), ())), preferred_element_type=F32)
        outs.append(o)
    if o_ref is not None:
        o_ref[...] = jnp.concatenate(outs, axis=1).astype(o_ref.dtype)


def _scatter_copy(buf_ref, out_hbm_ref, sem_ref, slot, chunk):
    per_col = out_hbm_ref.shape[0] // SCAN_CHUNK
    r0 = (chunk % per_col) * SCAN_CHUNK
    dst = out_hbm_ref.at[pl.ds(r0, SCAN_CHUNK), chunk // per_col, :]
    return pltpu.make_async_copy(buf_ref.at[slot], dst, sem_ref.at[slot])


def _scan_kernel(*refs, emit_outputs, n_casts):
    ins, s0_ref, rest = refs[:16], refs[16], refs[17:]
    cast_ins, rest = rest[:n_casts], rest[n_casts:]
    if emit_outputs:
        o_refs, rest = rest[:4], rest[4:]
    else:
        o_refs = (None,) * 4
    state_ref, cast_outs, rest = rest[0], rest[1:1 + n_casts], rest[1 + n_casts:]
    step, n_steps = pl.program_id(0), pl.num_programs(0)

    @pl.when(step == 0)
    def _():
        state_ref[...] = s0_ref[...]

    slot = step % 2
    for g in range(4):
        q_ref, v_ref, k_ref, lf_ref = ins[4 * g:4 * g + 4]
        scattered = emit_outputs and g % 2 == 1
        o_ref = rest[g // 2].at[slot] if scattered else o_refs[g]
        _scan_group(q_ref, v_ref, k_ref, lf_ref, state_ref, g * HALF_HEADS, g >= 2, o_ref)

    if emit_outputs:
        bufs, sems = rest[:2], rest[2:4]
        for d, backward in enumerate((False, True)):
            chunk_of = (lambda s: n_steps - 1 - s) if backward else (lambda s: s)
            copy = functools.partial(_scatter_copy, bufs[d], o_refs[2 * d + 1], sems[d])
            copy(slot, chunk_of(step)).start()

            @pl.when(step > 0)
            def _():
                copy(1 - slot, chunk_of(step - 1)).wait()

            @pl.when(step == n_steps - 1)
            def _():
                copy(slot, chunk_of(step)).wait()

    for w_ref, wb_ref in zip(cast_ins, cast_outs):
        wb_ref[...] = w_ref[...].astype(wb_ref.dtype)


def _scan(groups, s0, n_steps, out_maps, grid_rows, casts, name):
    blk = (SCAN_CHUNK, HALF_W)
    args, in_specs = [], []
    for arrays, index_map in groups:
        args.extend(arrays)
        in_specs.extend([pl.BlockSpec(blk, index_map)] * 4)
    args.append(s0)
    state_spec = pl.BlockSpec(s0.shape, lambda s: (0, 0, 0))
    in_specs.append(state_spec)
    cast_specs = [pl.BlockSpec((w.shape[0] // n_steps, w.shape[1]), lambda s: (s, 0)) for w in casts]
    args.extend(casts)
    in_specs.extend(cast_specs)
    out_specs, out_shape, scratch = [], [], []
    if out_maps is not None:
        raster = jax.ShapeDtypeStruct((grid_rows * GRID_W, HALF_W), F32)
        scattered = jax.ShapeDtypeStruct((grid_rows, GRID_W, HALF_W), F32)
        in_hbm = pl.BlockSpec(memory_space=pl.ANY)
        out_shape = [raster, scattered, raster, scattered]
        out_specs = [pl.BlockSpec(blk, out_maps[0]), in_hbm, pl.BlockSpec(blk, out_maps[1]), in_hbm]
        scratch = [pltpu.VMEM((2,) + blk, F32)] * 2 + [pltpu.SemaphoreType.DMA((2,))] * 2
    out_shape.append(jax.ShapeDtypeStruct(s0.shape, F32))
    out_specs.append(state_spec)
    out_shape.extend(jax.ShapeDtypeStruct(w.shape, BF16) for w in casts)
    out_specs.extend(cast_specs)
    return pl.pallas_call(
        functools.partial(_scan_kernel, emit_outputs=out_maps is not None, n_casts=len(casts)),
        grid=(n_steps,),
        in_specs=in_specs,
        out_specs=out_specs,
        out_shape=out_shape,
        scratch_shapes=scratch,
        compiler_params=_params(1),
        name=name,
    )(*args)


def _scan_ctx(i, kf, kb, lff, lfb):
    n = i.shape[0] // SCAN_CHUNK
    fwd, bwd = (i, i, kf, lff), (i, i, kb, lfb)
    groups = [(fwd, lambda s: (s, 0)), (fwd, lambda s: (s, 1)),
              (bwd, lambda s: (n - 1 - s, 0)), (bwd, lambda s: (n - 1 - s, 1))]
    s0 = jnp.zeros((2 * HEADS, HEAD_DIM, HEAD_DIM), F32)
    (state,) = _scan(groups, s0, n, None, 0, [], "scan_ctx")
    return state


def _scan_latent(raster, colmajor, s0, casts):
    t = raster[0].shape[0]
    n = t // SCAN_CHUNK
    up, down = (lambda s: (s, 0)), (lambda s: (n - 1 - s, 0))
    groups = []
    for backward in (False, True):
        for q, i, kf, kb, lff, lfb in (raster, colmajor):
            arrays = (q, i, kb, lfb) if backward else (q, i, kf, lff)
            groups.append((arrays, down if backward else up))
    o_fr, o_fc, o_br, o_bc, _, *casted = _scan(groups, s0, n, [up, down], t // GRID_W, casts, "scan_latent")
    return (o_fr, o_fc.reshape(t, HALF_W), o_br, o_bc.reshape(t, HALF_W)), casted


def _outproj_kernel(a_ref, ofr_ref, ofc_ref, obr_ref, obc_ref, sg_ref, x_ref, w_ref,
                    g1_ref, sh2_ref, sc2_ref, og_ref, n2g_ref, x1_ref, h2_ref):
    o_r = ofr_ref[...] + obr_ref[...]
    o_c = ofc_ref[...] + obc_ref[...]
    ys = []
    for h in range(HEADS):
        o = o_r if h < HALF_HEADS else o_c
        oh = o[:, (h % HALF_HEADS) * HEAD_DIM:(h % HALF_HEADS + 1) * HEAD_DIM]
        cs = slice(h * HEAD_DIM, (h + 1) * HEAD_DIM)
        ys.append((oh * _rms_scale(oh) * og_ref[:, cs] * sg_ref[:, cs].astype(F32)).astype(BF16))
    y = jnp.concatenate(ys, axis=1)
    proj = jnp.dot(a_ref[...], w_ref[:GROUP_W, :], preferred_element_type=F32)
    proj = proj + jnp.dot(y, w_ref[GROUP_W:, :], preferred_element_type=F32)
    x1 = x_ref[...] + g1_ref[0:1, :] * proj
    x1_ref[...] = x1
    h2 = x1 * _rms_scale(x1) * n2g_ref[...]
    h2_ref[...] = (h2 * (1.0 + sc2_ref[0:1, :]) + sh2_ref[0:1, :]).astype(h2_ref.dtype)


def _outproj(a, o_fr, o_fc, o_br, o_bc, sg, x, w_bf, mod, o_gain, n2_gain):
    t, d = x.shape
    tm = 512
    row = lambda i: (i, 0)
    const2 = lambda i: (0, 0)
    half = pl.BlockSpec((tm, HALF_W), row)
    full = pl.BlockSpec((tm, GROUP_W), row)
    mod_col = lambda c: pl.BlockSpec((2, d), lambda i: (0, c))
    return pl.pallas_call(
        _outproj_kernel,
        grid=(t // tm,),
        in_specs=[full, half, half, half, half, full,
                  pl.BlockSpec((tm, d), row),
                  pl.BlockSpec(w_bf.shape, const2, pipeline_mode=pl.Buffered(1)),
                  mod_col(2), mod_col(3), mod_col(4),
                  pl.BlockSpec((1, GROUP_W), const2),
                  pl.BlockSpec((1, d), const2)],
        out_specs=[pl.BlockSpec((tm, d), row), pl.BlockSpec((tm, d), row)],
        out_shape=[jax.ShapeDtypeStruct((t, d), F32), jax.ShapeDtypeStruct((t, d), BF16)],
        compiler_params=_params(1),
        name="outproj",
    )(a, o_fr, o_fc, o_br, o_bc, sg, x, w_bf, mod, mod, mod, o_gain, n2_gain)


def _ffn_kernel(h_ref, x1_ref, w1_ref, w2_ref, g2_ref, fg_ref, out_ref):
    f = pl.program_id(1)

    @pl.when(f == 0)
    def _():
        out_ref[...] = jnp.zeros_like(out_ref)

    hid = jnp.dot(h_ref[...], w1_ref[...], preferred_element_type=F32)
    hid = jnp.square(jnp.maximum(hid, 0.0)).astype(BF16)
    out_ref[...] += jnp.dot(hid, w2_ref[...], preferred_element_type=F32)

    @pl.when(f == pl.num_programs(1) - 1)
    def _():
        x2 = x1_ref[...] + g2_ref[0:1, :] * out_ref[...]
        out_ref[...] = x2 * _rms_scale(x2) * fg_ref[...]


def _ffn(h2, x1, w1_bf, w2_bf, mod, final_gain):
    t, d = x1.shape
    d_ff = w1_bf.shape[1]
    tm, tf = 1024, 1024
    row = lambda i, f: (i, 0)
    return pl.pallas_call(
        _ffn_kernel,
        grid=(t // tm, d_ff // tf),
        in_specs=[pl.BlockSpec((tm, d), row),
                  pl.BlockSpec((tm, d), row, pipeline_mode=pl.Buffered(1)),
                  pl.BlockSpec((d, tf), lambda i, f: (0, f)),
                  pl.BlockSpec((tf, d), lambda i, f: (f, 0)),
                  pl.BlockSpec((2, d), lambda i, f: (0, 5)),
                  pl.BlockSpec((1, d), lambda i, f: (0, 0))],
        out_specs=pl.BlockSpec((tm, d), row, pipeline_mode=pl.Buffered(1)),
        out_shape=jax.ShapeDtypeStruct((t, d), F32),
        compiler_params=_params(2),
        name="ffn",
    )(h2, x1, w1_bf, w2_bf, mod, final_gain)


def kernel(x, c, ctx, c_ctx, w_ada, b_ada, norm1_gain, w_in, gmlp_w_s, gmlp_b_s, gmlp_v_gain,
           hgrn_lb_logits, hgrn_o_gain, w_out, norm2_gain, w_ff1, w_ff2, final_gain):
    bsz, t, d = x.shape
    assert bsz == 1 and w_ada.shape[0] == 1, "single sample, single layer"
    assert w_in.shape[2] == 7 * GROUP_W and t % (GRID_W * SCAN_CHUNK) == 0

    cond_t = jnp.stack([c[0], c_ctx], axis=1)
    mod = _adaln(cond_t, w_ada[0], b_ada)
    w_in_bf = _cast_bf16(w_in[0], "cast_w_in")

    b_s_b = jnp.broadcast_to(gmlp_b_s[0][:, :, None], (HEADS, GMLP_CHUNK, HEAD_DIM))
    a, *raster, sg = _inproj_raster(
        x[0], mod, norm1_gain, w_in_bf, gmlp_w_s[0], b_s_b, gmlp_v_gain[0], hgrn_lb_logits)
    colmajor = _inproj_colmajor(x[0], mod, norm1_gain, w_in_bf, hgrn_lb_logits)
    ci, ckf, ckb, clff, clfb = _inproj_ctx(ctx[0], mod, norm1_gain, w_in_bf, hgrn_lb_logits)

    s0 = _scan_ctx(ci, ckf, ckb, clff, clfb)
    (o_fr, o_fc, o_br, o_bc), (w_out_bf, w1_bf, w2_bf) = _scan_latent(
        raster, colmajor, s0, [w_out[0], w_ff1[0], w_ff2[0]])

    x1, h2 = _outproj(a, o_fr, o_fc, o_br, o_bc, sg, x[0], w_out_bf, mod,
                      hgrn_o_gain.reshape(1, GROUP_W), norm2_gain)
    out = _ffn(h2, x1, w1_bf, w2_bf, mod, final_gain.reshape(1, d))
    return out[None]
```

```python
import functools

import jax
import jax.numpy as jnp
from jax import lax
from jax.experimental import pallas as pl
from jax.experimental.pallas import tpu as pltpu

EPS = 1e-6
GRID_W = 64
HEADS = 8
HALF_HEADS = HEADS // 2
HEAD_DIM = 128
GROUP_W = HEADS * HEAD_DIM
HALF_W = HALF_HEADS * HEAD_DIM
GMLP_CHUNK = 128
SCAN_CHUNK = 128
LANES = 128
SUBLANES = 8
VMEM_LIMIT = 56 * 1024 * 1024

F32 = jnp.float32
BF16 = jnp.bfloat16


def _params(n_axes, vmem=VMEM_LIMIT):
    return pltpu.CompilerParams(dimension_semantics=("arbitrary",) * n_axes, vmem_limit_bytes=vmem)


def _sigmoid(x):
    return 1.0 / (1.0 + jnp.exp(-x))


def _gelu_tanh(x):
    return x * (0.5 * (1.0 + jnp.tanh(0.7978845608028654 * (x + 0.044715 * (x * x * x)))))


def _rms_scale(x):
    return lax.rsqrt(jnp.mean(x * x, axis=-1, keepdims=True) + EPS)


def _adaln_kernel(cond_t_ref, w_ref, b_ref, out_ref, s_ref):
    @pl.when(pl.program_id(0) == 0)
    def _():
        ct = cond_t_ref[...]
        s = ct * _sigmoid(ct)
        for r in range(2):
            s_ref[r] = jnp.broadcast_to(s[:, r:r + 1], s_ref.shape[1:])

    tn = w_ref.shape[1]
    for cb in range(tn // LANES):
        w = w_ref[:, cb * LANES:(cb + 1) * LANES]
        rows = [jnp.sum(w * s_ref[r], axis=0, keepdims=True) for r in range(2)]
        out_ref[:, cb * LANES:(cb + 1) * LANES] = (
            jnp.concatenate(rows, axis=0) + b_ref[:, cb * LANES:(cb + 1) * LANES])


def _adaln(cond_t, w_ada, b_ada):
    d, n = w_ada.shape
    tn = 1024
    return pl.pallas_call(
        _adaln_kernel,
        grid=(n // tn,),
        in_specs=[pl.BlockSpec((d, 2), lambda j: (0, 0)),
                  pl.BlockSpec((d, tn), lambda j: (0, j)),
                  pl.BlockSpec((1, tn), lambda j: (0, j))],
        out_specs=pl.BlockSpec((2, tn), lambda j: (0, j)),
        out_shape=jax.ShapeDtypeStruct((2, n), F32),
        scratch_shapes=[pltpu.VMEM((2, d, LANES), F32)],
        compiler_params=_params(1),
        name="adaln",
    )(cond_t, w_ada, b_ada)


def _cast_kernel(w_ref, o_ref):
    o_ref[...] = w_ref[...].astype(o_ref.dtype)


def _cast_bf16(w, name):
    k, n = w.shape
    tk = 256
    return pl.pallas_call(
        _cast_kernel,
        grid=(k // tk,),
        in_specs=[pl.BlockSpec((tk, n), lambda i: (i, 0))],
        out_specs=pl.BlockSpec((tk, n), lambda i: (i, 0)),
        out_shape=jax.ShapeDtypeStruct((k, n), BF16),
        compiler_params=_params(1),
        name=name,
    )(w)


def _lower_bounds(lbl_ref):
    l0, l1 = lbl_ref[0], lbl_ref[1]
    m = jnp.maximum(l0, l1)
    e0, e1 = jnp.exp(l0 - m), jnp.exp(l1 - m)
    return e1 / (e0 + e1)


def _gate_outputs(z, lb, lf_ref, k_ref):
    f = lb + (1.0 - lb) * _sigmoid(z)
    lf_ref[...] = jnp.log(f)
    k_ref[...] = (1.0 - f).astype(k_ref.dtype)


def _normed_input(x, gain_ref, sh_ref, sc_ref, row):
    h = x * _rms_scale(x) * gain_ref[...]
    h = h * (1.0 + sc_ref[row:row + 1, :]) + sh_ref[row:row + 1, :]
    return h.astype(BF16)


def _inproj_raster_kernel(x_ref, sh_ref, sc_ref, gain_ref, w_ref, ws_ref, bs_ref, vg_ref, lbl_ref,
                          a_ref, q_ref, i_ref, kf_ref, kb_ref, lff_ref, lfb_ref, sg_ref):
    hb = _normed_input(x_ref[...], gain_ref, sh_ref, sc_ref, 0)

    def proj(col, width):
        return jnp.dot(hb, w_ref[:, col:col + width], preferred_element_type=F32)

    u = _gelu_tanh(proj(0, GROUP_W))
    v = _gelu_tanh(proj(GROUP_W, GROUP_W))
    tm = u.shape[0]
    for h in range(HEADS):
        cs = slice(h * HEAD_DIM, (h + 1) * HEAD_DIM)
        vh = v[:, cs]
        vn = (vh * _rms_scale(vh) * vg_ref[h:h + 1, :]).astype(BF16)
        w_s = ws_ref[h].astype(BF16)
        for c in range(tm // GMLP_CHUNK):
            rs = slice(c * GMLP_CHUNK, (c + 1) * GMLP_CHUNK)
            mixed = jnp.dot(w_s, vn[rs], preferred_element_type=F32) + bs_ref[h]
            a_ref[rs, cs] = (u[rs, cs] * mixed).astype(a_ref.dtype)

    q_ref[...] = proj(2 * GROUP_W, HALF_W).astype(q_ref.dtype)
    i_ref[...] = proj(3 * GROUP_W, HALF_W).astype(i_ref.dtype)
    lb = _lower_bounds(lbl_ref)
    _gate_outputs(proj(4 * GROUP_W, HALF_W), lb[0:1, :HALF_W], lff_ref, kf_ref)
    _gate_outputs(proj(5 * GROUP_W, HALF_W), lb[1:2, :HALF_W], lfb_ref, kb_ref)
    g = proj(6 * GROUP_W, GROUP_W)
    sg_ref[...] = (g * _sigmoid(g)).astype(sg_ref.dtype)


def _inproj_raster(x, mod, gain, w_bf, w_s, b_s_b, v_gain, lb_logits):
    t, d = x.shape
    tm = 256
    row = lambda i: (i, 0)
    const2 = lambda i: (0, 0)
    const3 = lambda i: (0, 0, 0)
    shape = lambda w, dt: jax.ShapeDtypeStruct((t, w), dt)
    full = pl.BlockSpec((tm, GROUP_W), row)
    half = pl.BlockSpec((tm, HALF_W), row)
    return pl.pallas_call(
        _inproj_raster_kernel,
        grid=(t // tm,),
        in_specs=[pl.BlockSpec((tm, d), row),
                  pl.BlockSpec((2, d), lambda i: (0, 0)),
                  pl.BlockSpec((2, d), lambda i: (0, 1)),
                  pl.BlockSpec((1, d), const2),
                  pl.BlockSpec(w_bf.shape, const2, pipeline_mode=pl.Buffered(1)),
                  pl.BlockSpec(w_s.shape, const3),
                  pl.BlockSpec(b_s_b.shape, const3),
                  pl.BlockSpec(v_gain.shape, const2),
                  pl.BlockSpec(lb_logits.shape, const3)],
        out_specs=[full, half, half, half, half, half, half, full],
        out_shape=[shape(GROUP_W, BF16), shape(HALF_W, BF16), shape(HALF_W, BF16), shape(HALF_W, BF16),
                   shape(HALF_W, BF16), shape(HALF_W, F32), shape(HALF_W, F32), shape(GROUP_W, BF16)],
        compiler_params=_params(1),
        name="inproj_raster",
    )(x, mod, mod, gain, w_bf, w_s, b_s_b, v_gain, lb_logits)


COLS_PER_STEP = 8


def _gather_copies(x_hbm_ref, buf_ref, sem_ref, slot, step):
    rows = x_hbm_ref.shape[0]
    return [pltpu.make_async_copy(x_hbm_ref.at[:, step * COLS_PER_STEP + wi, :],
                                  buf_ref.at[slot, pl.ds(wi * rows, rows), :],
                                  sem_ref.at[slot, wi])
            for wi in range(COLS_PER_STEP)]


def _inproj_colmajor_kernel(x_hbm_ref, sh_ref, sc_ref, gain_ref, wq_ref, wi_ref, wf_ref, wb_ref, lbl_ref,
                            q_ref, i_ref, kf_ref, kb_ref, lff_ref, lfb_ref, xbuf_ref, sem_ref):
    step, n_steps = pl.program_id(0), pl.num_programs(0)
    slot = step % 2
    copies = functools.partial(_gather_copies, x_hbm_ref, xbuf_ref, sem_ref)

    @pl.when(step == 0)
    def _():
        for cp in copies(slot, step):
            cp.start()

    @pl.when(step + 1 < n_steps)
    def _():
        for cp in copies(1 - slot, step + 1):
            cp.start()

    for cp in copies(slot, step):
        cp.wait()
    hb = _normed_input(xbuf_ref[slot], gain_ref, sh_ref, sc_ref, 0)
    q_ref[...] = jnp.dot(hb, wq_ref[...], preferred_element_type=F32).astype(q_ref.dtype)
    i_ref[...] = jnp.dot(hb, wi_ref[...], preferred_element_type=F32).astype(i_ref.dtype)
    lb = _lower_bounds(lbl_ref)
    _gate_outputs(jnp.dot(hb, wf_ref[...], preferred_element_type=F32), lb[0:1, HALF_W:], lff_ref, kf_ref)
    _gate_outputs(jnp.dot(hb, wb_ref[...], preferred_element_type=F32), lb[1:2, HALF_W:], lfb_ref, kb_ref)


def _inproj_colmajor(x, mod, gain, w_bf, lb_logits):
    t, d = x.shape
    rows = t // GRID_W
    tm = COLS_PER_STEP * rows
    row = lambda i: (i, 0)
    const2 = lambda i: (0, 0)
    shape = lambda dt: jax.ShapeDtypeStruct((t, HALF_W), dt)
    w_spec = lambda g: pl.BlockSpec((d, HALF_W), lambda i: (0, 2 * g + 1), pipeline_mode=pl.Buffered(1))
    out = pl.BlockSpec((tm, HALF_W), row)
    return pl.pallas_call(
        _inproj_colmajor_kernel,
        grid=(GRID_W // COLS_PER_STEP,),
        in_specs=[pl.BlockSpec(memory_space=pl.ANY),
                  pl.BlockSpec((2, d), lambda i: (0, 0)),
                  pl.BlockSpec((2, d), lambda i: (0, 1)),
                  pl.BlockSpec((1, d), const2),
                  w_spec(2), w_spec(3), w_spec(4), w_spec(5),
                  pl.BlockSpec(lb_logits.shape, lambda i: (0, 0, 0))],
        out_specs=[out] * 6,
        out_shape=[shape(BF16), shape(BF16), shape(BF16), shape(BF16), shape(F32), shape(F32)],
        scratch_shapes=[pltpu.VMEM((2, tm, d), F32), pltpu.SemaphoreType.DMA((2, COLS_PER_STEP))],
        compiler_params=_params(1),
        name="inproj_colmajor",
    )(x.reshape(rows, GRID_W, d), mod, mod, gain, w_bf, w_bf, w_bf, w_bf, lb_logits)


def _inproj_ctx_kernel(x_ref, sh_ref, sc_ref, gain_ref, w_ref, lbl_ref,
                       i_ref, kf_ref, kb_ref, lff_ref, lfb_ref):
    hb = _normed_input(x_ref[...], gain_ref, sh_ref, sc_ref, 1)

    def proj(g):
        return jnp.dot(hb, w_ref[:, g * GROUP_W:(g + 1) * GROUP_W], preferred_element_type=F32)

    i_ref[...] = proj(0).astype(i_ref.dtype)
    lb = _lower_bounds(lbl_ref)
    _gate_outputs(proj(1), lb[0:1, :], lff_ref, kf_ref)
    _gate_outputs(proj(2), lb[1:2, :], lfb_ref, kb_ref)


def _inproj_ctx(ctx, mod, gain, w_bf, lb_logits):
    t, d = ctx.shape
    const2 = lambda i: (0, 0)
    bf = lambda: jax.ShapeDtypeStruct((t, GROUP_W), BF16)
    f32 = lambda: jax.ShapeDtypeStruct((t, GROUP_W), F32)
    out_block = pl.BlockSpec((t, GROUP_W), const2)
    return pl.pallas_call(
        _inproj_ctx_kernel,
        grid=(1,),
        in_specs=[pl.BlockSpec((t, d), const2),
                  pl.BlockSpec((2, d), lambda i: (0, 0)),
                  pl.BlockSpec((2, d), lambda i: (0, 1)),
                  pl.BlockSpec((1, d), const2),
                  pl.BlockSpec((d, 3 * GROUP_W), lambda i: (0, 1)),
                  pl.BlockSpec(lb_logits.shape, lambda i: (0, 0, 0))],
        out_specs=[out_block] * 5,
        out_shape=[bf(), bf(), bf(), f32(), f32()],
        compiler_params=_params(1),
        name="inproj_ctx",
    )(ctx, mod, mod, gain, w_bf, lb_logits)


def _split3(x):
    hi = x.astype(BF16)
    r = x - hi.astype(F32)
    mid = r.astype(BF16)
    lo = (r - mid.astype(F32)).astype(BF16)
    return hi, mid, lo


def _visible(backward):
    ti = lax.broadcasted_iota(jnp.int32, (SCAN_CHUNK, SCAN_CHUNK), 0)
    si = lax.broadcasted_iota(jnp.int32, (SCAN_CHUNK, SCAN_CHUNK), 1)
    return (si >= ti) if backward else (si <= ti)


def _scan_chunk(ins, state_ref, o_refs):
    c = SCAN_CHUNK
    units = []
    for g in range(4):
        q_ref, v_ref, k_ref, lf_ref = ins[4 * g:4 * g + 4]
        backward = g >= 2
        visible = _visible(backward)
        ones_tri = visible.astype(BF16)
        b = sum(jnp.dot(ones_tri, p, preferred_element_type=F32) for p in _split3(lf_ref[...]))
        mid_row = c // 2 - 1 if backward else c // 2
        last_row = 0 if backward else c - 1
        for h in range(HALF_HEADS):
            cs = slice(h * HEAD_DIM, (h + 1) * HEAD_DIM)
            bh = b[:, cs]
            units.append(dict(u=g * HALF_HEADS + h, visible=visible, bh=bh,
                              b_mid=bh[mid_row:mid_row + 1, :], b_last=bh[last_row:last_row + 1, :],
                              q_ref=q_ref.at[:, cs], k_ref=k_ref.at[:, cs], v_ref=v_ref.at[:, cs]))
    for n in units:
        qh = n["q_ref"][...].astype(F32)
        kh = n["k_ref"][...].astype(F32)
        n["q_in"] = (qh * jnp.exp(n["bh"] - n["b_mid"])).astype(BF16)
        n["k_in"] = (kh * jnp.exp(n["b_mid"] - n["bh"])).astype(BF16)
        n["q_st"] = (qh * jnp.exp(n["bh"])).astype(BF16)
        n["k_st"] = (kh * jnp.exp(n["b_last"] - n["bh"])).astype(BF16)
    for n in units:
        scores = lax.dot_general(n["q_in"], n["k_in"], (((1,), (1,)), ((), ())),
                                 preferred_element_type=F32)
        n["scores"] = jnp.where(n["visible"], scores, 0.0).astype(BF16)
    outs = []
    for n in units:
        st = state_ref[n["u"]]
        vh = n["v_ref"][...]
        outs.append(jnp.dot(jnp.concatenate([n["scores"], n["q_st"]], axis=1),
                            jnp.concatenate([vh, st.astype(BF16)], axis=0), preferred_element_type=F32))
        decay = jnp.broadcast_to(jnp.exp(n["b_last"]), (HEAD_DIM, HEAD_DIM)).T
        state_ref[n["u"]] = st * decay + lax.dot_general(
            n["k_st"], vh, (((0,), (0,)), ((), ())), preferred_element_type=F32)
    for g, o_ref in enumerate(o_refs):
        if o_ref is not None:
            o_ref[...] = jnp.concatenate(outs[g * HALF_HEADS:(g + 1) * HALF_HEADS], axis=1)


def _scatter_copy(buf_ref, out_hbm_ref, sem_ref, slot, chunk):
    per_col = out_hbm_ref.shape[0] // SCAN_CHUNK
    r0 = (chunk % per_col) * SCAN_CHUNK
    dst = out_hbm_ref.at[pl.ds(r0, SCAN_CHUNK), chunk // per_col, :]
    return pltpu.make_async_copy(buf_ref.at[slot], dst, sem_ref.at[slot])


def _scan_kernel(*refs, emit_outputs, n_casts):
    ins, s0_ref, rest = refs[:16], refs[16], refs[17:]
    cast_ins, rest = rest[:n_casts], rest[n_casts:]
    if emit_outputs:
        o_refs, rest = rest[:4], rest[4:]
    else:
        o_refs = (None,) * 4
    state_ref, cast_outs, rest = rest[0], rest[1:1 + n_casts], rest[1 + n_casts:]
    step, n_steps = pl.program_id(0), pl.num_programs(0)

    @pl.when(step == 0)
    def _():
        state_ref[...] = s0_ref[...]

    slot = step % 2
    if emit_outputs:
        _scan_chunk(ins, state_ref, [o_refs[0], rest[0].at[slot], o_refs[2], rest[1].at[slot]])
    else:
        _scan_chunk(ins, state_ref, o_refs)

    if emit_outputs:
        bufs, sems = rest[:2], rest[2:4]
        for d, backward in enumerate((False, True)):
            chunk_of = (lambda s: n_steps - 1 - s) if backward else (lambda s: s)
            copy = functools.partial(_scatter_copy, bufs[d], o_refs[2 * d + 1], sems[d])
            copy(slot, chunk_of(step)).start()

            @pl.when(step > 0)
            def _():
                copy(1 - slot, chunk_of(step - 1)).wait()

            @pl.when(step == n_steps - 1)
            def _():
                copy(slot, chunk_of(step)).wait()

    for w_ref, wb_ref in zip(cast_ins, cast_outs):
        wb_ref[...] = w_ref[...].astype(wb_ref.dtype)


def _scan(groups, s0, n_steps, out_maps, grid_rows, casts, name):
    blk = (SCAN_CHUNK, HALF_W)
    args, in_specs = [], []
    for arrays, index_map in groups:
        args.extend(arrays)
        in_specs.extend([pl.BlockSpec(blk, index_map)] * 4)
    args.append(s0)
    state_spec = pl.BlockSpec(s0.shape, lambda s: (0, 0, 0))
    in_specs.append(state_spec)
    cast_specs = [pl.BlockSpec((w.shape[0] // n_steps, w.shape[1]), lambda s: (s, 0)) for w in casts]
    args.extend(casts)
    in_specs.extend(cast_specs)
    out_specs, out_shape, scratch = [], [], []
    if out_maps is not None:
        raster = jax.ShapeDtypeStruct((grid_rows * GRID_W, HALF_W), F32)
        scattered = jax.ShapeDtypeStruct((grid_rows, GRID_W, HALF_W), F32)
        in_hbm = pl.BlockSpec(memory_space=pl.ANY)
        out_shape = [raster, scattered, raster, scattered]
        out_specs = [pl.BlockSpec(blk, out_maps[0]), in_hbm, pl.BlockSpec(blk, out_maps[1]), in_hbm]
        scratch = [pltpu.VMEM((2,) + blk, F32)] * 2 + [pltpu.SemaphoreType.DMA((2,))] * 2
    out_shape.append(jax.ShapeDtypeStruct(s0.shape, F32))
    out_specs.append(state_spec)
    out_shape.extend(jax.ShapeDtypeStruct(w.shape, BF16) for w in casts)
    out_specs.extend(cast_specs)
    return pl.pallas_call(
        functools.partial(_scan_kernel, emit_outputs=out_maps is not None, n_casts=len(casts)),
        grid=(n_steps,),
        in_specs=in_specs,
        out_specs=out_specs,
        out_shape=out_shape,
        scratch_shapes=scratch,
        compiler_params=_params(1),
        name=name,
    )(*args)


def _scan_ctx(i, kf, kb, lff, lfb):
    n = i.shape[0] // SCAN_CHUNK
    fwd, bwd = (i, i, kf, lff), (i, i, kb, lfb)
    groups = [(fwd, lambda s: (s, 0)), (fwd, lambda s: (s, 1)),
              (bwd, lambda s: (n - 1 - s, 0)), (bwd, lambda s: (n - 1 - s, 1))]
    s0 = jnp.zeros((2 * HEADS, HEAD_DIM, HEAD_DIM), F32)
    (state,) = _scan(groups, s0, n, None, 0, [], "scan_ctx")
    return state


def _scan_latent(raster, colmajor, s0, casts):
    t = raster[0].shape[0]
    n = t // SCAN_CHUNK
    up, down = (lambda s: (s, 0)), (lambda s: (n - 1 - s, 0))
    groups = []
    for backward in (False, True):
        for q, i, kf, kb, lff, lfb in (raster, colmajor):
            arrays = (q, i, kb, lfb) if backward else (q, i, kf, lff)
            groups.append((arrays, down if backward else up))
    o_fr, o_fc, o_br, o_bc, _, *casted = _scan(groups, s0, n, [up, down], t // GRID_W, casts, "scan_latent")
    return (o_fr, o_fc.reshape(t, HALF_W), o_br, o_bc.reshape(t, HALF_W)), casted


def _outproj_kernel(a_ref, ofr_ref, ofc_ref, obr_ref, obc_ref, sg_ref, x_ref, w_ref,
                    g1_ref, sh2_ref, sc2_ref, og_ref, n2g_ref, x1_ref, h2_ref):
    o_r = ofr_ref[...] + obr_ref[...]
    o_c = ofc_ref[...] + obc_ref[...]
    ys = []
    for h in range(HEADS):
        o = o_r if h < HALF_HEADS else o_c
        oh = o[:, (h % HALF_HEADS) * HEAD_DIM:(h % HALF_HEADS + 1) * HEAD_DIM]
        cs = slice(h * HEAD_DIM, (h + 1) * HEAD_DIM)
        ys.append((oh * _rms_scale(oh) * og_ref[:, cs] * sg_ref[:, cs].astype(F32)).astype(BF16))
    y = jnp.concatenate(ys, axis=1)
    proj = jnp.dot(a_ref[...], w_ref[:GROUP_W, :], preferred_element_type=F32)
    proj = proj + jnp.dot(y, w_ref[GROUP_W:, :], preferred_element_type=F32)
    x1 = x_ref[...] + g1_ref[0:1, :] * proj
    x1_ref[...] = x1
    h2 = x1 * _rms_scale(x1) * n2g_ref[...]
    h2_ref[...] = (h2 * (1.0 + sc2_ref[0:1, :]) + sh2_ref[0:1, :]).astype(h2_ref.dtype)


def _outproj(a, o_fr, o_fc, o_br, o_bc, sg, x, w_bf, mod, o_gain, n2_gain):
    t, d = x.shape
    tm = 512
    row = lambda i: (i, 0)
    const2 = lambda i: (0, 0)
    half = pl.BlockSpec((tm, HALF_W), row)
    full = pl.BlockSpec((tm, GROUP_W), row)
    mod_col = lambda c: pl.BlockSpec((2, d), lambda i: (0, c))
    return pl.pallas_call(
        _outproj_kernel,
        grid=(t // tm,),
        in_specs=[full, half, half, half, half, full,
                  pl.BlockSpec((tm, d), row),
                  pl.BlockSpec(w_bf.shape, const2, pipeline_mode=pl.Buffered(1)),
                  mod_col(2), mod_col(3), mod_col(4),
                  pl.BlockSpec((1, GROUP_W), const2),
                  pl.BlockSpec((1, d), const2)],
        out_specs=[pl.BlockSpec((tm, d), row), pl.BlockSpec((tm, d), row)],
        out_shape=[jax.ShapeDtypeStruct((t, d), F32), jax.ShapeDtypeStruct((t, d), BF16)],
        compiler_params=_params(1),
        name="outproj",
    )(a, o_fr, o_fc, o_br, o_bc, sg, x, w_bf, mod, mod, mod, o_gain, n2_gain)


def _ffn_kernel(h_ref, x1_ref, w1_ref, w2_ref, g2_ref, fg_ref, out_ref):
    f = pl.program_id(1)

    @pl.when(f == 0)
    def _():
        out_ref[...] = jnp.zeros_like(out_ref)

    hid = jnp.dot(h_ref[...], w1_ref[...], preferred_element_type=F32)
    hid = jnp.square(jnp.maximum(hid, 0.0)).astype(BF16)
    out_ref[...] += jnp.dot(hid, w2_ref[...], preferred_element_type=F32)

    @pl.when(f == pl.num_programs(1) - 1)
    def _():
        x2 = x1_ref[...] + g2_ref[0:1, :] * out_ref[...]
        out_ref[...] = x2 * _rms_scale(x2) * fg_ref[...]


def _ffn(h2, x1, w1_bf, w2_bf, mod, final_gain):
    t, d = x1.shape
    d_ff = w1_bf.shape[1]
    tm, tf = 1024, 1024
    row = lambda i, f: (i, 0)
    return pl.pallas_call(
        _ffn_kernel,
        grid=(t // tm, d_ff // tf),
        in_specs=[pl.BlockSpec((tm, d), row),
                  pl.BlockSpec((tm, d), row, pipeline_mode=pl.Buffered(1)),
                  pl.BlockSpec((d, tf), lambda i, f: (0, f)),
                  pl.BlockSpec((tf, d), lambda i, f: (f, 0)),
                  pl.BlockSpec((2, d), lambda i, f: (0, 5)),
                  pl.BlockSpec((1, d), lambda i, f: (0, 0))],
        out_specs=pl.BlockSpec((tm, d), row, pipeline_mode=pl.Buffered(1)),
        out_shape=jax.ShapeDtypeStruct((t, d), F32),
        compiler_params=_params(2),
        name="ffn",
    )(h2, x1, w1_bf, w2_bf, mod, final_gain)


def kernel(x, c, ctx, c_ctx, w_ada, b_ada, norm1_gain, w_in, gmlp_w_s, gmlp_b_s, gmlp_v_gain,
           hgrn_lb_logits, hgrn_o_gain, w_out, norm2_gain, w_ff1, w_ff2, final_gain):
    bsz, t, d = x.shape
    assert bsz == 1 and w_ada.shape[0] == 1, "single sample, single layer"
    assert w_in.shape[2] == 7 * GROUP_W and t % (GRID_W * SCAN_CHUNK) == 0

    cond_t = jnp.stack([c[0], c_ctx], axis=1)
    mod = _adaln(cond_t, w_ada[0], b_ada)
    w_in_bf = _cast_bf16(w_in[0], "cast_w_in")

    b_s_b = jnp.broadcast_to(gmlp_b_s[0][:, :, None], (HEADS, GMLP_CHUNK, HEAD_DIM))
    a, *raster, sg = _inproj_raster(
        x[0], mod, norm1_gain, w_in_bf, gmlp_w_s[0], b_s_b, gmlp_v_gain[0], hgrn_lb_logits)
    colmajor = _inproj_colmajor(x[0], mod, norm1_gain, w_in_bf, hgrn_lb_logits)
    ci, ckf, ckb, clff, clfb = _inproj_ctx(ctx[0], mod, norm1_gain, w_in_bf, hgrn_lb_logits)

    s0 = _scan_ctx(ci, ckf, ckb, clff, clfb)
    (o_fr, o_fc, o_br, o_bc), (w_out_bf, w1_bf, w2_bf) = _scan_latent(
        raster, colmajor, s0, [w_out[0], w_ff1[0], w_ff2[0]])

    x1, h2 = _outproj(a, o_fr, o_fc, o_br, o_bc, sg, x[0], w_out_bf, mod,
                      hgrn_o_gain.reshape(1, GROUP_W), norm2_gain)
    out = _ffn(h2, x1, w1_bf, w2_bf, mod, final_gain.reshape(1, d))
    return out[None]
```

```python
import functools

import jax
import jax.numpy as jnp
from jax import lax
from jax.experimental import pallas as pl
from jax.experimental.pallas import tpu as pltpu

EPS = 1e-6
GRID_W = 64
HEADS = 8
HALF_HEADS = HEADS // 2
HEAD_DIM = 128
GROUP_W = HEADS * HEAD_DIM
HALF_W = HALF_HEADS * HEAD_DIM
GMLP_CHUNK = 128
SCAN_CHUNK = 128
LANES = 128
SUBLANES = 8
VMEM_LIMIT = 56 * 1024 * 1024

F32 = jnp.float32
BF16 = jnp.bfloat16


def _params(n_axes, vmem=VMEM_LIMIT):
    return pltpu.CompilerParams(dimension_semantics=("arbitrary",) * n_axes, vmem_limit_bytes=vmem)


def _sigmoid(x):
    return 1.0 / (1.0 + jnp.exp(-x))


def _gelu_tanh(x):
    return x * (0.5 * (1.0 + jnp.tanh(0.7978845608028654 * (x + 0.044715 * (x * x * x)))))


def _rms_scale(x):
    return lax.rsqrt(jnp.mean(x * x, axis=-1, keepdims=True) + EPS)


def _adaln_kernel(cond_t_ref, w_ref, b_ref, out_ref, s_ref):
    @pl.when(pl.program_id(0) == 0)
    def _():
        ct = cond_t_ref[...]
        s = ct * _sigmoid(ct)
        for r in range(2):
            s_ref[r] = jnp.broadcast_to(s[:, r:r + 1], s_ref.shape[1:])

    tn = w_ref.shape[1]
    for cb in range(tn // LANES):
        w = w_ref[:, cb * LANES:(cb + 1) * LANES]
        rows = [jnp.sum(w * s_ref[r], axis=0, keepdims=True) for r in range(2)]
        out_ref[:, cb * LANES:(cb + 1) * LANES] = (
            jnp.concatenate(rows, axis=0) + b_ref[:, cb * LANES:(cb + 1) * LANES])


def _adaln(cond_t, w_ada, b_ada):
    d, n = w_ada.shape
    tn = 1024
    return pl.pallas_call(
        _adaln_kernel,
        grid=(n // tn,),
        in_specs=[pl.BlockSpec((d, 2), lambda j: (0, 0)),
                  pl.BlockSpec((d, tn), lambda j: (0, j)),
                  pl.BlockSpec((1, tn), lambda j: (0, j))],
        out_specs=pl.BlockSpec((2, tn), lambda j: (0, j)),
        out_shape=jax.ShapeDtypeStruct((2, n), F32),
        scratch_shapes=[pltpu.VMEM((2, d, LANES), F32)],
        compiler_params=_params(1),
        name="adaln",
    )(cond_t, w_ada, b_ada)


def _cast_kernel(w_ref, o_ref):
    o_ref[...] = w_ref[...].astype(o_ref.dtype)


def _cast_bf16(w, name):
    k, n = w.shape
    tk = 256
    return pl.pallas_call(
        _cast_kernel,
        grid=(k // tk,),
        in_specs=[pl.BlockSpec((tk, n), lambda i: (i, 0))],
        out_specs=pl.BlockSpec((tk, n), lambda i: (i, 0)),
        out_shape=jax.ShapeDtypeStruct((k, n), BF16),
        compiler_params=_params(1),
        name=name,
    )(w)


def _lower_bounds(lbl_ref):
    l0, l1 = lbl_ref[0], lbl_ref[1]
    m = jnp.maximum(l0, l1)
    e0, e1 = jnp.exp(l0 - m), jnp.exp(l1 - m)
    return e1 / (e0 + e1)


def _visible(backward):
    ti = lax.broadcasted_iota(jnp.int32, (SCAN_CHUNK, SCAN_CHUNK), 0)
    si = lax.broadcasted_iota(jnp.int32, (SCAN_CHUNK, SCAN_CHUNK), 1)
    return (si >= ti) if backward else (si <= ti)


def _scan_operands(q, z, lb, backward, q_in_ref, k_in_ref, q_st_ref, k_st_ref, decay_ref):
    f = lb + (1.0 - lb) * _sigmoid(z)
    lf = jnp.log(f)
    k = 1.0 - f
    tri = _visible(backward).astype(BF16)
    hi = lf.astype(BF16)
    lo = (lf - hi.astype(F32)).astype(BF16)
    c = SCAN_CHUNK
    mid_row = c // 2 - 1 if backward else c // 2
    last_row = 0 if backward else c - 1
    for j in range(z.shape[0] // c):
        rs = slice(j * c, (j + 1) * c)
        b = (jnp.dot(tri, hi[rs], preferred_element_type=F32)
             + jnp.dot(tri, lo[rs], preferred_element_type=F32))
        b_mid = b[mid_row:mid_row + 1, :]
        b_last = b[last_row:last_row + 1, :]
        if q is not None:
            q_in_ref[rs, :] = (q[rs] * jnp.exp(b - b_mid)).astype(BF16)
            k_in_ref[rs, :] = (k[rs] * jnp.exp(b_mid - b)).astype(BF16)
            q_st_ref[rs, :] = (q[rs] * jnp.exp(b)).astype(BF16)
        k_st_ref[rs, :] = (k[rs] * jnp.exp(b_last - b)).astype(BF16)
        decay_ref[j] = jnp.broadcast_to(jnp.exp(b_last), decay_ref.shape[1:])


def _normed_input(x, gain_ref, sh_ref, sc_ref, row):
    h = x * _rms_scale(x) * gain_ref[...]
    h = h * (1.0 + sc_ref[row:row + 1, :]) + sh_ref[row:row + 1, :]
    return h.astype(BF16)


N_CASTS = 3


def _inproj_raster_kernel(x_ref, sh_ref, sc_ref, gain_ref, wuv_ref, wg_ref, wq_ref, wi_ref, wf_ref, wb_ref,
                          ws_ref, bs_ref, vg_ref, lbl_ref, *rest):
    cast_ins, rest = rest[:N_CASTS], rest[N_CASTS:]
    a_ref, sg_ref, v_ref = rest[:3]
    fwd_refs, bwd_refs, cast_outs = rest[3:8], rest[8:13], rest[13:]
    hb = _normed_input(x_ref[...], gain_ref, sh_ref, sc_ref, 0)
    proj = lambda w_ref: jnp.dot(hb, w_ref[...], preferred_element_type=F32)

    u = _gelu_tanh(jnp.dot(hb, wuv_ref[:, :GROUP_W], preferred_element_type=F32))
    v = _gelu_tanh(jnp.dot(hb, wuv_ref[:, GROUP_W:], preferred_element_type=F32))
    tm = u.shape[0]
    for h in range(HEADS):
        cs = slice(h * HEAD_DIM, (h + 1) * HEAD_DIM)
        vh = v[:, cs]
        vn = (vh * _rms_scale(vh) * vg_ref[h:h + 1, :]).astype(BF16)
        w_s = ws_ref[h].astype(BF16)
        for c in range(tm // GMLP_CHUNK):
            rs = slice(c * GMLP_CHUNK, (c + 1) * GMLP_CHUNK)
            mixed = jnp.dot(w_s, vn[rs], preferred_element_type=F32) + bs_ref[h]
            a_ref[rs, cs] = (u[rs, cs] * mixed).astype(a_ref.dtype)

    q = proj(wq_ref)
    v_ref[...] = proj(wi_ref).astype(v_ref.dtype)
    lb = _lower_bounds(lbl_ref)
    _scan_operands(q, proj(wf_ref), lb[0:1, :HALF_W], False, *fwd_refs)
    _scan_operands(q, proj(wb_ref), lb[1:2, :HALF_W], True, *bwd_refs)
    g = proj(wg_ref)
    sg_ref[...] = (g * _sigmoid(g)).astype(sg_ref.dtype)

    for w_ref, wb16_ref in zip(cast_ins, cast_outs):
        wb16_ref[...] = w_ref[...].astype(wb16_ref.dtype)


def _scan_operand_outputs(t, tm, width, index_map, decay_map):
    data = jax.ShapeDtypeStruct((t, width), BF16)
    decay = jax.ShapeDtypeStruct((t // SCAN_CHUNK, SUBLANES, width), F32)
    data_spec = pl.BlockSpec((tm, width), index_map)
    decay_spec = pl.BlockSpec((tm // SCAN_CHUNK, SUBLANES, width), decay_map)
    return [data] * 4 + [decay], [data_spec] * 4 + [decay_spec]


def _inproj_raster(x, mod, gain, w_bf, w_s, b_s_b, v_gain, lb_logits, casts):
    t, d = x.shape
    tm = 256
    n_steps = t // tm
    row = lambda i: (i, 0)
    row3 = lambda i: (i, 0, 0)
    const2 = lambda i: (0, 0)
    const3 = lambda i: (0, 0, 0)
    full = pl.BlockSpec((tm, GROUP_W), row)
    half = pl.BlockSpec((tm, HALF_W), row)
    resident = lambda width, blk: pl.BlockSpec((d, width), lambda i: (0, blk), pipeline_mode=pl.Buffered(1))
    cast_specs = [pl.BlockSpec((w.shape[0] // n_steps, w.shape[1]), row) for w in casts]
    op_shapes, op_specs = _scan_operand_outputs(t, tm, HALF_W, row, row3)
    return pl.pallas_call(
        _inproj_raster_kernel,
        grid=(n_steps,),
        in_specs=[pl.BlockSpec((tm, d), row),
                  pl.BlockSpec((2, d), lambda i: (0, 0)),
                  pl.BlockSpec((2, d), lambda i: (0, 1)),
                  pl.BlockSpec((1, d), const2),
                  resident(2 * GROUP_W, 0), resident(GROUP_W, 6),
                  resident(HALF_W, 4), resident(HALF_W, 6), resident(HALF_W, 8), resident(HALF_W, 10),
                  pl.BlockSpec(w_s.shape, const3),
                  pl.BlockSpec(b_s_b.shape, const3),
                  pl.BlockSpec(v_gain.shape, const2),
                  pl.BlockSpec(lb_logits.shape, const3)] + cast_specs,
        out_specs=[full, full, half] + op_specs + op_specs + cast_specs,
        out_shape=([jax.ShapeDtypeStruct((t, GROUP_W), BF16)] * 2 + [jax.ShapeDtypeStruct((t, HALF_W), BF16)]
                   + op_shapes + op_shapes + [jax.ShapeDtypeStruct(w.shape, BF16) for w in casts]),
        compiler_params=_params(1),
        name="inproj_raster",
    )(x, mod, mod, gain, *([w_bf] * 6), w_s, b_s_b, v_gain, lb_logits, *casts)


COLS_PER_STEP = 8


def _gather_copies(x_hbm_ref, buf_ref, sem_ref, slot, step):
    rows = x_hbm_ref.shape[0]
    return [pltpu.make_async_copy(x_hbm_ref.at[:, step * COLS_PER_STEP + wi, :],
                                  buf_ref.at[slot, pl.ds(wi * rows, rows), :],
                                  sem_ref.at[slot, wi])
            for wi in range(COLS_PER_STEP)]


def _inproj_colmajor_kernel(x_hbm_ref, sh_ref, sc_ref, gain_ref, wq_ref, wi_ref, wf_ref, wb_ref, lbl_ref,
                            v_ref, *rest):
    fwd_refs, bwd_refs, (xbuf_ref, sem_ref) = rest[:5], rest[5:10], rest[10:]
    step, n_steps = pl.program_id(0), pl.num_programs(0)
    slot = step % 2
    copies = functools.partial(_gather_copies, x_hbm_ref, xbuf_ref, sem_ref)

    @pl.when(step == 0)
    def _():
        for cp in copies(slot, step):
            cp.start()

    @pl.when(step + 1 < n_steps)
    def _():
        for cp in copies(1 - slot, step + 1):
            cp.start()

    for cp in copies(slot, step):
        cp.wait()
    hb = _normed_input(xbuf_ref[slot], gain_ref, sh_ref, sc_ref, 0)
    proj = lambda w_ref: jnp.dot(hb, w_ref[...], preferred_element_type=F32)
    q = proj(wq_ref)
    v_ref[...] = proj(wi_ref).astype(v_ref.dtype)
    lb = _lower_bounds(lbl_ref)
    _scan_operands(q, proj(wf_ref), lb[0:1, HALF_W:], False, *fwd_refs)
    _scan_operands(q, proj(wb_ref), lb[1:2, HALF_W:], True, *bwd_refs)


def _inproj_colmajor(x, mod, gain, w_bf, lb_logits):
    t, d = x.shape
    rows = t // GRID_W
    tm = COLS_PER_STEP * rows
    row = lambda i: (i, 0)
    const2 = lambda i: (0, 0)
    w_spec = lambda g: pl.BlockSpec((d, HALF_W), lambda i: (0, 2 * g + 1), pipeline_mode=pl.Buffered(1))
    op_shapes, op_specs = _scan_operand_outputs(t, tm, HALF_W, row, lambda i: (i, 0, 0))
    return pl.pallas_call(
        _inproj_colmajor_kernel,
        grid=(GRID_W // COLS_PER_STEP,),
        in_specs=[pl.BlockSpec(memory_space=pl.ANY),
                  pl.BlockSpec((2, d), lambda i: (0, 0)),
                  pl.BlockSpec((2, d), lambda i: (0, 1)),
                  pl.BlockSpec((1, d), const2),
                  w_spec(2), w_spec(3), w_spec(4), w_spec(5),
                  pl.BlockSpec(lb_logits.shape, lambda i: (0, 0, 0))],
        out_specs=[pl.BlockSpec((tm, HALF_W), row)] + op_specs + op_specs,
        out_shape=[jax.ShapeDtypeStruct((t, HALF_W), BF16)] + op_shapes + op_shapes,
        scratch_shapes=[pltpu.VMEM((2, tm, d), F32), pltpu.SemaphoreType.DMA((2, COLS_PER_STEP))],
        compiler_params=_params(1),
        name="inproj_colmajor",
    )(x.reshape(rows, GRID_W, d), mod, mod, gain, w_bf, w_bf, w_bf, w_bf, lb_logits)


def _inproj_ctx_kernel(x_ref, sh_ref, sc_ref, gain_ref, w_ref, lbl_ref,
                       v_ref, kf_ref, df_ref, kb_ref, db_ref):
    hb = _normed_input(x_ref[...], gain_ref, sh_ref, sc_ref, 1)

    def proj(g):
        return jnp.dot(hb, w_ref[:, g * GROUP_W:(g + 1) * GROUP_W], preferred_element_type=F32)

    v_ref[...] = proj(0).astype(v_ref.dtype)
    lb = _lower_bounds(lbl_ref)
    _scan_operands(None, proj(1), lb[0:1, :], False, None, None, None, kf_ref, df_ref)
    _scan_operands(None, proj(2), lb[1:2, :], True, None, None, None, kb_ref, db_ref)


def _inproj_ctx(ctx, mod, gain, w_bf, lb_logits):
    t, d = ctx.shape
    const2 = lambda i: (0, 0)
    const3 = lambda i: (0, 0, 0)
    data = jax.ShapeDtypeStruct((t, GROUP_W), BF16)
    decay = jax.ShapeDtypeStruct((t // SCAN_CHUNK, SUBLANES, GROUP_W), F32)
    data_spec = pl.BlockSpec(data.shape, const2)
    decay_spec = pl.BlockSpec(decay.shape, const3)
    return pl.pallas_call(
        _inproj_ctx_kernel,
        grid=(1,),
        in_specs=[pl.BlockSpec((t, d), const2),
                  pl.BlockSpec((2, d), lambda i: (0, 0)),
                  pl.BlockSpec((2, d), lambda i: (0, 1)),
                  pl.BlockSpec((1, d), const2),
                  pl.BlockSpec((d, 3 * GROUP_W), lambda i: (0, 1)),
                  pl.BlockSpec(lb_logits.shape, const3)],
        out_specs=[data_spec, data_spec, decay_spec, data_spec, decay_spec],
        out_shape=[data, data, decay, data, decay],
        compiler_params=_params(1),
        name="inproj_ctx",
    )(ctx, mod, mod, gain, w_bf, lb_logits)


def _scan_chunk(groups, state_ref, o_refs):
    units = [(g, h, slice(h * HEAD_DIM, (h + 1) * HEAD_DIM)) for g in range(4) for h in range(HALF_HEADS)]
    with_outputs = len(groups[0]) == 6
    scores = {}
    if with_outputs:
        for g, h, cs in units:
            q_in_ref, k_in_ref = groups[g][:2]
            s = lax.dot_general(q_in_ref[:, cs], k_in_ref[:, cs], (((1,), (1,)), ((), ())),
                                preferred_element_type=F32)
            scores[g, h] = jnp.where(_visible(g >= 2), s, 0.0).astype(BF16)
    outs = {}
    for g, h, cs in units:
        k_st_ref, v_ref, decay_ref = groups[g][-3:]
        u = g * HALF_HEADS + h
        st = state_ref[u]
        vh = v_ref[:, cs]
        if with_outputs:
            q_st_ref = groups[g][2]
            outs[g, h] = jnp.dot(jnp.concatenate([scores[g, h], q_st_ref[:, cs]], axis=1),
                                 jnp.concatenate([vh, st.astype(BF16)], axis=0), preferred_element_type=F32)
        decay = jnp.broadcast_to(decay_ref[0:1, cs], (HEAD_DIM, HEAD_DIM)).T
        state_ref[u] = st * decay + lax.dot_general(
            k_st_ref[:, cs], vh, (((0,), (0,)), ((), ())), preferred_element_type=F32)
    if with_outputs:
        for g, o_ref in enumerate(o_refs):
            o_ref[...] = jnp.concatenate([outs[g, h] for h in range(HALF_HEADS)], axis=1)


def _scatter_copy(buf_ref, out_hbm_ref, sem_ref, slot, chunk):
    per_col = out_hbm_ref.shape[0] // SCAN_CHUNK
    r0 = (chunk % per_col) * SCAN_CHUNK
    dst = out_hbm_ref.at[pl.ds(r0, SCAN_CHUNK), chunk // per_col, :]
    return pltpu.make_async_copy(buf_ref.at[slot], dst, sem_ref.at[slot])


def _scan_kernel(*refs, n_in, emit_outputs):
    groups = [refs[g * n_in:(g + 1) * n_in] for g in range(4)]
    s0_ref, rest = refs[4 * n_in], refs[4 * n_in + 1:]
    step, n_steps = pl.program_id(0), pl.num_programs(0)
    state_ref = rest[4] if emit_outputs else rest[0]

    @pl.when(step == 0)
    def _():
        state_ref[...] = s0_ref[...]

    if not emit_outputs:
        _scan_chunk(groups, state_ref, None)
        return

    o_refs, (bufs, sems) = rest[:4], (rest[5:7], rest[7:9])
    slot = step % 2
    _scan_chunk(groups, state_ref, [o_refs[0], bufs[0].at[slot], o_refs[2], bufs[1].at[slot]])
    for d, backward in enumerate((False, True)):
        chunk_of = (lambda s: n_steps - 1 - s) if backward else (lambda s: s)
        copy = functools.partial(_scatter_copy, bufs[d], o_refs[2 * d + 1], sems[d])
        copy(slot, chunk_of(step)).start()

        @pl.when(step > 0)
        def _():
            copy(1 - slot, chunk_of(step - 1)).wait()

        @pl.when(step == n_steps - 1)
        def _():
            copy(slot, chunk_of(step)).wait()


def _scan(groups, s0, n_steps, out_maps, grid_rows, name):
    blk = (SCAN_CHUNK, HALF_W)
    args, in_specs = [], []
    for arrays, chunk_of, col in groups:
        args.extend(arrays)
        data_map = lambda s, chunk_of=chunk_of, col=col: (chunk_of(s), col)
        decay_map = lambda s, chunk_of=chunk_of, col=col: (chunk_of(s), 0, col)
        in_specs.extend([pl.BlockSpec(blk, data_map)] * (len(arrays) - 1))
        in_specs.append(pl.BlockSpec((None, SUBLANES, HALF_W), decay_map))
    args.append(s0)
    state_spec = pl.BlockSpec(s0.shape, lambda s: (0, 0, 0))
    in_specs.append(state_spec)
    out_specs, out_shape, scratch = [], [], []
    if out_maps is not None:
        raster = jax.ShapeDtypeStruct((grid_rows * GRID_W, HALF_W), F32)
        scattered = jax.ShapeDtypeStruct((grid_rows, GRID_W, HALF_W), F32)
        in_hbm = pl.BlockSpec(memory_space=pl.ANY)
        out_shape = [raster, scattered, raster, scattered]
        out_specs = [pl.BlockSpec(blk, out_maps[0]), in_hbm, pl.BlockSpec(blk, out_maps[1]), in_hbm]
        scratch = [pltpu.VMEM((2,) + blk, F32)] * 2 + [pltpu.SemaphoreType.DMA((2,))] * 2
    out_shape.append(jax.ShapeDtypeStruct(s0.shape, F32))
    out_specs.append(state_spec)
    return pl.pallas_call(
        functools.partial(_scan_kernel, n_in=len(groups[0][0]), emit_outputs=out_maps is not None),
        grid=(n_steps,),
        in_specs=in_specs,
        out_specs=out_specs,
        out_shape=out_shape,
        scratch_shapes=scratch,
        compiler_params=_params(1),
        name=name,
    )(*args)


def _scan_ctx(v, kf, df, kb, db):
    n = v.shape[0] // SCAN_CHUNK
    up, down = (lambda s: s), (lambda s: n - 1 - s)
    groups = [((kf, v, df), up, 0), ((kf, v, df), up, 1), ((kb, v, db), down, 0), ((kb, v, db), down, 1)]
    s0 = jnp.zeros((2 * HEADS, HEAD_DIM, HEAD_DIM), F32)
    (state,) = _scan(groups, s0, n, None, 0, "scan_ctx")
    return state


def _scan_latent(raster, colmajor, s0):
    t = raster[0].shape[0]
    n = t // SCAN_CHUNK
    up, down = (lambda s: s), (lambda s: n - 1 - s)
    groups = []
    for d, chunk_of in enumerate((up, down)):
        for v, *operands in (raster, colmajor):
            q_in, k_in, q_st, k_st, decay = operands[5 * d:5 * d + 5]
            groups.append(((q_in, k_in, q_st, k_st, v, decay), chunk_of, 0))
    out_maps = [lambda s: (s, 0), lambda s: (n - 1 - s, 0)]
    o_fr, o_fc, o_br, o_bc, _ = _scan(groups, s0, n, out_maps, t // GRID_W, "scan_latent")
    return o_fr, o_fc.reshape(t, HALF_W), o_br, o_bc.reshape(t, HALF_W)


def _outproj_kernel(a_ref, ofr_ref, ofc_ref, obr_ref, obc_ref, sg_ref, x_ref, w_ref,
                    g1_ref, sh2_ref, sc2_ref, og_ref, n2g_ref, x1_ref, h2_ref):
    o_r = ofr_ref[...] + obr_ref[...]
    o_c = ofc_ref[...] + obc_ref[...]
    ys = []
    for h in range(HEADS):
        o = o_r if h < HALF_HEADS else o_c
        oh = o[:, (h % HALF_HEADS) * HEAD_DIM:(h % HALF_HEADS + 1) * HEAD_DIM]
        cs = slice(h * HEAD_DIM, (h + 1) * HEAD_DIM)
        ys.append((oh * _rms_scale(oh) * og_ref[:, cs] * sg_ref[:, cs].astype(F32)).astype(BF16))
    y = jnp.concatenate(ys, axis=1)
    proj = jnp.dot(a_ref[...], w_ref[:GROUP_W, :], preferred_element_type=F32)
    proj = proj + jnp.dot(y, w_ref[GROUP_W:, :], preferred_element_type=F32)
    x1 = x_ref[...] + g1_ref[0:1, :] * proj
    x1_ref[...] = x1
    h2 = x1 * _rms_scale(x1) * n2g_ref[...]
    h2_ref[...] = (h2 * (1.0 + sc2_ref[0:1, :]) + sh2_ref[0:1, :]).astype(h2_ref.dtype)


def _outproj(a, o_fr, o_fc, o_br, o_bc, sg, x, w_bf, mod, o_gain, n2_gain):
    t, d = x.shape
    tm = 512
    row = lambda i: (i, 0)
    const2 = lambda i: (0, 0)
    half = pl.BlockSpec((tm, HALF_W), row)
    full = pl.BlockSpec((tm, GROUP_W), row)
    mod_col = lambda c: pl.BlockSpec((2, d), lambda i: (0, c))
    return pl.pallas_call(
        _outproj_kernel,
        grid=(t // tm,),
        in_specs=[full, half, half, half, half, full,
                  pl.BlockSpec((tm, d), row),
                  pl.BlockSpec(w_bf.shape, const2, pipeline_mode=pl.Buffered(1)),
                  mod_col(2), mod_col(3), mod_col(4),
                  pl.BlockSpec((1, GROUP_W), const2),
                  pl.BlockSpec((1, d), const2)],
        out_specs=[pl.BlockSpec((tm, d), row), pl.BlockSpec((tm, d), row)],
        out_shape=[jax.ShapeDtypeStruct((t, d), F32), jax.ShapeDtypeStruct((t, d), BF16)],
        compiler_params=_params(1),
        name="outproj",
    )(a, o_fr, o_fc, o_br, o_bc, sg, x, w_bf, mod, mod, mod, o_gain, n2_gain)


def _ffn_kernel(h_ref, x1_ref, w1_ref, w2_ref, g2_ref, fg_ref, out_ref):
    f = pl.program_id(1)

    @pl.when(f == 0)
    def _():
        out_ref[...] = jnp.zeros_like(out_ref)

    hid = jnp.dot(h_ref[...], w1_ref[...], preferred_element_type=F32)
    hid = jnp.square(jnp.maximum(hid, 0.0)).astype(BF16)
    out_ref[...] += jnp.dot(hid, w2_ref[...], preferred_element_type=F32)

    @pl.when(f == pl.num_programs(1) - 1)
    def _():
        x2 = x1_ref[...] + g2_ref[0:1, :] * out_ref[...]
        out_ref[...] = x2 * _rms_scale(x2) * fg_ref[...]


def _ffn(h2, x1, w1_bf, w2_bf, mod, final_gain):
    t, d = x1.shape
    d_ff = w1_bf.shape[1]
    tm, tf = 1024, 1024
    row = lambda i, f: (i, 0)
    return pl.pallas_call(
        _ffn_kernel,
        grid=(t // tm, d_ff // tf),
        in_specs=[pl.BlockSpec((tm, d), row),
                  pl.BlockSpec((tm, d), row, pipeline_mode=pl.Buffered(1)),
                  pl.BlockSpec((d, tf), lambda i, f: (0, f)),
                  pl.BlockSpec((tf, d), lambda i, f: (f, 0)),
                  pl.BlockSpec((2, d), lambda i, f: (0, 5)),
                  pl.BlockSpec((1, d), lambda i, f: (0, 0))],
        out_specs=pl.BlockSpec((tm, d), row, pipeline_mode=pl.Buffered(1)),
        out_shape=jax.ShapeDtypeStruct((t, d), F32),
        compiler_params=_params(2),
        name="ffn",
    )(h2, x1, w1_bf, w2_bf, mod, final_gain)


def kernel(x, c, ctx, c_ctx, w_ada, b_ada, norm1_gain, w_in, gmlp_w_s, gmlp_b_s, gmlp_v_gain,
           hgrn_lb_logits, hgrn_o_gain, w_out, norm2_gain, w_ff1, w_ff2, final_gain):
    bsz, t, d = x.shape
    assert bsz == 1 and w_ada.shape[0] == 1, "single sample, single layer"
    assert w_in.shape[2] == 7 * GROUP_W and t % (GRID_W * SCAN_CHUNK) == 0

    cond_t = jnp.stack([c[0], c_ctx], axis=1)
    mod = _adaln(cond_t, w_ada[0], b_ada)
    w_in_bf = _cast_bf16(w_in[0], "cast_w_in")

    b_s_b = jnp.broadcast_to(gmlp_b_s[0][:, :, None], (HEADS, GMLP_CHUNK, HEAD_DIM))
    a, sg, *rest = _inproj_raster(x[0], mod, norm1_gain, w_in_bf, gmlp_w_s[0], b_s_b, gmlp_v_gain[0],
                                  hgrn_lb_logits, [w_out[0], w_ff1[0], w_ff2[0]])
    raster, (w_out_bf, w1_bf, w2_bf) = rest[:-N_CASTS], rest[-N_CASTS:]
    colmajor = _inproj_colmajor(x[0], mod, norm1_gain, w_in_bf, hgrn_lb_logits)
    ctx_ops = _inproj_ctx(ctx[0], mod, norm1_gain, w_in_bf, hgrn_lb_logits)

    s0 = _scan_ctx(*ctx_ops)
    o_fr, o_fc, o_br, o_bc = _scan_latent(raster, colmajor, s0)

    x1, h2 = _outproj(a, o_fr, o_fc, o_br, o_bc, sg, x[0], w_out_bf, mod,
                      hgrn_o_gain.reshape(1, GROUP_W), norm2_gain)
    out = _ffn(h2, x1, w1_bf, w2_bf, mod, final_gain.reshape(1, d))
    return out[None]
```

```python
import functools

import jax
import jax.numpy as jnp
from jax import lax
from jax.experimental import pallas as pl
from jax.experimental.pallas import tpu as pltpu

EPS = 1e-6
GRID_W = 64
HEADS = 8
HALF_HEADS = HEADS // 2
HEAD_DIM = 128
GROUP_W = HEADS * HEAD_DIM
HALF_W = HALF_HEADS * HEAD_DIM
GMLP_CHUNK = 128
SCAN_CHUNK = 128
LANES = 128
SUBLANES = 8
MXU_COLS = 256
VMEM_LIMIT = 56 * 1024 * 1024

F32 = jnp.float32
BF16 = jnp.bfloat16


def _params(n_axes, vmem=VMEM_LIMIT):
    return pltpu.CompilerParams(dimension_semantics=("arbitrary",) * n_axes, vmem_limit_bytes=vmem)


def _sigmoid(x):
    return 1.0 / (1.0 + jnp.exp(-x))


def _gelu_tanh(x):
    return x * (0.5 * (1.0 + jnp.tanh(0.7978845608028654 * (x + 0.044715 * (x * x * x)))))


def _rms_scale(x):
    return lax.rsqrt(jnp.mean(x * x, axis=-1, keepdims=True) + EPS)


def _adaln_kernel(cond_t_ref, w_ref, b_ref, out_ref, s_ref):
    @pl.when(pl.program_id(0) == 0)
    def _():
        ct = cond_t_ref[...]
        s = ct * _sigmoid(ct)
        for r in range(2):
            s_ref[r] = jnp.broadcast_to(s[:, r:r + 1], s_ref.shape[1:])

    tn = w_ref.shape[1]
    for cb in range(tn // LANES):
        w = w_ref[:, cb * LANES:(cb + 1) * LANES]
        rows = [jnp.sum(w * s_ref[r], axis=0, keepdims=True) for r in range(2)]
        out_ref[:, cb * LANES:(cb + 1) * LANES] = (
            jnp.concatenate(rows, axis=0) + b_ref[:, cb * LANES:(cb + 1) * LANES])


def _adaln(cond_t, w_ada, b_ada):
    d, n = w_ada.shape
    tn = 1024
    return pl.pallas_call(
        _adaln_kernel,
        grid=(n // tn,),
        in_specs=[pl.BlockSpec((d, 2), lambda j: (0, 0)),
                  pl.BlockSpec((d, tn), lambda j: (0, j)),
                  pl.BlockSpec((1, tn), lambda j: (0, j))],
        out_specs=pl.BlockSpec((2, tn), lambda j: (0, j)),
        out_shape=jax.ShapeDtypeStruct((2, n), F32),
        scratch_shapes=[pltpu.VMEM((2, d, LANES), F32)],
        compiler_params=_params(1),
        name="adaln",
    )(cond_t, w_ada, b_ada)


def _cast_kernel(w_ref, o_ref):
    o_ref[...] = w_ref[...].astype(o_ref.dtype)


def _cast_bf16(w, name):
    k, n = w.shape
    tk = 256
    return pl.pallas_call(
        _cast_kernel,
        grid=(k // tk,),
        in_specs=[pl.BlockSpec((tk, n), lambda i: (i, 0))],
        out_specs=pl.BlockSpec((tk, n), lambda i: (i, 0)),
        out_shape=jax.ShapeDtypeStruct((k, n), BF16),
        compiler_params=_params(1),
        name=name,
    )(w)


def _lower_bounds(lbl_ref):
    l0, l1 = lbl_ref[0], lbl_ref[1]
    m = jnp.maximum(l0, l1)
    e0, e1 = jnp.exp(l0 - m), jnp.exp(l1 - m)
    return e1 / (e0 + e1)


def _visible(backward):
    ti = lax.broadcasted_iota(jnp.int32, (SCAN_CHUNK, SCAN_CHUNK), 0)
    si = lax.broadcasted_iota(jnp.int32, (SCAN_CHUNK, SCAN_CHUNK), 1)
    return (si >= ti) if backward else (si <= ti)


def _gate_terms(z, lb):
    f = lb + (1.0 - lb) * _sigmoid(z)
    lf = jnp.log(f)
    hi = lf.astype(BF16)
    lo = (lf - hi.astype(F32)).astype(BF16)
    return 1.0 - f, hi, lo


def _chunk_log_decay(gate, backward):
    _, hi, lo = gate
    tri = _visible(backward).astype(BF16)
    c = SCAN_CHUNK
    return [jnp.dot(tri, hi[j * c:(j + 1) * c], preferred_element_type=F32)
            + jnp.dot(tri, lo[j * c:(j + 1) * c], preferred_element_type=F32)
            for j in range(hi.shape[0] // c)]


def _emit_scan_operands(q, gate, bs, backward, q_in_ref, k_in_ref, q_st_ref, k_st_ref, decay_ref):
    k = gate[0]
    c = SCAN_CHUNK
    mid_row = c // 2 - 1 if backward else c // 2
    last_row = 0 if backward else c - 1
    for j, b in enumerate(bs):
        rs = slice(j * c, (j + 1) * c)
        b_mid = b[mid_row:mid_row + 1, :]
        b_last = b[last_row:last_row + 1, :]
        if q is None:
            k_st_ref[rs, :] = (k[rs] * jnp.exp(b_last - b)).astype(BF16)
        else:
            k_in = k[rs] * jnp.exp(b_mid - b)
            k_st_ref[rs, :] = (k_in * jnp.exp(b_last - b_mid)).astype(BF16)
            q_in = q[rs] * jnp.exp(b - b_mid)
            q_in_ref[rs, :] = q_in.astype(BF16)
            k_in_ref[rs, :] = k_in.astype(BF16)
            q_st_ref[rs, :] = (q_in * jnp.exp(b_mid)).astype(BF16)
        decay_ref[j] = jnp.broadcast_to(jnp.exp(b_last), decay_ref.shape[1:])


def _scan_operands(q, z, lb, backward, *refs):
    gate = _gate_terms(z, lb)
    _emit_scan_operands(q, gate, _chunk_log_decay(gate, backward), backward, *refs)


def _normed_input(x, gain_ref, sh_ref, sc_ref, row):
    h = x * _rms_scale(x) * gain_ref[...]
    h = h * (1.0 + sc_ref[row:row + 1, :]) + sh_ref[row:row + 1, :]
    return h.astype(BF16)


N_CASTS = 3


def _inproj_raster_kernel(x_ref, sh_ref, sc_ref, gain_ref, wuv_ref, wg_ref, wq_ref, wi_ref, wf_ref, wb_ref,
                          ws_ref, bs_ref, vg_ref, lbl_ref, *rest):
    cast_ins, rest = rest[:N_CASTS], rest[N_CASTS:]
    a_ref, sg_ref, v_ref = rest[:3]
    fwd_refs, bwd_refs, cast_outs = rest[3:8], rest[8:13], rest[13:]
    hb = _normed_input(x_ref[...], gain_ref, sh_ref, sc_ref, 0)
    proj = lambda w_ref: jnp.dot(hb, w_ref[...], preferred_element_type=F32)

    q = proj(wq_ref)
    zf = proj(wf_ref)
    zb = proj(wb_ref)
    lb = _lower_bounds(lbl_ref)
    gate_f = _gate_terms(zf, lb[0:1, :HALF_W])
    u = jnp.dot(hb, wuv_ref[:, :GROUP_W], preferred_element_type=F32)
    b_f = _chunk_log_decay(gate_f, False)
    gate_b = _gate_terms(zb, lb[1:2, :HALF_W])
    v = jnp.dot(hb, wuv_ref[:, GROUP_W:], preferred_element_type=F32)
    _emit_scan_operands(q, gate_f, b_f, False, *fwd_refs)
    b_b = _chunk_log_decay(gate_b, True)
    g = proj(wg_ref)
    _emit_scan_operands(q, gate_b, b_b, True, *bwd_refs)
    for w_ref, wb16_ref in zip(cast_ins, cast_outs):
        wb16_ref[...] = w_ref[...].astype(wb16_ref.dtype)
    u = _gelu_tanh(u)
    v = _gelu_tanh(v)

    tm = u.shape[0]
    for h in range(HEADS):
        cs = slice(h * HEAD_DIM, (h + 1) * HEAD_DIM)
        vh = v[:, cs]
        vn = (vh * _rms_scale(vh) * vg_ref[h:h + 1, :]).astype(BF16)
        w_s = ws_ref[h].astype(BF16)
        for c in range(tm // GMLP_CHUNK):
            rs = slice(c * GMLP_CHUNK, (c + 1) * GMLP_CHUNK)
            mixed = jnp.dot(w_s, vn[rs], preferred_element_type=F32) + bs_ref[h]
            a_ref[rs, cs] = (u[rs, cs] * mixed).astype(a_ref.dtype)

    vi = proj(wi_ref)
    sg_ref[...] = (g * _sigmoid(g)).astype(sg_ref.dtype)
    v_ref[...] = vi.astype(v_ref.dtype)


def _scan_operand_outputs(t, tm, width, index_map, decay_map):
    data = jax.ShapeDtypeStruct((t, width), BF16)
    decay = jax.ShapeDtypeStruct((t // SCAN_CHUNK, SUBLANES, width), F32)
    data_spec = pl.BlockSpec((tm, width), index_map)
    decay_spec = pl.BlockSpec((tm // SCAN_CHUNK, SUBLANES, width), decay_map)
    return [data] * 4 + [decay], [data_spec] * 4 + [decay_spec]


def _inproj_raster(x, mod, gain, w_bf, w_s, b_s_b, v_gain, lb_logits, casts):
    t, d = x.shape
    tm = 256
    n_steps = t // tm
    row = lambda i: (i, 0)
    const2 = lambda i: (0, 0)
    const3 = lambda i: (0, 0, 0)
    full = pl.BlockSpec((tm, GROUP_W), row)
    half = pl.BlockSpec((tm, HALF_W), row)
    resident = lambda width, blk: pl.BlockSpec((d, width), lambda i: (0, blk), pipeline_mode=pl.Buffered(1))
    cast_specs = [pl.BlockSpec((w.shape[0] // n_steps, w.shape[1]), row) for w in casts]
    op_shapes, op_specs = _scan_operand_outputs(t, tm, HALF_W, row, lambda i: (i, 0, 0))
    return pl.pallas_call(
        _inproj_raster_kernel,
        grid=(n_steps,),
        in_specs=[pl.BlockSpec((tm, d), row),
                  pl.BlockSpec((2, d), lambda i: (0, 0)),
                  pl.BlockSpec((2, d), lambda i: (0, 1)),
                  pl.BlockSpec((1, d), const2),
                  resident(2 * GROUP_W, 0), resident(GROUP_W, 6),
                  resident(HALF_W, 4), resident(HALF_W, 6), resident(HALF_W, 8), resident(HALF_W, 10),
                  pl.BlockSpec(w_s.shape, const3),
                  pl.BlockSpec(b_s_b.shape, const3),
                  pl.BlockSpec(v_gain.shape, const2),
                  pl.BlockSpec(lb_logits.shape, const3)] + cast_specs,
        out_specs=[full, full, half] + op_specs + op_specs + cast_specs,
        out_shape=([jax.ShapeDtypeStruct((t, GROUP_W), BF16)] * 2 + [jax.ShapeDtypeStruct((t, HALF_W), BF16)]
                   + op_shapes + op_shapes + [jax.ShapeDtypeStruct(w.shape, BF16) for w in casts]),
        compiler_params=_params(1),
        name="inproj_raster",
    )(x, mod, mod, gain, *([w_bf] * 6), w_s, b_s_b, v_gain, lb_logits, *casts)


COLS_PER_STEP = 4


def _gather_copies(x_hbm_ref, buf_ref, sem_ref, slot, step):
    rows = x_hbm_ref.shape[0]
    return [pltpu.make_async_copy(x_hbm_ref.at[:, step * COLS_PER_STEP + wi, :],
                                  buf_ref.at[slot, pl.ds(wi * rows, rows), :],
                                  sem_ref.at[slot, wi])
            for wi in range(COLS_PER_STEP)]


def _inproj_colmajor_kernel(x_hbm_ref, sh_ref, sc_ref, gain_ref, wq_ref, wi_ref, wf_ref, wb_ref, lbl_ref,
                            v_ref, *rest):
    fwd_refs, bwd_refs, (xbuf_ref, sem_ref) = rest[:5], rest[5:10], rest[10:]
    step, n_steps = pl.program_id(0), pl.num_programs(0)
    slot = step % 2
    copies = functools.partial(_gather_copies, x_hbm_ref, xbuf_ref, sem_ref)

    @pl.when(step == 0)
    def _():
        for cp in copies(slot, step):
            cp.start()

    @pl.when(step + 1 < n_steps)
    def _():
        for cp in copies(1 - slot, step + 1):
            cp.start()

    for cp in copies(slot, step):
        cp.wait()
    hb = _normed_input(xbuf_ref[slot], gain_ref, sh_ref, sc_ref, 0)
    proj = lambda w_ref: jnp.dot(hb, w_ref[...], preferred_element_type=F32)
    zf = proj(wf_ref)
    zb = proj(wb_ref)
    lb = _lower_bounds(lbl_ref)
    gate_f = _gate_terms(zf, lb[0:1, HALF_W:])
    q = proj(wq_ref)
    b_f = _chunk_log_decay(gate_f, False)
    gate_b = _gate_terms(zb, lb[1:2, HALF_W:])
    vi = proj(wi_ref)
    _emit_scan_operands(q, gate_f, b_f, False, *fwd_refs)
    b_b = _chunk_log_decay(gate_b, True)
    _emit_scan_operands(q, gate_b, b_b, True, *bwd_refs)
    v_ref[...] = vi.astype(v_ref.dtype)


def _inproj_colmajor(x, mod, gain, w_bf, lb_logits):
    t, d = x.shape
    rows = t // GRID_W
    tm = COLS_PER_STEP * rows
    row = lambda i: (i, 0)
    const2 = lambda i: (0, 0)
    w_spec = lambda g: pl.BlockSpec((d, HALF_W), lambda i: (0, 2 * g + 1), pipeline_mode=pl.Buffered(1))
    op_shapes, op_specs = _scan_operand_outputs(t, tm, HALF_W, row, lambda i: (i, 0, 0))
    return pl.pallas_call(
        _inproj_colmajor_kernel,
        grid=(GRID_W // COLS_PER_STEP,),
        in_specs=[pl.BlockSpec(memory_space=pl.ANY),
                  pl.BlockSpec((2, d), lambda i: (0, 0)),
                  pl.BlockSpec((2, d), lambda i: (0, 1)),
                  pl.BlockSpec((1, d), const2),
                  w_spec(2), w_spec(3), w_spec(4), w_spec(5),
                  pl.BlockSpec(lb_logits.shape, lambda i: (0, 0, 0))],
        out_specs=[pl.BlockSpec((tm, HALF_W), row)] + op_specs + op_specs,
        out_shape=[jax.ShapeDtypeStruct((t, HALF_W), BF16)] + op_shapes + op_shapes,
        scratch_shapes=[pltpu.VMEM((2, tm, d), F32), pltpu.SemaphoreType.DMA((2, COLS_PER_STEP))],
        compiler_params=_params(1),
        name="inproj_colmajor",
    )(x.reshape(rows, GRID_W, d), mod, mod, gain, w_bf, w_bf, w_bf, w_bf, lb_logits)


def _inproj_ctx_kernel(x_ref, sh_ref, sc_ref, gain_ref, w_ref, lbl_ref,
                       v_ref, kf_ref, df_ref, kb_ref, db_ref):
    hb = _normed_input(x_ref[...], gain_ref, sh_ref, sc_ref, 1)

    def proj(g):
        return jnp.dot(hb, w_ref[:, g * GROUP_W:(g + 1) * GROUP_W], preferred_element_type=F32)

    v_ref[...] = proj(0).astype(v_ref.dtype)
    lb = _lower_bounds(lbl_ref)
    _scan_operands(None, proj(1), lb[0:1, :], False, None, None, None, kf_ref, df_ref)
    _scan_operands(None, proj(2), lb[1:2, :], True, None, None, None, kb_ref, db_ref)


def _inproj_ctx(ctx, mod, gain, w_bf, lb_logits):
    t, d = ctx.shape
    const2 = lambda i: (0, 0)
    const3 = lambda i: (0, 0, 0)
    data = jax.ShapeDtypeStruct((t, GROUP_W), BF16)
    decay = jax.ShapeDtypeStruct((t // SCAN_CHUNK, SUBLANES, GROUP_W), F32)
    data_spec = pl.BlockSpec(data.shape, const2)
    decay_spec = pl.BlockSpec(decay.shape, const3)
    return pl.pallas_call(
        _inproj_ctx_kernel,
        grid=(1,),
        in_specs=[pl.BlockSpec((t, d), const2),
                  pl.BlockSpec((2, d), lambda i: (0, 0)),
                  pl.BlockSpec((2, d), lambda i: (0, 1)),
                  pl.BlockSpec((1, d), const2),
                  pl.BlockSpec((d, 3 * GROUP_W), lambda i: (0, 1)),
                  pl.BlockSpec(lb_logits.shape, const3)],
        out_specs=[data_spec, data_spec, decay_spec, data_spec, decay_spec],
        out_shape=[data, data, decay, data, decay],
        compiler_params=_params(1),
        name="inproj_ctx",
    )(ctx, mod, mod, gain, w_bf, lb_logits)


def _scan_chunk(groups, state_ref, o_refs):
    units = [(g, h, slice(h * HEAD_DIM, (h + 1) * HEAD_DIM)) for g in range(4) for h in range(HALF_HEADS)]
    with_outputs = len(groups[0]) == 6
    scores = {}
    if with_outputs:
        for g, h, cs in units:
            q_in_ref, k_in_ref = groups[g][:2]
            s = lax.dot_general(q_in_ref[:, cs], k_in_ref[:, cs], (((1,), (1,)), ((), ())),
                                preferred_element_type=F32)
            scores[g, h] = jnp.where(_visible(g >= 2), s, 0.0).astype(BF16)
    outs = {}
    for g, h, cs in units:
        k_st_ref, v_ref, decay_ref = groups[g][-3:]
        u = g * HALF_HEADS + h
        st = state_ref[u]
        vh = v_ref[:, cs]
        if with_outputs:
            q_st_ref = groups[g][2]
            outs[g, h] = jnp.dot(jnp.concatenate([scores[g, h], q_st_ref[:, cs]], axis=1),
                                 jnp.concatenate([vh, st.astype(BF16)], axis=0), preferred_element_type=F32)
        decay = jnp.broadcast_to(decay_ref[0:1, cs], (HEAD_DIM, HEAD_DIM)).T
        state_ref[u] = st * decay + lax.dot_general(
            k_st_ref[:, cs], vh, (((0,), (0,)), ((), ())), preferred_element_type=F32)
    if with_outputs:
        for g, o_ref in enumerate(o_refs):
            o_ref[...] = jnp.concatenate([outs[g, h] for h in range(HALF_HEADS)], axis=1)


def _scatter_copy(buf_ref, out_hbm_ref, sem_ref, slot, chunk):
    per_col = out_hbm_ref.shape[0] // SCAN_CHUNK
    r0 = (chunk % per_col) * SCAN_CHUNK
    dst = out_hbm_ref.at[pl.ds(r0, SCAN_CHUNK), chunk // per_col, :]
    return pltpu.make_async_copy(buf_ref.at[slot], dst, sem_ref.at[slot])


def _scan_kernel(*refs, n_in, emit_outputs):
    groups = [refs[g * n_in:(g + 1) * n_in] for g in range(4)]
    s0_ref, rest = refs[4 * n_in], refs[4 * n_in + 1:]
    step, n_steps = pl.program_id(0), pl.num_programs(0)
    state_ref = rest[4] if emit_outputs else rest[0]

    @pl.when(step == 0)
    def _():
        state_ref[...] = s0_ref[...]

    if not emit_outputs:
        _scan_chunk(groups, state_ref, None)
        return

    o_refs, (bufs, sems) = rest[:4], (rest[5:7], rest[7:9])
    slot = step % 2
    _scan_chunk(groups, state_ref, [o_refs[0], bufs[0].at[slot], o_refs[2], bufs[1].at[slot]])
    for d, backward in enumerate((False, True)):
        chunk_of = (lambda s: n_steps - 1 - s) if backward else (lambda s: s)
        copy = functools.partial(_scatter_copy, bufs[d], o_refs[2 * d + 1], sems[d])
        copy(slot, chunk_of(step)).start()

        @pl.when(step > 0)
        def _():
            copy(1 - slot, chunk_of(step - 1)).wait()

        @pl.when(step == n_steps - 1)
        def _():
            copy(slot, chunk_of(step)).wait()


def _scan(groups, s0, n_steps, out_maps, grid_rows, name):
    blk = (SCAN_CHUNK, HALF_W)
    args, in_specs = [], []
    for arrays, chunk_of, col in groups:
        args.extend(arrays)
        data_map = lambda s, chunk_of=chunk_of, col=col: (chunk_of(s), col)
        decay_map = lambda s, chunk_of=chunk_of, col=col: (chunk_of(s), 0, col)
        in_specs.extend([pl.BlockSpec(blk, data_map)] * (len(arrays) - 1))
        in_specs.append(pl.BlockSpec((None, SUBLANES, HALF_W), decay_map))
    args.append(s0)
    state_spec = pl.BlockSpec(s0.shape, lambda s: (0, 0, 0))
    in_specs.append(state_spec)
    out_specs, out_shape, scratch = [], [], []
    if out_maps is not None:
        raster = jax.ShapeDtypeStruct((grid_rows * GRID_W, HALF_W), F32)
        scattered = jax.ShapeDtypeStruct((grid_rows, GRID_W, HALF_W), F32)
        in_hbm = pl.BlockSpec(memory_space=pl.ANY)
        out_shape = [raster, scattered, raster, scattered]
        out_specs = [pl.BlockSpec(blk, out_maps[0]), in_hbm, pl.BlockSpec(blk, out_maps[1]), in_hbm]
        scratch = [pltpu.VMEM((2,) + blk, F32)] * 2 + [pltpu.SemaphoreType.DMA((2,))] * 2
    out_shape.append(jax.ShapeDtypeStruct(s0.shape, F32))
    out_specs.append(state_spec)
    return pl.pallas_call(
        functools.partial(_scan_kernel, n_in=len(groups[0][0]), emit_outputs=out_maps is not None),
        grid=(n_steps,),
        in_specs=in_specs,
        out_specs=out_specs,
        out_shape=out_shape,
        scratch_shapes=scratch,
        compiler_params=_params(1),
        name=name,
    )(*args)


def _scan_ctx(v, kf, df, kb, db):
    n = v.shape[0] // SCAN_CHUNK
    up, down = (lambda s: s), (lambda s: n - 1 - s)
    groups = [((kf, v, df), up, 0), ((kf, v, df), up, 1), ((kb, v, db), down, 0), ((kb, v, db), down, 1)]
    s0 = jnp.zeros((2 * HEADS, HEAD_DIM, HEAD_DIM), F32)
    (state,) = _scan(groups, s0, n, None, 0, "scan_ctx")
    return state


def _scan_latent(raster, colmajor, s0):
    t = raster[0].shape[0]
    n = t // SCAN_CHUNK
    up, down = (lambda s: s), (lambda s: n - 1 - s)
    groups = []
    for d, chunk_of in enumerate((up, down)):
        for v, *operands in (raster, colmajor):
            q_in, k_in, q_st, k_st, decay = operands[5 * d:5 * d + 5]
            groups.append(((q_in, k_in, q_st, k_st, v, decay), chunk_of, 0))
    out_maps = [lambda s: (s, 0), lambda s: (n - 1 - s, 0)]
    o_fr, o_fc, o_br, o_bc, _ = _scan(groups, s0, n, out_maps, t // GRID_W, "scan_latent")
    return o_fr, o_fc.reshape(t, HALF_W), o_br, o_bc.reshape(t, HALF_W)


def _outproj_kernel(a_ref, ofr_ref, ofc_ref, obr_ref, obc_ref, sg_ref, x_ref, w_ref,
                    g1_ref, sh2_ref, sc2_ref, og_ref, n2g_ref, x1_ref, h2_ref):
    o_r = ofr_ref[...] + obr_ref[...]
    o_c = ofc_ref[...] + obc_ref[...]
    ys = []
    for h in range(HEADS):
        o = o_r if h < HALF_HEADS else o_c
        oh = o[:, (h % HALF_HEADS) * HEAD_DIM:(h % HALF_HEADS + 1) * HEAD_DIM]
        cs = slice(h * HEAD_DIM, (h + 1) * HEAD_DIM)
        ys.append((oh * _rms_scale(oh) * og_ref[:, cs] * sg_ref[:, cs].astype(F32)).astype(BF16))
    y = jnp.concatenate(ys, axis=1)
    proj = jnp.dot(a_ref[...], w_ref[:GROUP_W, :], preferred_element_type=F32)
    proj = proj + jnp.dot(y, w_ref[GROUP_W:, :], preferred_element_type=F32)
    x1 = x_ref[...] + g1_ref[0:1, :] * proj
    x1_ref[...] = x1
    h2 = x1 * _rms_scale(x1) * n2g_ref[...]
    h2_ref[...] = (h2 * (1.0 + sc2_ref[0:1, :]) + sh2_ref[0:1, :]).astype(h2_ref.dtype)


def _outproj(a, o_fr, o_fc, o_br, o_bc, sg, x, w_bf, mod, o_gain, n2_gain):
    t, d = x.shape
    tm = 512
    row = lambda i: (i, 0)
    const2 = lambda i: (0, 0)
    half = pl.BlockSpec((tm, HALF_W), row)
    full = pl.BlockSpec((tm, GROUP_W), row)
    mod_col = lambda c: pl.BlockSpec((2, d), lambda i: (0, c))
    return pl.pallas_call(
        _outproj_kernel,
        grid=(t // tm,),
        in_specs=[full, half, half, half, half, full,
                  pl.BlockSpec((tm, d), row),
                  pl.BlockSpec(w_bf.shape, const2, pipeline_mode=pl.Buffered(1)),
                  mod_col(2), mod_col(3), mod_col(4),
                  pl.BlockSpec((1, GROUP_W), const2),
                  pl.BlockSpec((1, d), const2)],
        out_specs=[pl.BlockSpec((tm, d), row), pl.BlockSpec((tm, d), row)],
        out_shape=[jax.ShapeDtypeStruct((t, d), F32), jax.ShapeDtypeStruct((t, d), BF16)],
        compiler_params=_params(1),
        name="outproj",
    )(a, o_fr, o_fc, o_br, o_bc, sg, x, w_bf, mod, mod, mod, o_gain, n2_gain)


def _ffn_kernel(h_ref, x1_ref, w1_ref, w2_ref, g2_ref, fg_ref, out_ref):
    f = pl.program_id(1)

    @pl.when(f == 0)
    def _():
        out_ref[...] = jnp.zeros_like(out_ref)

    hid = jnp.dot(h_ref[...], w1_ref[...], preferred_element_type=F32)
    hid = jnp.square(jnp.maximum(hid, 0.0)).astype(BF16)
    out_ref[...] += jnp.dot(hid, w2_ref[...], preferred_element_type=F32)

    @pl.when(f == pl.num_programs(1) - 1)
    def _():
        x2 = x1_ref[...] + g2_ref[0:1, :] * out_ref[...]
        out_ref[...] = x2 * _rms_scale(x2) * fg_ref[...]


def _ffn(h2, x1, w1_bf, w2_bf, mod, final_gain):
    t, d = x1.shape
    d_ff = w1_bf.shape[1]
    tm, tf = 1024, 1024
    row = lambda i, f: (i, 0)
    return pl.pallas_call(
        _ffn_kernel,
        grid=(t // tm, d_ff // tf),
        in_specs=[pl.BlockSpec((tm, d), row),
                  pl.BlockSpec((tm, d), row, pipeline_mode=pl.Buffered(1)),
                  pl.BlockSpec((d, tf), lambda i, f: (0, f)),
                  pl.BlockSpec((tf, d), lambda i, f: (f, 0)),
                  pl.BlockSpec((2, d), lambda i, f: (0, 5)),
                  pl.BlockSpec((1, d), lambda i, f: (0, 0))],
        out_specs=pl.BlockSpec((tm, d), row, pipeline_mode=pl.Buffered(1)),
        out_shape=jax.ShapeDtypeStruct((t, d), F32),
        compiler_params=_params(2),
        name="ffn",
    )(h2, x1, w1_bf, w2_bf, mod, final_gain)


def kernel(x, c, ctx, c_ctx, w_ada, b_ada, norm1_gain, w_in, gmlp_w_s, gmlp_b_s, gmlp_v_gain,
           hgrn_lb_logits, hgrn_o_gain, w_out, norm2_gain, w_ff1, w_ff2, final_gain):
    bsz, t, d = x.shape
    assert bsz == 1 and w_ada.shape[0] == 1, "single sample, single layer"
    assert w_in.shape[2] == 7 * GROUP_W and t % (GRID_W * SCAN_CHUNK) == 0

    cond_t = jnp.stack([c[0], c_ctx], axis=1)
    mod = _adaln(cond_t, w_ada[0], b_ada)
    w_in_bf = _cast_bf16(w_in[0], "cast_w_in")

    b_s_b = jnp.broadcast_to(gmlp_b_s[0][:, :, None], (HEADS, GMLP_CHUNK, HEAD_DIM))
    a, sg, *rest = _inproj_raster(x[0], mod, norm1_gain, w_in_bf, gmlp_w_s[0], b_s_b, gmlp_v_gain[0],
                                  hgrn_lb_logits, [w_out[0], w_ff1[0], w_ff2[0]])
    raster, (w_out_bf, w1_bf, w2_bf) = rest[:-N_CASTS], rest[-N_CASTS:]
    colmajor = _inproj_colmajor(x[0], mod, norm1_gain, w_in_bf, hgrn_lb_logits)
    ctx_ops = _inproj_ctx(ctx[0], mod, norm1_gain, w_in_bf, hgrn_lb_logits)

    s0 = _scan_ctx(*ctx_ops)
    o_fr, o_fc, o_br, o_bc = _scan_latent(raster, colmajor, s0)

    x1, h2 = _outproj(a, o_fr, o_fc, o_br, o_bc, sg, x[0], w_out_bf, mod,
                      hgrn_o_gain.reshape(1, GROUP_W), norm2_gain)
    out = _ffn(h2, x1, w1_bf, w2_bf, mod, final_gain.reshape(1, d))
    return out[None]
```

```python
import functools

import jax
import jax.numpy as jnp
from jax import lax
from jax.experimental import pallas as pl
from jax.experimental.pallas import tpu as pltpu

EPS = 1e-6
GRID_W = 64
HEADS = 8
HALF_HEADS = HEADS // 2
HEAD_DIM = 128
GROUP_W = HEADS * HEAD_DIM
HALF_W = HALF_HEADS * HEAD_DIM
GMLP_CHUNK = 128
SCAN_CHUNK = 128
LANES = 128
SUBLANES = 8
MXU_COLS = 256
VMEM_LIMIT = 56 * 1024 * 1024

F32 = jnp.float32
BF16 = jnp.bfloat16


def _params(n_axes, vmem=VMEM_LIMIT):
    return pltpu.CompilerParams(dimension_semantics=("arbitrary",) * n_axes, vmem_limit_bytes=vmem)


def _sigmoid(x):
    return 1.0 / (1.0 + jnp.exp(-x))


def _gelu_tanh(x):
    return x * (0.5 * (1.0 + jnp.tanh(0.7978845608028654 * (x + 0.044715 * (x * x * x)))))


def _rms_scale(x):
    return lax.rsqrt(jnp.mean(x * x, axis=-1, keepdims=True) + EPS)


def _adaln_kernel(cond_t_ref, w_ref, b_ref, out_ref, s_ref):
    @pl.when(pl.program_id(0) == 0)
    def _():
        ct = cond_t_ref[...]
        s = ct * _sigmoid(ct)
        for r in range(2):
            s_ref[r] = jnp.broadcast_to(s[:, r:r + 1], s_ref.shape[1:])

    tn = w_ref.shape[1]
    for cb in range(tn // LANES):
        w = w_ref[:, cb * LANES:(cb + 1) * LANES]
        rows = [jnp.sum(w * s_ref[r], axis=0, keepdims=True) for r in range(2)]
        out_ref[:, cb * LANES:(cb + 1) * LANES] = (
            jnp.concatenate(rows, axis=0) + b_ref[:, cb * LANES:(cb + 1) * LANES])


def _adaln(cond_t, w_ada, b_ada):
    d, n = w_ada.shape
    tn = 1024
    return pl.pallas_call(
        _adaln_kernel,
        grid=(n // tn,),
        in_specs=[pl.BlockSpec((d, 2), lambda j: (0, 0)),
                  pl.BlockSpec((d, tn), lambda j: (0, j)),
                  pl.BlockSpec((1, tn), lambda j: (0, j))],
        out_specs=pl.BlockSpec((2, tn), lambda j: (0, j)),
        out_shape=jax.ShapeDtypeStruct((2, n), F32),
        scratch_shapes=[pltpu.VMEM((2, d, LANES), F32)],
        compiler_params=_params(1),
        name="adaln",
    )(cond_t, w_ada, b_ada)


def _cast_kernel(w_ref, o_ref):
    o_ref[...] = w_ref[...].astype(o_ref.dtype)


def _cast_bf16(w, name):
    k, n = w.shape
    tk = 256
    return pl.pallas_call(
        _cast_kernel,
        grid=(k // tk,),
        in_specs=[pl.BlockSpec((tk, n), lambda i: (i, 0))],
        out_specs=pl.BlockSpec((tk, n), lambda i: (i, 0)),
        out_shape=jax.ShapeDtypeStruct((k, n), BF16),
        compiler_params=_params(1),
        name=name,
    )(w)


def _lower_bounds(lbl_ref):
    l0, l1 = lbl_ref[0], lbl_ref[1]
    m = jnp.maximum(l0, l1)
    e0, e1 = jnp.exp(l0 - m), jnp.exp(l1 - m)
    return e1 / (e0 + e1)


def _visible(backward):
    ti = lax.broadcasted_iota(jnp.int32, (SCAN_CHUNK, SCAN_CHUNK), 0)
    si = lax.broadcasted_iota(jnp.int32, (SCAN_CHUNK, SCAN_CHUNK), 1)
    return (si >= ti) if backward else (si <= ti)


def _gate_terms(z, lb):
    f = lb + (1.0 - lb) * _sigmoid(z)
    lf = jnp.log(f)
    hi = lf.astype(BF16)
    lo = (lf - hi.astype(F32)).astype(BF16)
    return 1.0 - f, hi, lo


def _chunk_log_decay(gate, backward):
    _, hi, lo = gate
    tri = _visible(backward).astype(BF16)
    c = SCAN_CHUNK
    return [jnp.dot(tri, hi[j * c:(j + 1) * c], preferred_element_type=F32)
            + jnp.dot(tri, lo[j * c:(j + 1) * c], preferred_element_type=F32)
            for j in range(hi.shape[0] // c)]


N_OPERANDS = 4


def _emit_scan_operands(q, gate, bs, backward, out_ref, decay_ref):
    k = gate[0]
    c = SCAN_CHUNK
    w = k.shape[1]
    mid_row = c // 2 - 1 if backward else c // 2
    last_row = 0 if backward else c - 1
    for j, b in enumerate(bs):
        rs = slice(j * c, (j + 1) * c)
        b_mid = b[mid_row:mid_row + 1, :]
        b_last = b[last_row:last_row + 1, :]
        if q is None:
            out_ref[rs, :] = (k[rs] * jnp.exp(b_last - b)).astype(BF16)
        else:
            k_in = k[rs] * jnp.exp(b_mid - b)
            q_in = q[rs] * jnp.exp(b - b_mid)
            out_ref[rs, 0:w] = q_in.astype(BF16)
            out_ref[rs, w:2 * w] = k_in.astype(BF16)
            out_ref[rs, 2 * w:3 * w] = (q_in * jnp.exp(b_mid)).astype(BF16)
            out_ref[rs, 3 * w:4 * w] = (k_in * jnp.exp(b_last - b_mid)).astype(BF16)
        decay_ref[j] = jnp.broadcast_to(jnp.exp(b_last), decay_ref.shape[1:])


def _scan_operands(q, z, lb, backward, out_ref, decay_ref):
    gate = _gate_terms(z, lb)
    _emit_scan_operands(q, gate, _chunk_log_decay(gate, backward), backward, out_ref, decay_ref)


def _normed_input(x, gain_ref, sh_ref, sc_ref, row):
    h = x * _rms_scale(x) * gain_ref[...]
    h = h * (1.0 + sc_ref[row:row + 1, :]) + sh_ref[row:row + 1, :]
    return h.astype(BF16)


N_CASTS = 3


def _inproj_raster_kernel(x_ref, sh_ref, sc_ref, gain_ref, wuv_ref, wg_ref, wq_ref, wi_ref, wf_ref, wb_ref,
                          ws_ref, bs_ref, vg_ref, lbl_ref, *rest):
    cast_ins, rest = rest[:N_CASTS], rest[N_CASTS:]
    a_ref, sg_ref, v_ref = rest[:3]
    fwd_refs, bwd_refs, cast_outs = rest[3:5], rest[5:7], rest[7:]
    hb = _normed_input(x_ref[...], gain_ref, sh_ref, sc_ref, 0)
    proj = lambda w_ref: jnp.dot(hb, w_ref[...], preferred_element_type=F32)

    q = proj(wq_ref)
    zf = proj(wf_ref)
    zb = proj(wb_ref)
    lb = _lower_bounds(lbl_ref)
    gate_f = _gate_terms(zf, lb[0:1, :HALF_W])
    u = jnp.dot(hb, wuv_ref[:, :GROUP_W], preferred_element_type=F32)
    b_f = _chunk_log_decay(gate_f, False)
    gate_b = _gate_terms(zb, lb[1:2, :HALF_W])
    v = jnp.dot(hb, wuv_ref[:, GROUP_W:], preferred_element_type=F32)
    _emit_scan_operands(q, gate_f, b_f, False, *fwd_refs)
    b_b = _chunk_log_decay(gate_b, True)
    g = proj(wg_ref)
    _emit_scan_operands(q, gate_b, b_b, True, *bwd_refs)
    for w_ref, wb16_ref in zip(cast_ins, cast_outs):
        wb16_ref[...] = w_ref[...].astype(wb16_ref.dtype)
    u = _gelu_tanh(u)
    v = _gelu_tanh(v)

    tm = u.shape[0]
    for h in range(HEADS):
        cs = slice(h * HEAD_DIM, (h + 1) * HEAD_DIM)
        vh = v[:, cs]
        vn = (vh * _rms_scale(vh) * vg_ref[h:h + 1, :]).astype(BF16)
        w_s = ws_ref[h].astype(BF16)
        for c in range(tm // GMLP_CHUNK):
            rs = slice(c * GMLP_CHUNK, (c + 1) * GMLP_CHUNK)
            mixed = jnp.dot(w_s, vn[rs], preferred_element_type=F32) + bs_ref[h]
            a_ref[rs, cs] = (u[rs, cs] * mixed).astype(a_ref.dtype)

    vi = proj(wi_ref)
    sg_ref[...] = (g * _sigmoid(g)).astype(sg_ref.dtype)
    v_ref[...] = vi.astype(v_ref.dtype)


def _scan_operand_outputs(t, tm, width, index_map, decay_map):
    data = jax.ShapeDtypeStruct((t, N_OPERANDS * width), BF16)
    decay = jax.ShapeDtypeStruct((t // SCAN_CHUNK, SUBLANES, width), F32)
    data_spec = pl.BlockSpec((tm, N_OPERANDS * width), index_map)
    decay_spec = pl.BlockSpec((tm // SCAN_CHUNK, SUBLANES, width), decay_map)
    return [data, decay], [data_spec, decay_spec]


def _inproj_raster(x, mod, gain, w_bf, w_s, b_s_b, v_gain, lb_logits, casts):
    t, d = x.shape
    tm = 256
    n_steps = t // tm
    row = lambda i: (i, 0)
    const2 = lambda i: (0, 0)
    const3 = lambda i: (0, 0, 0)
    full = pl.BlockSpec((tm, GROUP_W), row)
    half = pl.BlockSpec((tm, HALF_W), row)
    resident = lambda width, blk: pl.BlockSpec((d, width), lambda i: (0, blk), pipeline_mode=pl.Buffered(1))
    cast_specs = [pl.BlockSpec((w.shape[0] // n_steps, w.shape[1]), row) for w in casts]
    op_shapes, op_specs = _scan_operand_outputs(t, tm, HALF_W, row, lambda i: (i, 0, 0))
    return pl.pallas_call(
        _inproj_raster_kernel,
        grid=(n_steps,),
        in_specs=[pl.BlockSpec((tm, d), row),
                  pl.BlockSpec((2, d), lambda i: (0, 0)),
                  pl.BlockSpec((2, d), lambda i: (0, 1)),
                  pl.BlockSpec((1, d), const2),
                  resident(2 * GROUP_W, 0), resident(GROUP_W, 6),
                  resident(HALF_W, 4), resident(HALF_W, 6), resident(HALF_W, 8), resident(HALF_W, 10),
                  pl.BlockSpec(w_s.shape, const3),
                  pl.BlockSpec(b_s_b.shape, const3),
                  pl.BlockSpec(v_gain.shape, const2),
                  pl.BlockSpec(lb_logits.shape, const3)] + cast_specs,
        out_specs=[full, full, half] + op_specs + op_specs + cast_specs,
        out_shape=([jax.ShapeDtypeStruct((t, GROUP_W), BF16)] * 2 + [jax.ShapeDtypeStruct((t, HALF_W), BF16)]
                   + op_shapes + op_shapes + [jax.ShapeDtypeStruct(w.shape, BF16) for w in casts]),
        compiler_params=_params(1),
        name="inproj_raster",
    )(x, mod, mod, gain, *([w_bf] * 6), w_s, b_s_b, v_gain, lb_logits, *casts)


COLS_PER_STEP = 4


def _gather_copies(x_hbm_ref, buf_ref, sem_ref, slot, step):
    rows = x_hbm_ref.shape[0]
    return [pltpu.make_async_copy(x_hbm_ref.at[:, step * COLS_PER_STEP + wi, :],
                                  buf_ref.at[slot, pl.ds(wi * rows, rows), :],
                                  sem_ref.at[slot, wi])
            for wi in range(COLS_PER_STEP)]


def _inproj_colmajor_kernel(x_hbm_ref, sh_ref, sc_ref, gain_ref, wq_ref, wi_ref, wf_ref, wb_ref, lbl_ref,
                            v_ref, *rest):
    fwd_refs, bwd_refs, (xbuf_ref, sem_ref) = rest[:2], rest[2:4], rest[4:]
    step, n_steps = pl.program_id(0), pl.num_programs(0)
    slot = step % 2
    copies = functools.partial(_gather_copies, x_hbm_ref, xbuf_ref, sem_ref)

    @pl.when(step == 0)
    def _():
        for cp in copies(slot, step):
            cp.start()

    @pl.when(step + 1 < n_steps)
    def _():
        for cp in copies(1 - slot, step + 1):
            cp.start()

    for cp in copies(slot, step):
        cp.wait()
    hb = _normed_input(xbuf_ref[slot], gain_ref, sh_ref, sc_ref, 0)
    proj = lambda w_ref: jnp.dot(hb, w_ref[...], preferred_element_type=F32)
    zf = proj(wf_ref)
    zb = proj(wb_ref)
    lb = _lower_bounds(lbl_ref)
    gate_f = _gate_terms(zf, lb[0:1, HALF_W:])
    q = proj(wq_ref)
    b_f = _chunk_log_decay(gate_f, False)
    gate_b = _gate_terms(zb, lb[1:2, HALF_W:])
    vi = proj(wi_ref)
    _emit_scan_operands(q, gate_f, b_f, False, *fwd_refs)
    b_b = _chunk_log_decay(gate_b, True)
    _emit_scan_operands(q, gate_b, b_b, True, *bwd_refs)
    v_ref[...] = vi.astype(v_ref.dtype)


def _inproj_colmajor(x, mod, gain, w_bf, lb_logits):
    t, d = x.shape
    rows = t // GRID_W
    tm = COLS_PER_STEP * rows
    row = lambda i: (i, 0)
    const2 = lambda i: (0, 0)
    w_spec = lambda g: pl.BlockSpec((d, HALF_W), lambda i: (0, 2 * g + 1), pipeline_mode=pl.Buffered(1))
    op_shapes, op_specs = _scan_operand_outputs(t, tm, HALF_W, row, lambda i: (i, 0, 0))
    return pl.pallas_call(
        _inproj_colmajor_kernel,
        grid=(GRID_W // COLS_PER_STEP,),
        in_specs=[pl.BlockSpec(memory_space=pl.ANY),
                  pl.BlockSpec((2, d), lambda i: (0, 0)),
                  pl.BlockSpec((2, d), lambda i: (0, 1)),
                  pl.BlockSpec((1, d), const2),
                  w_spec(2), w_spec(3), w_spec(4), w_spec(5),
                  pl.BlockSpec(lb_logits.shape, lambda i: (0, 0, 0))],
        out_specs=[pl.BlockSpec((tm, HALF_W), row)] + op_specs + op_specs,
        out_shape=[jax.ShapeDtypeStruct((t, HALF_W), BF16)] + op_shapes + op_shapes,
        scratch_shapes=[pltpu.VMEM((2, tm, d), F32), pltpu.SemaphoreType.DMA((2, COLS_PER_STEP))],
        compiler_params=_params(1),
        name="inproj_colmajor",
    )(x.reshape(rows, GRID_W, d), mod, mod, gain, w_bf, w_bf, w_bf, w_bf, lb_logits)


def _inproj_ctx_kernel(x_ref, sh_ref, sc_ref, gain_ref, w_ref, lbl_ref,
                       v_ref, kf_ref, df_ref, kb_ref, db_ref):
    hb = _normed_input(x_ref[...], gain_ref, sh_ref, sc_ref, 1)

    def proj(g):
        return jnp.dot(hb, w_ref[:, g * GROUP_W:(g + 1) * GROUP_W], preferred_element_type=F32)

    v_ref[...] = proj(0).astype(v_ref.dtype)
    lb = _lower_bounds(lbl_ref)
    _scan_operands(None, proj(1), lb[0:1, :], False, kf_ref, df_ref)
    _scan_operands(None, proj(2), lb[1:2, :], True, kb_ref, db_ref)


def _inproj_ctx(ctx, mod, gain, w_bf, lb_logits):
    t, d = ctx.shape
    const2 = lambda i: (0, 0)
    const3 = lambda i: (0, 0, 0)
    data = jax.ShapeDtypeStruct((t, GROUP_W), BF16)
    decay = jax.ShapeDtypeStruct((t // SCAN_CHUNK, SUBLANES, GROUP_W), F32)
    data_spec = pl.BlockSpec(data.shape, const2)
    decay_spec = pl.BlockSpec(decay.shape, const3)
    return pl.pallas_call(
        _inproj_ctx_kernel,
        grid=(1,),
        in_specs=[pl.BlockSpec((t, d), const2),
                  pl.BlockSpec((2, d), lambda i: (0, 0)),
                  pl.BlockSpec((2, d), lambda i: (0, 1)),
                  pl.BlockSpec((1, d), const2),
                  pl.BlockSpec((d, 3 * GROUP_W), lambda i: (0, 1)),
                  pl.BlockSpec(lb_logits.shape, const3)],
        out_specs=[data_spec, data_spec, decay_spec, data_spec, decay_spec],
        out_shape=[data, data, decay, data, decay],
        compiler_params=_params(1),
        name="inproj_ctx",
    )(ctx, mod, mod, gain, w_bf, lb_logits)


def _scan_chunk(groups, state_ref, o_refs):
    units = [(g, h, slice(h * HEAD_DIM, (h + 1) * HEAD_DIM)) for g in range(4) for h in range(HALF_HEADS)]
    with_outputs = o_refs is not None

    def operand(g, j, cs):
        return groups[g][0][:, j * HALF_W + cs.start:j * HALF_W + cs.stop]

    scores = {}
    if with_outputs:
        for g, h, cs in units:
            s = lax.dot_general(operand(g, 0, cs), operand(g, 1, cs), (((1,), (1,)), ((), ())),
                                preferred_element_type=F32)
            scores[g, h] = jnp.where(_visible(g >= 2), s, 0.0).astype(BF16)
    outs = {}
    for g, h, cs in units:
        ops_ref, v_ref, decay_ref = groups[g]
        u = g * HALF_HEADS + h
        st = state_ref[u]
        vh = v_ref[:, cs]
        if with_outputs:
            outs[g, h] = jnp.dot(jnp.concatenate([scores[g, h], operand(g, 2, cs)], axis=1),
                                 jnp.concatenate([vh, st.astype(BF16)], axis=0), preferred_element_type=F32)
        decay = jnp.broadcast_to(decay_ref[0:1, cs], (HEAD_DIM, HEAD_DIM)).T
        k_st = operand(g, 3, cs) if with_outputs else ops_ref[:, cs]
        state_ref[u] = st * decay + lax.dot_general(
            k_st, vh, (((0,), (0,)), ((), ())), preferred_element_type=F32)
    if with_outputs:
        for g, o_ref in enumerate(o_refs):
            o_ref[...] = jnp.concatenate([outs[g, h] for h in range(HALF_HEADS)], axis=1).astype(o_ref.dtype)


def _scatter_copy(buf_ref, out_hbm_ref, sem_ref, slot, chunk):
    per_col = out_hbm_ref.shape[0] // SCAN_CHUNK
    r0 = (chunk % per_col) * SCAN_CHUNK
    dst = out_hbm_ref.at[pl.ds(r0, SCAN_CHUNK), chunk // per_col, :]
    return pltpu.make_async_copy(buf_ref.at[slot], dst, sem_ref.at[slot])


def _scan_kernel(*refs, emit_outputs):
    groups = [refs[3 * g:3 * g + 3] for g in range(4)]
    s0_ref, rest = refs[12], refs[13:]
    step, n_steps = pl.program_id(0), pl.num_programs(0)
    state_ref = rest[4] if emit_outputs else rest[0]

    @pl.when(step == 0)
    def _():
        state_ref[...] = s0_ref[...]

    if not emit_outputs:
        _scan_chunk(groups, state_ref, None)
        return

    o_refs, (bufs, sems) = rest[:4], (rest[5:7], rest[7:9])
    slot = step % 2
    _scan_chunk(groups, state_ref, [o_refs[0], bufs[0].at[slot], o_refs[2], bufs[1].at[slot]])
    for d, backward in enumerate((False, True)):
        chunk_of = (lambda s: n_steps - 1 - s) if backward else (lambda s: s)
        copy = functools.partial(_scatter_copy, bufs[d], o_refs[2 * d + 1], sems[d])
        copy(slot, chunk_of(step)).start()

        @pl.when(step > 0)
        def _():
            copy(1 - slot, chunk_of(step - 1)).wait()

        @pl.when(step == n_steps - 1)
        def _():
            copy(slot, chunk_of(step)).wait()


def _scan(groups, s0, n_steps, out_maps, grid_rows, name):
    blk = (SCAN_CHUNK, HALF_W)
    args, in_specs = [], []
    for arrays, chunk_of, col in groups:
        args.extend(arrays)
        data_map = lambda s, chunk_of=chunk_of, col=col: (chunk_of(s), col)
        decay_map = lambda s, chunk_of=chunk_of, col=col: (chunk_of(s), 0, col)
        operands, _, _ = arrays
        packed = operands.shape[1] == N_OPERANDS * HALF_W
        in_specs.append(pl.BlockSpec((SCAN_CHUNK, N_OPERANDS * HALF_W), lambda s, chunk_of=chunk_of:
                                     (chunk_of(s), 0)) if packed else pl.BlockSpec(blk, data_map))
        in_specs.append(pl.BlockSpec(blk, data_map))
        in_specs.append(pl.BlockSpec((None, SUBLANES, HALF_W), decay_map))
    args.append(s0)
    state_spec = pl.BlockSpec(s0.shape, lambda s: (0, 0, 0))
    in_specs.append(state_spec)
    out_specs, out_shape, scratch = [], [], []
    if out_maps is not None:
        raster = jax.ShapeDtypeStruct((grid_rows * GRID_W, HALF_W), BF16)
        scattered = jax.ShapeDtypeStruct((grid_rows, GRID_W, HALF_W), F32)
        in_hbm = pl.BlockSpec(memory_space=pl.ANY)
        out_shape = [raster, scattered, raster, scattered]
        out_specs = [pl.BlockSpec(blk, out_maps[0]), in_hbm, pl.BlockSpec(blk, out_maps[1]), in_hbm]
        scratch = [pltpu.VMEM((2,) + blk, F32)] * 2 + [pltpu.SemaphoreType.DMA((2,))] * 2
    out_shape.append(jax.ShapeDtypeStruct(s0.shape, F32))
    out_specs.append(state_spec)
    return pl.pallas_call(
        functools.partial(_scan_kernel, emit_outputs=out_maps is not None),
        grid=(n_steps,),
        in_specs=in_specs,
        out_specs=out_specs,
        out_shape=out_shape,
        scratch_shapes=scratch,
        compiler_params=_params(1),
        name=name,
    )(*args)


def _scan_ctx(v, kf, df, kb, db):
    n = v.shape[0] // SCAN_CHUNK
    up, down = (lambda s: s), (lambda s: n - 1 - s)
    groups = [((kf, v, df), up, 0), ((kf, v, df), up, 1), ((kb, v, db), down, 0), ((kb, v, db), down, 1)]
    s0 = jnp.zeros((2 * HEADS, HEAD_DIM, HEAD_DIM), F32)
    (state,) = _scan(groups, s0, n, None, 0, "scan_ctx")
    return state


def _scan_latent(raster, colmajor, s0):
    t = raster[0].shape[0]
    n = t // SCAN_CHUNK
    up, down = (lambda s: s), (lambda s: n - 1 - s)
    groups = []
    for d, chunk_of in enumerate((up, down)):
        for v, *operands in (raster, colmajor):
            packed, decay = operands[2 * d:2 * d + 2]
            groups.append(((packed, v, decay), chunk_of, 0))
    out_maps = [lambda s: (s, 0), lambda s: (n - 1 - s, 0)]
    o_fr, o_fc, o_br, o_bc, _ = _scan(groups, s0, n, out_maps, t // GRID_W, "scan_latent")
    return o_fr, o_fc.reshape(t, HALF_W), o_br, o_bc.reshape(t, HALF_W)


def _outproj_kernel(a_ref, ofr_ref, ofc_ref, obr_ref, obc_ref, sg_ref, x_ref, w_ref,
                    g1_ref, sh2_ref, sc2_ref, og_ref, n2g_ref, x1_ref, h2_ref):
    def readout(rs):
        o_r = ofr_ref[rs, :].astype(F32) + obr_ref[rs, :].astype(F32)
        o_c = ofc_ref[rs, :] + obc_ref[rs, :]
        ys = []
        for h in range(HEADS):
            o = o_r if h < HALF_HEADS else o_c
            oh = o[:, (h % HALF_HEADS) * HEAD_DIM:(h % HALF_HEADS + 1) * HEAD_DIM]
            cs = slice(h * HEAD_DIM, (h + 1) * HEAD_DIM)
            ys.append((oh * _rms_scale(oh) * og_ref[:, cs] * sg_ref[rs, cs].astype(F32)).astype(BF16))
        return jnp.concatenate(ys, axis=1)

    def finish(rs, proj):
        x1 = x_ref[rs, :] + g1_ref[0:1, :] * proj
        x1_ref[rs, :] = x1
        h2 = x1 * _rms_scale(x1) * n2g_ref[...]
        h2_ref[rs, :] = (h2 * (1.0 + sc2_ref[0:1, :]) + sh2_ref[0:1, :]).astype(h2_ref.dtype)

    half = x_ref.shape[0] // 2
    top, bottom = slice(0, half), slice(half, 2 * half)
    gmlp_side = lambda rs: jnp.dot(a_ref[rs, :], w_ref[:GROUP_W, :], preferred_element_type=F32)
    hgrn_side = lambda y: jnp.dot(y, w_ref[GROUP_W:, :], preferred_element_type=F32)
    p_top = gmlp_side(top)
    y_top = readout(top)
    p_bottom = gmlp_side(bottom)
    y_bottom = readout(bottom)
    p_top = p_top + hgrn_side(y_top)
    p_bottom = p_bottom + hgrn_side(y_bottom)
    finish(top, p_top)
    finish(bottom, p_bottom)


def _outproj(a, o_fr, o_fc, o_br, o_bc, sg, x, w_bf, mod, o_gain, n2_gain):
    t, d = x.shape
    tm = 512
    row = lambda i: (i, 0)
    const2 = lambda i: (0, 0)
    half = pl.BlockSpec((tm, HALF_W), row)
    full = pl.BlockSpec((tm, GROUP_W), row)
    mod_col = lambda c: pl.BlockSpec((2, d), lambda i: (0, c))
    return pl.pallas_call(
        _outproj_kernel,
        grid=(t // tm,),
        in_specs=[full, half, half, half, half, full,
                  pl.BlockSpec((tm, d), row),
                  pl.BlockSpec(w_bf.shape, const2, pipeline_mode=pl.Buffered(1)),
                  mod_col(2), mod_col(3), mod_col(4),
                  pl.BlockSpec((1, GROUP_W), const2),
                  pl.BlockSpec((1, d), const2)],
        out_specs=[pl.BlockSpec((tm, d), row), pl.BlockSpec((tm, d), row)],
        out_shape=[jax.ShapeDtypeStruct((t, d), F32), jax.ShapeDtypeStruct((t, d), BF16)],
        compiler_params=_params(1),
        name="outproj",
    )(a, o_fr, o_fc, o_br, o_bc, sg, x, w_bf, mod, mod, mod, o_gain, n2_gain)


def _ffn_kernel(h_ref, x1_ref, w1_ref, w2_ref, g2_ref, fg_ref, out_ref):
    f = pl.program_id(1)

    @pl.when(f == 0)
    def _():
        out_ref[...] = jnp.zeros_like(out_ref)

    hid = jnp.dot(h_ref[...], w1_ref[...], preferred_element_type=F32)
    hid = jnp.square(jnp.maximum(hid, 0.0)).astype(BF16)
    out_ref[...] += jnp.dot(hid, w2_ref[...], preferred_element_type=F32)

    @pl.when(f == pl.num_programs(1) - 1)
    def _():
        x2 = x1_ref[...] + g2_ref[0:1, :] * out_ref[...]
        out_ref[...] = x2 * _rms_scale(x2) * fg_ref[...]


def _ffn(h2, x1, w1_bf, w2_bf, mod, final_gain):
    t, d = x1.shape
    d_ff = w1_bf.shape[1]
    tm, tf = 1024, 1024
    row = lambda i, f: (i, 0)
    return pl.pallas_call(
        _ffn_kernel,
        grid=(t // tm, d_ff // tf),
        in_specs=[pl.BlockSpec((tm, d), row),
                  pl.BlockSpec((tm, d), row, pipeline_mode=pl.Buffered(1)),
                  pl.BlockSpec((d, tf), lambda i, f: (0, f)),
                  pl.BlockSpec((tf, d), lambda i, f: (f, 0)),
                  pl.BlockSpec((2, d), lambda i, f: (0, 5)),
                  pl.BlockSpec((1, d), lambda i, f: (0, 0))],
        out_specs=pl.BlockSpec((tm, d), row, pipeline_mode=pl.Buffered(1)),
        out_shape=jax.ShapeDtypeStruct((t, d), F32),
        compiler_params=_params(2),
        name="ffn",
    )(h2, x1, w1_bf, w2_bf, mod, final_gain)


def kernel(x, c, ctx, c_ctx, w_ada, b_ada, norm1_gain, w_in, gmlp_w_s, gmlp_b_s, gmlp_v_gain,
           hgrn_lb_logits, hgrn_o_gain, w_out, norm2_gain, w_ff1, w_ff2, final_gain):
    bsz, t, d = x.shape
    assert bsz == 1 and w_ada.shape[0] == 1, "single sample, single layer"
    assert w_in.shape[2] == 7 * GROUP_W and t % (GRID_W * SCAN_CHUNK) == 0

    cond_t = jnp.stack([c[0], c_ctx], axis=1)
    mod = _adaln(cond_t, w_ada[0], b_ada)
    w_in_bf = _cast_bf16(w_in[0], "cast_w_in")

    b_s_b = jnp.broadcast_to(gmlp_b_s[0][:, :, None], (HEADS, GMLP_CHUNK, HEAD_DIM))
    a, sg, *rest = _inproj_raster(x[0], mod, norm1_gain, w_in_bf, gmlp_w_s[0], b_s_b, gmlp_v_gain[0],
                                  hgrn_lb_logits, [w_out[0], w_ff1[0], w_ff2[0]])
    raster, (w_out_bf, w1_bf, w2_bf) = rest[:-N_CASTS], rest[-N_CASTS:]
    colmajor = _inproj_colmajor(x[0], mod, norm1_gain, w_in_bf, hgrn_lb_logits)
    ctx_ops = _inproj_ctx(ctx[0], mod, norm1_gain, w_in_bf, hgrn_lb_logits)

    s0 = _scan_ctx(*ctx_ops)
    o_fr, o_fc, o_br, o_bc = _scan_latent(raster, colmajor, s0)

    x1, h2 = _outproj(a, o_fr, o_fc, o_br, o_bc, sg, x[0], w_out_bf, mod,
                      hgrn_o_gain.reshape(1, GROUP_W), norm2_gain)
    out = _ffn(h2, x1, w1_bf, w2_bf, mod, final_gain.reshape(1, d))
    return out[None]
```

```python
import functools

import jax
import jax.numpy as jnp
from jax import lax
from jax.experimental import pallas as pl
from jax.experimental.pallas import tpu as pltpu

EPS = 1e-6
GRID_W = 64
HEADS = 8
HALF_HEADS = HEADS // 2
HEAD_DIM = 128
GROUP_W = HEADS * HEAD_DIM
HALF_W = HALF_HEADS * HEAD_DIM
GMLP_CHUNK = 128
SCAN_CHUNK = 128
LANES = 128
N_BLOCKS = 2
SCORE_BLOCK = SCAN_CHUNK // N_BLOCKS
TABLE_ROWS = 8
VMEM_LIMIT = 56 * 1024 * 1024

F32 = jnp.float32
BF16 = jnp.bfloat16


def _params(n_axes, vmem=VMEM_LIMIT):
    return pltpu.CompilerParams(dimension_semantics=("arbitrary",) * n_axes, vmem_limit_bytes=vmem)


def _sigmoid(x):
    return 1.0 / (1.0 + jnp.exp(-x))


def _gelu_tanh(x):
    return x * (0.5 * (1.0 + jnp.tanh(0.7978845608028654 * (x + 0.044715 * (x * x * x)))))


def _rms_scale(x):
    return lax.rsqrt(jnp.mean(x * x, axis=-1, keepdims=True) + EPS)


def _adaln_kernel(cond_t_ref, w_ref, b_ref, out_ref, s_ref):
    @pl.when(pl.program_id(0) == 0)
    def _():
        ct = cond_t_ref[...]
        s = ct * _sigmoid(ct)
        for r in range(2):
            s_ref[r] = jnp.broadcast_to(s[:, r:r + 1], s_ref.shape[1:])

    tn = w_ref.shape[1]
    for cb in range(tn // LANES):
        w = w_ref[:, cb * LANES:(cb + 1) * LANES]
        rows = [jnp.sum(w * s_ref[r], axis=0, keepdims=True) for r in range(2)]
        out_ref[:, cb * LANES:(cb + 1) * LANES] = (
            jnp.concatenate(rows, axis=0) + b_ref[:, cb * LANES:(cb + 1) * LANES])


def _adaln(cond_t, w_ada, b_ada):
    d, n = w_ada.shape
    tn = 1024
    return pl.pallas_call(
        _adaln_kernel,
        grid=(n // tn,),
        in_specs=[pl.BlockSpec((d, 2), lambda j: (0, 0)),
                  pl.BlockSpec((d, tn), lambda j: (0, j)),
                  pl.BlockSpec((1, tn), lambda j: (0, j))],
        out_specs=pl.BlockSpec((2, tn), lambda j: (0, j)),
        out_shape=jax.ShapeDtypeStruct((2, n), F32),
        scratch_shapes=[pltpu.VMEM((2, d, LANES), F32)],
        compiler_params=_params(1),
        name="adaln",
    )(cond_t, w_ada, b_ada)


def _cast_kernel(w_ref, o_ref):
    o_ref[...] = w_ref[...].astype(o_ref.dtype)


def _cast_bf16(w, name):
    k, n = w.shape
    tk = 256
    return pl.pallas_call(
        _cast_kernel,
        grid=(k // tk,),
        in_specs=[pl.BlockSpec((tk, n), lambda i: (i, 0))],
        out_specs=pl.BlockSpec((tk, n), lambda i: (i, 0)),
        out_shape=jax.ShapeDtypeStruct((k, n), BF16),
        compiler_params=_params(1),
        name=name,
    )(w)


def _lower_bounds(lbl_ref):
    l0, l1 = lbl_ref[0], lbl_ref[1]
    m = jnp.maximum(l0, l1)
    e0, e1 = jnp.exp(l0 - m), jnp.exp(l1 - m)
    return e1 / (e0 + e1)


def _visible(backward):
    ti = lax.broadcasted_iota(jnp.int32, (SCAN_CHUNK, SCAN_CHUNK), 0)
    si = lax.broadcasted_iota(jnp.int32, (SCAN_CHUNK, SCAN_CHUNK), 1)
    return (si >= ti) if backward else (si <= ti)


def _gate_terms(z, lb):
    f = lb + (1.0 - lb) * _sigmoid(z)
    lf = jnp.log(f)
    hi = lf.astype(BF16)
    lo = (lf - hi.astype(F32)).astype(BF16)
    return 1.0 - f, hi, lo


def _chunk_log_decay(gate, backward):
    _, hi, lo = gate
    tri = _visible(backward).astype(BF16)
    c = SCAN_CHUNK
    return [jnp.dot(tri, hi[j * c:(j + 1) * c], preferred_element_type=F32)
            + jnp.dot(tri, lo[j * c:(j + 1) * c], preferred_element_type=F32)
            for j in range(hi.shape[0] // c)]


N_OPERANDS = 4


def _scan_block(r, backward):
    return N_BLOCKS - 1 - r if backward else r


def _emit_scan_operands(q, gate, bs, backward, out_ref, table_ref):
    k = gate[0]
    w = k.shape[1]
    row_of = lambda pos: SCAN_CHUNK - 1 - pos if backward else pos
    per_block = lambda vals: jnp.concatenate(
        [jnp.broadcast_to(vals[_scan_block(r, backward)], (SCORE_BLOCK, w)) for r in range(N_BLOCKS)], axis=0)
    for j, b in enumerate(bs):
        rs = slice(j * SCAN_CHUNK, (j + 1) * SCAN_CHUNK)
        at = lambda pos: b[row_of(pos):row_of(pos) + 1, :]
        b_last = at(SCAN_CHUNK - 1)
        if q is None:
            out_ref[rs, :] = (k[rs] * jnp.exp(b_last - b)).astype(BF16)
            table_ref[j] = jnp.broadcast_to(jnp.exp(b_last), table_ref.shape[1:])
            continue
        centre = [0.5 * (at(i * SCORE_BLOCK) + at((i + 1) * SCORE_BLOCK - 1)) for i in range(N_BLOCKS)]
        edge = at(SCORE_BLOCK - 1)
        centre_rows = per_block(centre)
        q_c = q[rs] * jnp.exp(b - centre_rows)
        k_c = k[rs] * jnp.exp(centre_rows - b)
        out_ref[rs, 0:w] = q_c.astype(BF16)
        out_ref[rs, w:2 * w] = k_c.astype(BF16)
        out_ref[rs, 2 * w:3 * w] = (q_c * per_block([jnp.exp(c) for c in centre])).astype(BF16)
        out_ref[rs, 3 * w:4 * w] = (k_c * per_block([jnp.exp(b_last - c) for c in centre])).astype(BF16)
        to_edge = [jnp.exp(edge - centre[0]), jnp.exp(centre[1] - edge)]
        rows = [jnp.exp(b_last)] + [to_edge[_scan_block(r, backward)] for r in range(N_BLOCKS)]
        rows.append(jnp.zeros((TABLE_ROWS - len(rows), w), F32))
        table_ref[j] = jnp.concatenate(rows, axis=0)


def _scan_operands(q, z, lb, backward, out_ref, decay_ref):
    gate = _gate_terms(z, lb)
    _emit_scan_operands(q, gate, _chunk_log_decay(gate, backward), backward, out_ref, decay_ref)


def _normed_input(x, gain_ref, sh_ref, sc_ref, row):
    h = x * _rms_scale(x) * gain_ref[...]
    h = h * (1.0 + sc_ref[row:row + 1, :]) + sh_ref[row:row + 1, :]
    return h.astype(BF16)


N_CASTS = 3


def _inproj_raster_kernel(x_ref, sh_ref, sc_ref, gain_ref, wuv_ref, wg_ref, wq_ref, wi_ref, wf_ref, wb_ref,
                          ws_ref, bs_ref, vg_ref, lbl_ref, *rest):
    cast_ins, rest = rest[:N_CASTS], rest[N_CASTS:]
    a_ref, sg_ref, v_ref = rest[:3]
    fwd_refs, bwd_refs, cast_outs = rest[3:5], rest[5:7], rest[7:]
    hb = _normed_input(x_ref[...], gain_ref, sh_ref, sc_ref, 0)
    proj = lambda w_ref: jnp.dot(hb, w_ref[...], preferred_element_type=F32)

    q = proj(wq_ref)
    zf = proj(wf_ref)
    zb = proj(wb_ref)
    lb = _lower_bounds(lbl_ref)
    gate_f = _gate_terms(zf, lb[0:1, :HALF_W])
    u = jnp.dot(hb, wuv_ref[:, :GROUP_W], preferred_element_type=F32)
    b_f = _chunk_log_decay(gate_f, False)
    gate_b = _gate_terms(zb, lb[1:2, :HALF_W])
    v = jnp.dot(hb, wuv_ref[:, GROUP_W:], preferred_element_type=F32)
    _emit_scan_operands(q, gate_f, b_f, False, *fwd_refs)
    b_b = _chunk_log_decay(gate_b, True)
    g = proj(wg_ref)
    _emit_scan_operands(q, gate_b, b_b, True, *bwd_refs)
    for w_ref, wb16_ref in zip(cast_ins, cast_outs):
        wb16_ref[...] = w_ref[...].astype(wb16_ref.dtype)
    u = _gelu_tanh(u)
    v = _gelu_tanh(v)

    tm = u.shape[0]
    for h in range(HEADS):
        cs = slice(h * HEAD_DIM, (h + 1) * HEAD_DIM)
        vh = v[:, cs]
        vn = (vh * _rms_scale(vh) * vg_ref[h:h + 1, :]).astype(BF16)
        w_s = ws_ref[h].astype(BF16)
        for c in range(tm // GMLP_CHUNK):
            rs = slice(c * GMLP_CHUNK, (c + 1) * GMLP_CHUNK)
            mixed = jnp.dot(w_s, vn[rs], preferred_element_type=F32) + bs_ref[h]
            a_ref[rs, cs] = (u[rs, cs] * mixed).astype(a_ref.dtype)

    vi = proj(wi_ref)
    sg_ref[...] = (g * _sigmoid(g)).astype(sg_ref.dtype)
    v_ref[...] = vi.astype(v_ref.dtype)


def _scan_operand_outputs(t, tm, width, index_map, decay_map):
    data = jax.ShapeDtypeStruct((t, N_OPERANDS * width), BF16)
    decay = jax.ShapeDtypeStruct((t // SCAN_CHUNK, TABLE_ROWS,width), F32)
    data_spec = pl.BlockSpec((tm, N_OPERANDS * width), index_map)
    decay_spec = pl.BlockSpec((tm // SCAN_CHUNK, TABLE_ROWS,width), decay_map)
    return [data, decay], [data_spec, decay_spec]


def _inproj_raster(x, mod, gain, w_bf, w_s, b_s_b, v_gain, lb_logits, casts):
    t, d = x.shape
    tm = 256
    n_steps = t // tm
    row = lambda i: (i, 0)
    const2 = lambda i: (0, 0)
    const3 = lambda i: (0, 0, 0)
    full = pl.BlockSpec((tm, GROUP_W), row)
    half = pl.BlockSpec((tm, HALF_W), row)
    resident = lambda width, blk: pl.BlockSpec((d, width), lambda i: (0, blk), pipeline_mode=pl.Buffered(1))
    cast_specs = [pl.BlockSpec((w.shape[0] // n_steps, w.shape[1]), row) for w in casts]
    op_shapes, op_specs = _scan_operand_outputs(t, tm, HALF_W, row, lambda i: (i, 0, 0))
    return pl.pallas_call(
        _inproj_raster_kernel,
        grid=(n_steps,),
        in_specs=[pl.BlockSpec((tm, d), row),
                  pl.BlockSpec((2, d), lambda i: (0, 0)),
                  pl.BlockSpec((2, d), lambda i: (0, 1)),
                  pl.BlockSpec((1, d), const2),
                  resident(2 * GROUP_W, 0), resident(GROUP_W, 6),
                  resident(HALF_W, 4), resident(HALF_W, 6), resident(HALF_W, 8), resident(HALF_W, 10),
                  pl.BlockSpec(w_s.shape, const3),
                  pl.BlockSpec(b_s_b.shape, const3),
                  pl.BlockSpec(v_gain.shape, const2),
                  pl.BlockSpec(lb_logits.shape, const3)] + cast_specs,
        out_specs=[full, full, half] + op_specs + op_specs + cast_specs,
        out_shape=([jax.ShapeDtypeStruct((t, GROUP_W), BF16)] * 2 + [jax.ShapeDtypeStruct((t, HALF_W), BF16)]
                   + op_shapes + op_shapes + [jax.ShapeDtypeStruct(w.shape, BF16) for w in casts]),
        compiler_params=_params(1),
        name="inproj_raster",
    )(x, mod, mod, gain, *([w_bf] * 6), w_s, b_s_b, v_gain, lb_logits, *casts)


COLS_PER_STEP = 4


def _gather_copies(x_hbm_ref, buf_ref, sem_ref, slot, step):
    rows = x_hbm_ref.shape[0]
    return [pltpu.make_async_copy(x_hbm_ref.at[:, step * COLS_PER_STEP + wi, :],
                                  buf_ref.at[slot, pl.ds(wi * rows, rows), :],
                                  sem_ref.at[slot, wi])
            for wi in range(COLS_PER_STEP)]


def _inproj_colmajor_kernel(x_hbm_ref, sh_ref, sc_ref, gain_ref, wq_ref, wi_ref, wf_ref, wb_ref, lbl_ref,
                            v_ref, *rest):
    fwd_refs, bwd_refs, (xbuf_ref, sem_ref) = rest[:2], rest[2:4], rest[4:]
    step, n_steps = pl.program_id(0), pl.num_programs(0)
    slot = step % 2
    copies = functools.partial(_gather_copies, x_hbm_ref, xbuf_ref, sem_ref)

    @pl.when(step == 0)
    def _():
        for cp in copies(slot, step):
            cp.start()

    @pl.when(step + 1 < n_steps)
    def _():
        for cp in copies(1 - slot, step + 1):
            cp.start()

    for cp in copies(slot, step):
        cp.wait()
    hb = _normed_input(xbuf_ref[slot], gain_ref, sh_ref, sc_ref, 0)
    proj = lambda w_ref: jnp.dot(hb, w_ref[...], preferred_element_type=F32)
    zf = proj(wf_ref)
    zb = proj(wb_ref)
    lb = _lower_bounds(lbl_ref)
    gate_f = _gate_terms(zf, lb[0:1, HALF_W:])
    q = proj(wq_ref)
    b_f = _chunk_log_decay(gate_f, False)
    gate_b = _gate_terms(zb, lb[1:2, HALF_W:])
    vi = proj(wi_ref)
    _emit_scan_operands(q, gate_f, b_f, False, *fwd_refs)
    b_b = _chunk_log_decay(gate_b, True)
    _emit_scan_operands(q, gate_b, b_b, True, *bwd_refs)
    v_ref[...] = vi.astype(v_ref.dtype)


def _inproj_colmajor(x, mod, gain, w_bf, lb_logits):
    t, d = x.shape
    rows = t // GRID_W
    tm = COLS_PER_STEP * rows
    row = lambda i: (i, 0)
    const2 = lambda i: (0, 0)
    w_spec = lambda g: pl.BlockSpec((d, HALF_W), lambda i: (0, 2 * g + 1), pipeline_mode=pl.Buffered(1))
    op_shapes, op_specs = _scan_operand_outputs(t, tm, HALF_W, row, lambda i: (i, 0, 0))
    return pl.pallas_call(
        _inproj_colmajor_kernel,
        grid=(GRID_W // COLS_PER_STEP,),
        in_specs=[pl.BlockSpec(memory_space=pl.ANY),
                  pl.BlockSpec((2, d), lambda i: (0, 0)),
                  pl.BlockSpec((2, d), lambda i: (0, 1)),
                  pl.BlockSpec((1, d), const2),
                  w_spec(2), w_spec(3), w_spec(4), w_spec(5),
                  pl.BlockSpec(lb_logits.shape, lambda i: (0, 0, 0))],
        out_specs=[pl.BlockSpec((tm, HALF_W), row)] + op_specs + op_specs,
        out_shape=[jax.ShapeDtypeStruct((t, HALF_W), BF16)] + op_shapes + op_shapes,
        scratch_shapes=[pltpu.VMEM((2, tm, d), F32), pltpu.SemaphoreType.DMA((2, COLS_PER_STEP))],
        compiler_params=_params(1),
        name="inproj_colmajor",
    )(x.reshape(rows, GRID_W, d), mod, mod, gain, w_bf, w_bf, w_bf, w_bf, lb_logits)


def _inproj_ctx_kernel(x_ref, sh_ref, sc_ref, gain_ref, w_ref, lbl_ref,
                       v_ref, kf_ref, df_ref, kb_ref, db_ref):
    hb = _normed_input(x_ref[...], gain_ref, sh_ref, sc_ref, 1)

    def proj(g):
        return jnp.dot(hb, w_ref[:, g * GROUP_W:(g + 1) * GROUP_W], preferred_element_type=F32)

    v_ref[...] = proj(0).astype(v_ref.dtype)
    lb = _lower_bounds(lbl_ref)
    _scan_operands(None, proj(1), lb[0:1, :], False, kf_ref, df_ref)
    _scan_operands(None, proj(2), lb[1:2, :], True, kb_ref, db_ref)


def _inproj_ctx(ctx, mod, gain, w_bf, lb_logits):
    t, d = ctx.shape
    const2 = lambda i: (0, 0)
    const3 = lambda i: (0, 0, 0)
    data = jax.ShapeDtypeStruct((t, GROUP_W), BF16)
    decay = jax.ShapeDtypeStruct((t // SCAN_CHUNK, TABLE_ROWS,GROUP_W), F32)
    data_spec = pl.BlockSpec(data.shape, const2)
    decay_spec = pl.BlockSpec(decay.shape, const3)
    return pl.pallas_call(
        _inproj_ctx_kernel,
        grid=(1,),
        in_specs=[pl.BlockSpec((t, d), const2),
                  pl.BlockSpec((2, d), lambda i: (0, 0)),
                  pl.BlockSpec((2, d), lambda i: (0, 1)),
                  pl.BlockSpec((1, d), const2),
                  pl.BlockSpec((d, 3 * GROUP_W), lambda i: (0, 1)),
                  pl.BlockSpec(lb_logits.shape, const3)],
        out_specs=[data_spec, data_spec, decay_spec, data_spec, decay_spec],
        out_shape=[data, data, decay, data, decay],
        compiler_params=_params(1),
        name="inproj_ctx",
    )(ctx, mod, mod, gain, w_bf, lb_logits)


def _intra_chunk_scores(q_c, k_c, table, backward):
    nt = (((1,), (1,)), ((), ()))
    blk = SCORE_BLOCK
    block_rows = lambda r: slice(r * blk, (r + 1) * blk)
    to_edge = jnp.concatenate([jnp.broadcast_to(table[1 + r:2 + r, :], (blk, HEAD_DIM))
                               for r in range(N_BLOCKS)], axis=0).astype(BF16)
    ti = lax.broadcasted_iota(jnp.int32, (SCAN_CHUNK, SCAN_CHUNK), 0)
    si = lax.broadcasted_iota(jnp.int32, (SCAN_CHUNK, SCAN_CHUNK), 1)
    if backward:
        ti, si = SCAN_CHUNK - 1 - ti, SCAN_CHUNK - 1 - si
    same_block = (ti // blk == si // blk) & (si <= ti)
    earlier_block = si // blk < ti // blk

    s_block = lax.dot_general(q_c, k_c, nt, preferred_element_type=F32)
    late = block_rows(0 if backward else 1)
    s_cross = lax.dot_general(q_c[late] * to_edge[late], k_c * to_edge, nt, preferred_element_type=F32)
    out = []
    for r in range(N_BLOCKS):
        rows = block_rows(r)
        val = jnp.where(same_block[rows], s_block[rows], 0.0)
        if _scan_block(r, backward) == 1:
            val = jnp.where(earlier_block[rows], s_cross, val)
        out.append(val)
    return jnp.concatenate(out, axis=0).astype(BF16)


def _scan_chunk(groups, state_ref, o_refs):
    units = [(g, h, slice(h * HEAD_DIM, (h + 1) * HEAD_DIM)) for g in range(4) for h in range(HALF_HEADS)]
    with_outputs = o_refs is not None

    def operand(g, j, cs):
        return groups[g][0][:, j * HALF_W + cs.start:j * HALF_W + cs.stop]

    scores = {}
    if with_outputs:
        for g, h, cs in units:
            scores[g, h] = _intra_chunk_scores(operand(g, 0, cs), operand(g, 1, cs), groups[g][2][:, cs], g >= 2)
    outs = {}
    for g, h, cs in units:
        ops_ref, v_ref, decay_ref = groups[g]
        u = g * HALF_HEADS + h
        st = state_ref[u]
        vh = v_ref[:, cs]
        if with_outputs:
            outs[g, h] = jnp.dot(jnp.concatenate([scores[g, h], operand(g, 2, cs)], axis=1),
                                 jnp.concatenate([vh, st.astype(BF16)], axis=0), preferred_element_type=F32)
        decay = jnp.broadcast_to(decay_ref[0:1, cs], (HEAD_DIM, HEAD_DIM)).T
        k_st = operand(g, 3, cs) if with_outputs else ops_ref[:, cs]
        state_ref[u] = st * decay + lax.dot_general(
            k_st, vh, (((0,), (0,)), ((), ())), preferred_element_type=F32)
    if with_outputs:
        for g, o_ref in enumerate(o_refs):
            o_ref[...] = jnp.concatenate([outs[g, h] for h in range(HALF_HEADS)], axis=1).astype(o_ref.dtype)


def _scatter_copy(buf_ref, out_hbm_ref, sem_ref, slot, chunk):
    per_col = out_hbm_ref.shape[0] // SCAN_CHUNK
    r0 = (chunk % per_col) * SCAN_CHUNK
    dst = out_hbm_ref.at[pl.ds(r0, SCAN_CHUNK), chunk // per_col, :]
    return pltpu.make_async_copy(buf_ref.at[slot], dst, sem_ref.at[slot])


def _scan_kernel(*refs, emit_outputs):
    groups = [refs[3 * g:3 * g + 3] for g in range(4)]
    s0_ref, rest = refs[12], refs[13:]
    step, n_steps = pl.program_id(0), pl.num_programs(0)
    state_ref = rest[4] if emit_outputs else rest[0]

    @pl.when(step == 0)
    def _():
        state_ref[...] = s0_ref[...]

    if not emit_outputs:
        _scan_chunk(groups, state_ref, None)
        return

    o_refs, (bufs, sems) = rest[:4], (rest[5:7], rest[7:9])
    slot = step % 2
    _scan_chunk(groups, state_ref, [o_refs[0], bufs[0].at[slot], o_refs[2], bufs[1].at[slot]])
    for d, backward in enumerate((False, True)):
        chunk_of = (lambda s: n_steps - 1 - s) if backward else (lambda s: s)
        copy = functools.partial(_scatter_copy, bufs[d], o_refs[2 * d + 1], sems[d])
        copy(slot, chunk_of(step)).start()

        @pl.when(step > 0)
        def _():
            copy(1 - slot, chunk_of(step - 1)).wait()

        @pl.when(step == n_steps - 1)
        def _():
            copy(slot, chunk_of(step)).wait()


def _scan(groups, s0, n_steps, out_maps, grid_rows, name):
    blk = (SCAN_CHUNK, HALF_W)
    args, in_specs = [], []
    for arrays, chunk_of, col in groups:
        args.extend(arrays)
        data_map = lambda s, chunk_of=chunk_of, col=col: (chunk_of(s), col)
        decay_map = lambda s, chunk_of=chunk_of, col=col: (chunk_of(s), 0, col)
        operands, _, _ = arrays
        packed = operands.shape[1] == N_OPERANDS * HALF_W
        in_specs.append(pl.BlockSpec((SCAN_CHUNK, N_OPERANDS * HALF_W), lambda s, chunk_of=chunk_of:
                                     (chunk_of(s), 0)) if packed else pl.BlockSpec(blk, data_map))
        in_specs.append(pl.BlockSpec(blk, data_map))
        in_specs.append(pl.BlockSpec((None, TABLE_ROWS, HALF_W), decay_map))
    args.append(s0)
    state_spec = pl.BlockSpec(s0.shape, lambda s: (0, 0, 0))
    in_specs.append(state_spec)
    out_specs, out_shape, scratch = [], [], []
    if out_maps is not None:
        raster = jax.ShapeDtypeStruct((grid_rows * GRID_W, HALF_W), BF16)
        scattered = jax.ShapeDtypeStruct((grid_rows, GRID_W, HALF_W), F32)
        in_hbm = pl.BlockSpec(memory_space=pl.ANY)
        out_shape = [raster, scattered, raster, scattered]
        out_specs = [pl.BlockSpec(blk, out_maps[0]), in_hbm, pl.BlockSpec(blk, out_maps[1]), in_hbm]
        scratch = [pltpu.VMEM((2,) + blk, F32)] * 2 + [pltpu.SemaphoreType.DMA((2,))] * 2
    out_shape.append(jax.ShapeDtypeStruct(s0.shape, F32))
    out_specs.append(state_spec)
    return pl.pallas_call(
        functools.partial(_scan_kernel, emit_outputs=out_maps is not None),
        grid=(n_steps,),
        in_specs=in_specs,
        out_specs=out_specs,
        out_shape=out_shape,
        scratch_shapes=scratch,
        compiler_params=_params(1),
        name=name,
    )(*args)


def _scan_ctx(v, kf, df, kb, db):
    n = v.shape[0] // SCAN_CHUNK
    up, down = (lambda s: s), (lambda s: n - 1 - s)
    groups = [((kf, v, df), up, 0), ((kf, v, df), up, 1), ((kb, v, db), down, 0), ((kb, v, db), down, 1)]
    s0 = jnp.zeros((2 * HEADS, HEAD_DIM, HEAD_DIM), F32)
    (state,) = _scan(groups, s0, n, None, 0, "scan_ctx")
    return state


def _scan_latent(raster, colmajor, s0):
    t = raster[0].shape[0]
    n = t // SCAN_CHUNK
    up, down = (lambda s: s), (lambda s: n - 1 - s)
    groups = []
    for d, chunk_of in enumerate((up, down)):
        for v, *operands in (raster, colmajor):
            packed, decay = operands[2 * d:2 * d + 2]
            groups.append(((packed, v, decay), chunk_of, 0))
    out_maps = [lambda s: (s, 0), lambda s: (n - 1 - s, 0)]
    o_fr, o_fc, o_br, o_bc, _ = _scan(groups, s0, n, out_maps, t // GRID_W, "scan_latent")
    return o_fr, o_fc.reshape(t, HALF_W), o_br, o_bc.reshape(t, HALF_W)


def _outproj_kernel(a_ref, ofr_ref, ofc_ref, obr_ref, obc_ref, sg_ref, x_ref, w_ref,
                    g1_ref, sh2_ref, sc2_ref, og_ref, n2g_ref, x1_ref, h2_ref):
    def readout(rs):
        o_r = ofr_ref[rs, :].astype(F32) + obr_ref[rs, :].astype(F32)
        o_c = ofc_ref[rs, :] + obc_ref[rs, :]
        ys = []
        for h in range(HEADS):
            o = o_r if h < HALF_HEADS else o_c
            oh = o[:, (h % HALF_HEADS) * HEAD_DIM:(h % HALF_HEADS + 1) * HEAD_DIM]
            cs = slice(h * HEAD_DIM, (h + 1) * HEAD_DIM)
            ys.append((oh * _rms_scale(oh) * og_ref[:, cs] * sg_ref[rs, cs].astype(F32)).astype(BF16))
        return jnp.concatenate(ys, axis=1)

    def finish(rs, proj):
        x1 = x_ref[rs, :] + g1_ref[0:1, :] * proj
        x1_ref[rs, :] = x1
        h2 = x1 * _rms_scale(x1) * n2g_ref[...]
        h2_ref[rs, :] = (h2 * (1.0 + sc2_ref[0:1, :]) + sh2_ref[0:1, :]).astype(h2_ref.dtype)

    half = x_ref.shape[0] // 2
    top, bottom = slice(0, half), slice(half, 2 * half)
    gmlp_side = lambda rs: jnp.dot(a_ref[rs, :], w_ref[:GROUP_W, :], preferred_element_type=F32)
    hgrn_side = lambda y: jnp.dot(y, w_ref[GROUP_W:, :], preferred_element_type=F32)
    p_top = gmlp_side(top)
    y_top = readout(top)
    p_bottom = gmlp_side(bottom)
    y_bottom = readout(bottom)
    p_top = p_top + hgrn_side(y_top)
    p_bottom = p_bottom + hgrn_side(y_bottom)
    finish(top, p_top)
    finish(bottom, p_bottom)


def _outproj(a, o_fr, o_fc, o_br, o_bc, sg, x, w_bf, mod, o_gain, n2_gain):
    t, d = x.shape
    tm = 512
    row = lambda i: (i, 0)
    const2 = lambda i: (0, 0)
    half = pl.BlockSpec((tm, HALF_W), row)
    full = pl.BlockSpec((tm, GROUP_W), row)
    mod_col = lambda c: pl.BlockSpec((2, d), lambda i: (0, c))
    return pl.pallas_call(
        _outproj_kernel,
        grid=(t // tm,),
        in_specs=[full, half, half, half, half, full,
                  pl.BlockSpec((tm, d), row),
                  pl.BlockSpec(w_bf.shape, const2, pipeline_mode=pl.Buffered(1)),
                  mod_col(2), mod_col(3), mod_col(4),
                  pl.BlockSpec((1, GROUP_W), const2),
                  pl.BlockSpec((1, d), const2)],
        out_specs=[pl.BlockSpec((tm, d), row), pl.BlockSpec((tm, d), row)],
        out_shape=[jax.ShapeDtypeStruct((t, d), F32), jax.ShapeDtypeStruct((t, d), BF16)],
        compiler_params=_params(1),
        name="outproj",
    )(a, o_fr, o_fc, o_br, o_bc, sg, x, w_bf, mod, mod, mod, o_gain, n2_gain)


def _ffn_kernel(h_ref, x1_ref, w1_ref, w2_ref, g2_ref, fg_ref, out_ref):
    f = pl.program_id(1)

    @pl.when(f == 0)
    def _():
        out_ref[...] = jnp.zeros_like(out_ref)

    hid = jnp.dot(h_ref[...], w1_ref[...], preferred_element_type=F32)
    hid = jnp.square(jnp.maximum(hid, 0.0)).astype(BF16)
    out_ref[...] += jnp.dot(hid, w2_ref[...], preferred_element_type=F32)

    @pl.when(f == pl.num_programs(1) - 1)
    def _():
        x2 = x1_ref[...] + g2_ref[0:1, :] * out_ref[...]
        out_ref[...] = x2 * _rms_scale(x2) * fg_ref[...]


def _ffn(h2, x1, w1_bf, w2_bf, mod, final_gain):
    t, d = x1.shape
    d_ff = w1_bf.shape[1]
    tm, tf = 1024, 1024
    row = lambda i, f: (i, 0)
    return pl.pallas_call(
        _ffn_kernel,
        grid=(t // tm, d_ff // tf),
        in_specs=[pl.BlockSpec((tm, d), row),
                  pl.BlockSpec((tm, d), row, pipeline_mode=pl.Buffered(1)),
                  pl.BlockSpec((d, tf), lambda i, f: (0, f)),
                  pl.BlockSpec((tf, d), lambda i, f: (f, 0)),
                  pl.BlockSpec((2, d), lambda i, f: (0, 5)),
                  pl.BlockSpec((1, d), lambda i, f: (0, 0))],
        out_specs=pl.BlockSpec((tm, d), row, pipeline_mode=pl.Buffered(1)),
        out_shape=jax.ShapeDtypeStruct((t, d), F32),
        compiler_params=_params(2),
        name="ffn",
    )(h2, x1, w1_bf, w2_bf, mod, final_gain)


def kernel(x, c, ctx, c_ctx, w_ada, b_ada, norm1_gain, w_in, gmlp_w_s, gmlp_b_s, gmlp_v_gain,
           hgrn_lb_logits, hgrn_o_gain, w_out, norm2_gain, w_ff1, w_ff2, final_gain):
    bsz, t, d = x.shape
    assert bsz == 1 and w_ada.shape[0] == 1, "single sample, single layer"
    assert w_in.shape[2] == 7 * GROUP_W and t % (GRID_W * SCAN_CHUNK) == 0

    cond_t = jnp.stack([c[0], c_ctx], axis=1)
    mod = _adaln(cond_t, w_ada[0], b_ada)
    w_in_bf = _cast_bf16(w_in[0], "cast_w_in")

    b_s_b = jnp.broadcast_to(gmlp_b_s[0][:, :, None], (HEADS, GMLP_CHUNK, HEAD_DIM))
    a, sg, *rest = _inproj_raster(x[0], mod, norm1_gain, w_in_bf, gmlp_w_s[0], b_s_b, gmlp_v_gain[0],
                                  hgrn_lb_logits, [w_out[0], w_ff1[0], w_ff2[0]])
    raster, (w_out_bf, w1_bf, w2_bf) = rest[:-N_CASTS], rest[-N_CASTS:]
    colmajor = _inproj_colmajor(x[0], mod, norm1_gain, w_in_bf, hgrn_lb_logits)
    ctx_ops = _inproj_ctx(ctx[0], mod, norm1_gain, w_in_bf, hgrn_lb_logits)

    s0 = _scan_ctx(*ctx_ops)
    o_fr, o_fc, o_br, o_bc = _scan_latent(raster, colmajor, s0)

    x1, h2 = _outproj(a, o_fr, o_fc, o_br, o_bc, sg, x[0], w_out_bf, mod,
                      hgrn_o_gain.reshape(1, GROUP_W), norm2_gain)
    out = _ffn(h2, x1, w1_bf, w2_bf, mod, final_gain.reshape(1, d))
    return out[None]
```

```python
import functools

import jax
import jax.numpy as jnp
from jax import lax
from jax.experimental import pallas as pl
from jax.experimental.pallas import tpu as pltpu

EPS = 1e-6
GRID_W = 64
HEADS = 8
HALF_HEADS = HEADS // 2
HEAD_DIM = 128
GROUP_W = HEADS * HEAD_DIM
HALF_W = HALF_HEADS * HEAD_DIM
GMLP_CHUNK = 128
SCAN_CHUNK = 128
LANES = 128
N_LEVELS = 2
N_BLOCKS = 1 << N_LEVELS
SCORE_BLOCK = SCAN_CHUNK // N_BLOCKS
TABLE_ROWS = 16
VMEM_LIMIT = 56 * 1024 * 1024

F32 = jnp.float32
BF16 = jnp.bfloat16


def _params(n_axes, vmem=VMEM_LIMIT):
    return pltpu.CompilerParams(dimension_semantics=("arbitrary",) * n_axes, vmem_limit_bytes=vmem)


def _sigmoid(x):
    return 1.0 / (1.0 + jnp.exp(-x))


def _gelu_tanh(x):
    return x * (0.5 * (1.0 + jnp.tanh(0.7978845608028654 * (x + 0.044715 * (x * x * x)))))


def _rms_scale(x):
    return lax.rsqrt(jnp.mean(x * x, axis=-1, keepdims=True) + EPS)


def _adaln_kernel(cond_t_ref, w_ref, b_ref, out_ref, s_ref):
    @pl.when(pl.program_id(0) == 0)
    def _():
        ct = cond_t_ref[...]
        s = ct * _sigmoid(ct)
        for r in range(2):
            s_ref[r] = jnp.broadcast_to(s[:, r:r + 1], s_ref.shape[1:])

    d, tn = w_ref.shape
    rows_per_step = 8

    def body(i, acc):
        rows = pl.ds(pl.multiple_of(i * rows_per_step, rows_per_step), rows_per_step)
        w = w_ref[rows, :]
        return tuple(a + w * jnp.tile(s_ref[r, rows, :], (1, tn // LANES)) for r, a in enumerate(acc))

    zeros = jnp.zeros((rows_per_step, tn), F32)
    acc = lax.fori_loop(0, d // rows_per_step, body, (zeros, zeros), unroll=8)
    out_ref[...] = jnp.concatenate([jnp.sum(a, axis=0, keepdims=True) for a in acc], axis=0) + b_ref[...]


def _adaln(cond_t, w_ada, b_ada):
    d, n = w_ada.shape
    tn = 1024
    return pl.pallas_call(
        _adaln_kernel,
        grid=(n // tn,),
        in_specs=[pl.BlockSpec((d, 2), lambda j: (0, 0)),
                  pl.BlockSpec((d, tn), lambda j: (0, j)),
                  pl.BlockSpec((1, tn), lambda j: (0, j))],
        out_specs=pl.BlockSpec((2, tn), lambda j: (0, j)),
        out_shape=jax.ShapeDtypeStruct((2, n), F32),
        scratch_shapes=[pltpu.VMEM((2, d, LANES), F32)],
        compiler_params=_params(1),
        name="adaln",
    )(cond_t, w_ada, b_ada)


def _cast_kernel(w_ref, o_ref):
    o_ref[...] = w_ref[...].astype(o_ref.dtype)


def _cast_bf16(w, name):
    k, n = w.shape
    tk = 256
    return pl.pallas_call(
        _cast_kernel,
        grid=(k // tk,),
        in_specs=[pl.BlockSpec((tk, n), lambda i: (i, 0))],
        out_specs=pl.BlockSpec((tk, n), lambda i: (i, 0)),
        out_shape=jax.ShapeDtypeStruct((k, n), BF16),
        compiler_params=_params(1),
        name=name,
    )(w)


def _lower_bounds(lbl_ref):
    l0, l1 = lbl_ref[0], lbl_ref[1]
    m = jnp.maximum(l0, l1)
    e0, e1 = jnp.exp(l0 - m), jnp.exp(l1 - m)
    return e1 / (e0 + e1)


def _visible(backward):
    ti = lax.broadcasted_iota(jnp.int32, (SCAN_CHUNK, SCAN_CHUNK), 0)
    si = lax.broadcasted_iota(jnp.int32, (SCAN_CHUNK, SCAN_CHUNK), 1)
    return (si >= ti) if backward else (si <= ti)


def _gate_terms(z, lb):
    f = lb + (1.0 - lb) * _sigmoid(z)
    lf = jnp.log(f)
    hi = lf.astype(BF16)
    lo = (lf - hi.astype(F32)).astype(BF16)
    return 1.0 - f, hi, lo


def _chunk_log_decay(gate, backward):
    _, hi, lo = gate
    tri = _visible(backward).astype(BF16)
    c = SCAN_CHUNK
    return [jnp.dot(tri, hi[j * c:(j + 1) * c], preferred_element_type=F32)
            + jnp.dot(tri, lo[j * c:(j + 1) * c], preferred_element_type=F32)
            for j in range(hi.shape[0] // c)]


N_OPERANDS = 4


def _scan_block(r, backward):
    return N_BLOCKS - 1 - r if backward else r


def _is_later(i, level):
    return (i >> (level - 1)) & 1 == 1


def _level_edge(i, level):
    group = i >> level
    return (group * (1 << level) + (1 << (level - 1))) * SCORE_BLOCK - 1


def _emit_scan_operands(q, gate, bs, backward, out_ref, table_ref):
    k = gate[0]
    w = k.shape[1]
    row_of = lambda pos: SCAN_CHUNK - 1 - pos if backward else pos
    per_block = lambda vals: jnp.concatenate(
        [jnp.broadcast_to(vals[_scan_block(r, backward)], (SCORE_BLOCK, w)) for r in range(N_BLOCKS)], axis=0)
    for j, b in enumerate(bs):
        rs = slice(j * SCAN_CHUNK, (j + 1) * SCAN_CHUNK)
        at = lambda pos: b[row_of(pos):row_of(pos) + 1, :]
        b_last = at(SCAN_CHUNK - 1)
        if q is None:
            out_ref[rs, :] = (k[rs] * jnp.exp(b_last - b)).astype(BF16)
            table_ref[j] = jnp.broadcast_to(jnp.exp(b_last), table_ref.shape[1:])
            continue
        centre = [0.5 * (at(i * SCORE_BLOCK) + at((i + 1) * SCORE_BLOCK - 1)) for i in range(N_BLOCKS)]
        centre_rows = per_block(centre)
        q_c = q[rs] * jnp.exp(b - centre_rows)
        k_c = k[rs] * jnp.exp(centre_rows - b)
        out_ref[rs, 0:w] = q_c.astype(BF16)
        out_ref[rs, w:2 * w] = k_c.astype(BF16)
        out_ref[rs, 2 * w:3 * w] = (q_c * per_block([jnp.exp(c) for c in centre])).astype(BF16)
        out_ref[rs, 3 * w:4 * w] = (k_c * per_block([jnp.exp(b_last - c) for c in centre])).astype(BF16)
        rows = [jnp.exp(b_last)]
        for level in range(1, N_LEVELS + 1):
            to_edge = []
            for i in range(N_BLOCKS):
                edge = at(_level_edge(i, level))
                to_edge.append(jnp.exp(centre[i] - edge) if _is_later(i, level) else jnp.exp(edge - centre[i]))
            rows.extend(to_edge[_scan_block(r, backward)] for r in range(N_BLOCKS))
        rows.append(jnp.zeros((TABLE_ROWS - len(rows), w), F32))
        table_ref[j] = jnp.concatenate(rows, axis=0)


def _scan_operands(q, z, lb, backward, out_ref, decay_ref):
    gate = _gate_terms(z, lb)
    _emit_scan_operands(q, gate, _chunk_log_decay(gate, backward), backward, out_ref, decay_ref)


def _normed_input(x, gain_ref, sh_ref, sc_ref, row):
    h = x * _rms_scale(x) * gain_ref[...]
    h = h * (1.0 + sc_ref[row:row + 1, :]) + sh_ref[row:row + 1, :]
    return h.astype(BF16)


N_CASTS = 3


def _inproj_raster_kernel(x_ref, sh_ref, sc_ref, gain_ref, wuv_ref, wg_ref, wq_ref, wi_ref, wf_ref, wb_ref,
                          ws_ref, bs_ref, vg_ref, lbl_ref, *rest):
    cast_ins, rest = rest[:N_CASTS], rest[N_CASTS:]
    a_ref, sg_ref, v_ref = rest[:3]
    fwd_refs, bwd_refs, cast_outs = rest[3:5], rest[5:7], rest[7:]
    hb = _normed_input(x_ref[...], gain_ref, sh_ref, sc_ref, 0)
    proj = lambda w_ref: jnp.dot(hb, w_ref[...], preferred_element_type=F32)

    q = proj(wq_ref)
    zf = proj(wf_ref)
    zb = proj(wb_ref)
    lb = _lower_bounds(lbl_ref)
    gate_f = _gate_terms(zf, lb[0:1, :HALF_W])
    u = jnp.dot(hb, wuv_ref[:, :GROUP_W], preferred_element_type=F32)
    b_f = _chunk_log_decay(gate_f, False)
    gate_b = _gate_terms(zb, lb[1:2, :HALF_W])
    v = jnp.dot(hb, wuv_ref[:, GROUP_W:], preferred_element_type=F32)
    _emit_scan_operands(q, gate_f, b_f, False, *fwd_refs)
    b_b = _chunk_log_decay(gate_b, True)
    g = proj(wg_ref)
    _emit_scan_operands(q, gate_b, b_b, True, *bwd_refs)
    for w_ref, wb16_ref in zip(cast_ins, cast_outs):
        wb16_ref[...] = w_ref[...].astype(wb16_ref.dtype)
    u = _gelu_tanh(u)
    v = _gelu_tanh(v)

    tm = u.shape[0]
    for h in range(HEADS):
        cs = slice(h * HEAD_DIM, (h + 1) * HEAD_DIM)
        vh = v[:, cs]
        vn = (vh * _rms_scale(vh) * vg_ref[h:h + 1, :]).astype(BF16)
        w_s = ws_ref[h].astype(BF16)
        for c in range(tm // GMLP_CHUNK):
            rs = slice(c * GMLP_CHUNK, (c + 1) * GMLP_CHUNK)
            mixed = jnp.dot(w_s, vn[rs], preferred_element_type=F32) + bs_ref[h]
            a_ref[rs, cs] = (u[rs, cs] * mixed).astype(a_ref.dtype)

    vi = proj(wi_ref)
    sg_ref[...] = (g * _sigmoid(g)).astype(sg_ref.dtype)
    v_ref[...] = vi.astype(v_ref.dtype)


def _scan_operand_outputs(t, tm, width, index_map, decay_map):
    data = jax.ShapeDtypeStruct((t, N_OPERANDS * width), BF16)
    decay = jax.ShapeDtypeStruct((t // SCAN_CHUNK, TABLE_ROWS,width), F32)
    data_spec = pl.BlockSpec((tm, N_OPERANDS * width), index_map)
    decay_spec = pl.BlockSpec((tm // SCAN_CHUNK, TABLE_ROWS,width), decay_map)
    return [data, decay], [data_spec, decay_spec]


def _inproj_raster(x, mod, gain, w_bf, w_s, b_s_b, v_gain, lb_logits, casts):
    t, d = x.shape
    tm = 256
    n_steps = t // tm
    row = lambda i: (i, 0)
    const2 = lambda i: (0, 0)
    const3 = lambda i: (0, 0, 0)
    full = pl.BlockSpec((tm, GROUP_W), row)
    half = pl.BlockSpec((tm, HALF_W), row)
    resident = lambda width, blk: pl.BlockSpec((d, width), lambda i: (0, blk), pipeline_mode=pl.Buffered(1))
    cast_specs = [pl.BlockSpec((w.shape[0] // n_steps, w.shape[1]), row) for w in casts]
    op_shapes, op_specs = _scan_operand_outputs(t, tm, HALF_W, row, lambda i: (i, 0, 0))
    return pl.pallas_call(
        _inproj_raster_kernel,
        grid=(n_steps,),
        in_specs=[pl.BlockSpec((tm, d), row),
                  pl.BlockSpec((2, d), lambda i: (0, 0)),
                  pl.BlockSpec((2, d), lambda i: (0, 1)),
                  pl.BlockSpec((1, d), const2),
                  resident(2 * GROUP_W, 0), resident(GROUP_W, 6),
                  resident(HALF_W, 4), resident(HALF_W, 6), resident(HALF_W, 8), resident(HALF_W, 10),
                  pl.BlockSpec(w_s.shape, const3),
                  pl.BlockSpec(b_s_b.shape, const3),
                  pl.BlockSpec(v_gain.shape, const2),
                  pl.BlockSpec(lb_logits.shape, const3)] + cast_specs,
        out_specs=[full, full, half] + op_specs + op_specs + cast_specs,
        out_shape=([jax.ShapeDtypeStruct((t, GROUP_W), BF16)] * 2 + [jax.ShapeDtypeStruct((t, HALF_W), BF16)]
                   + op_shapes + op_shapes + [jax.ShapeDtypeStruct(w.shape, BF16) for w in casts]),
        compiler_params=_params(1),
        name="inproj_raster",
    )(x, mod, mod, gain, *([w_bf] * 6), w_s, b_s_b, v_gain, lb_logits, *casts)


COLS_PER_STEP = 4


def _gather_copies(x_hbm_ref, buf_ref, sem_ref, slot, step):
    rows = x_hbm_ref.shape[0]
    return [pltpu.make_async_copy(x_hbm_ref.at[:, step * COLS_PER_STEP + wi, :],
                                  buf_ref.at[slot, pl.ds(wi * rows, rows), :],
                                  sem_ref.at[slot, wi])
            for wi in range(COLS_PER_STEP)]


def _inproj_colmajor_kernel(x_hbm_ref, sh_ref, sc_ref, gain_ref, wq_ref, wi_ref, wf_ref, wb_ref, lbl_ref,
                            v_ref, *rest):
    fwd_refs, bwd_refs, (xbuf_ref, sem_ref) = rest[:2], rest[2:4], rest[4:]
    step, n_steps = pl.program_id(0), pl.num_programs(0)
    slot = step % 2
    copies = functools.partial(_gather_copies, x_hbm_ref, xbuf_ref, sem_ref)

    @pl.when(step == 0)
    def _():
        for cp in copies(slot, step):
            cp.start()

    @pl.when(step + 1 < n_steps)
    def _():
        for cp in copies(1 - slot, step + 1):
            cp.start()

    for cp in copies(slot, step):
        cp.wait()
    hb = _normed_input(xbuf_ref[slot], gain_ref, sh_ref, sc_ref, 0)
    proj = lambda w_ref: jnp.dot(hb, w_ref[...], preferred_element_type=F32)
    zf = proj(wf_ref)
    zb = proj(wb_ref)
    lb = _lower_bounds(lbl_ref)
    gate_f = _gate_terms(zf, lb[0:1, HALF_W:])
    q = proj(wq_ref)
    b_f = _chunk_log_decay(gate_f, False)
    gate_b = _gate_terms(zb, lb[1:2, HALF_W:])
    vi = proj(wi_ref)
    _emit_scan_operands(q, gate_f, b_f, False, *fwd_refs)
    b_b = _chunk_log_decay(gate_b, True)
    _emit_scan_operands(q, gate_b, b_b, True, *bwd_refs)
    v_ref[...] = vi.astype(v_ref.dtype)


def _inproj_colmajor(x, mod, gain, w_bf, lb_logits):
    t, d = x.shape
    rows = t // GRID_W
    tm = COLS_PER_STEP * rows
    row = lambda i: (i, 0)
    const2 = lambda i: (0, 0)
    w_spec = lambda g: pl.BlockSpec((d, HALF_W), lambda i: (0, 2 * g + 1), pipeline_mode=pl.Buffered(1))
    op_shapes, op_specs = _scan_operand_outputs(t, tm, HALF_W, row, lambda i: (i, 0, 0))
    return pl.pallas_call(
        _inproj_colmajor_kernel,
        grid=(GRID_W // COLS_PER_STEP,),
        in_specs=[pl.BlockSpec(memory_space=pl.ANY),
                  pl.BlockSpec((2, d), lambda i: (0, 0)),
                  pl.BlockSpec((2, d), lambda i: (0, 1)),
                  pl.BlockSpec((1, d), const2),
                  w_spec(2), w_spec(3), w_spec(4), w_spec(5),
                  pl.BlockSpec(lb_logits.shape, lambda i: (0, 0, 0))],
        out_specs=[pl.BlockSpec((tm, HALF_W), row)] + op_specs + op_specs,
        out_shape=[jax.ShapeDtypeStruct((t, HALF_W), BF16)] + op_shapes + op_shapes,
        scratch_shapes=[pltpu.VMEM((2, tm, d), F32), pltpu.SemaphoreType.DMA((2, COLS_PER_STEP))],
        compiler_params=_params(1),
        name="inproj_colmajor",
    )(x.reshape(rows, GRID_W, d), mod, mod, gain, w_bf, w_bf, w_bf, w_bf, lb_logits)


def _inproj_ctx_kernel(x_ref, sh_ref, sc_ref, gain_ref, w_ref, lbl_ref,
                       v_ref, kf_ref, df_ref, kb_ref, db_ref):
    hb = _normed_input(x_ref[...], gain_ref, sh_ref, sc_ref, 1)

    def proj(g):
        return jnp.dot(hb, w_ref[:, g * GROUP_W:(g + 1) * GROUP_W], preferred_element_type=F32)

    v_ref[...] = proj(0).astype(v_ref.dtype)
    lb = _lower_bounds(lbl_ref)
    _scan_operands(None, proj(1), lb[0:1, :], False, kf_ref, df_ref)
    _scan_operands(None, proj(2), lb[1:2, :], True, kb_ref, db_ref)


def _inproj_ctx(ctx, mod, gain, w_bf, lb_logits):
    t, d = ctx.shape
    const2 = lambda i: (0, 0)
    const3 = lambda i: (0, 0, 0)
    data = jax.ShapeDtypeStruct((t, GROUP_W), BF16)
    decay = jax.ShapeDtypeStruct((t // SCAN_CHUNK, TABLE_ROWS,GROUP_W), F32)
    data_spec = pl.BlockSpec(data.shape, const2)
    decay_spec = pl.BlockSpec(decay.shape, const3)
    return pl.pallas_call(
        _inproj_ctx_kernel,
        grid=(1,),
        in_specs=[pl.BlockSpec((t, d), const2),
                  pl.BlockSpec((2, d), lambda i: (0, 0)),
                  pl.BlockSpec((2, d), lambda i: (0, 1)),
                  pl.BlockSpec((1, d), const2),
                  pl.BlockSpec((d, 3 * GROUP_W), lambda i: (0, 1)),
                  pl.BlockSpec(lb_logits.shape, const3)],
        out_specs=[data_spec, data_spec, decay_spec, data_spec, decay_spec],
        out_shape=[data, data, decay, data, decay],
        compiler_params=_params(1),
        name="inproj_ctx",
    )(ctx, mod, mod, gain, w_bf, lb_logits)


def _intra_chunk_scores(q_c, k_c, table, backward):
    nt = (((1,), (1,)), ((), ()))
    blk = SCORE_BLOCK
    ti = lax.broadcasted_iota(jnp.int32, (SCAN_CHUNK, SCAN_CHUNK), 0)
    si = lax.broadcasted_iota(jnp.int32, (SCAN_CHUNK, SCAN_CHUNK), 1)
    if backward:
        ti, si = SCAN_CHUNK - 1 - ti, SCAN_CHUNK - 1 - si
    bt, bs = ti // blk, si // blk
    scores = jnp.where((bt == bs) & (si <= ti), lax.dot_general(q_c, k_c, nt, preferred_element_type=F32), 0.0)
    for level in range(1, N_LEVELS + 1):
        first = 1 + (level - 1) * N_BLOCKS
        to_edge = jnp.concatenate([jnp.broadcast_to(table[first + r:first + r + 1, :], (blk, HEAD_DIM))
                                   for r in range(N_BLOCKS)], axis=0).astype(BF16)
        sees = ((bs >> level == bt >> level) & ((bt >> (level - 1)) & 1 == 1) & ((bs >> (level - 1)) & 1 == 0))
        s_level = lax.dot_general(q_c * to_edge, k_c * to_edge, nt, preferred_element_type=F32)
        scores = jnp.where(sees, s_level, scores)
    return scores.astype(BF16)


def _scan_chunk(groups, state_ref, o_refs):
    units = [(g, h, slice(h * HEAD_DIM, (h + 1) * HEAD_DIM)) for g in range(4) for h in range(HALF_HEADS)]
    with_outputs = o_refs is not None

    def operand(g, j, cs):
        return groups[g][0][:, j * HALF_W + cs.start:j * HALF_W + cs.stop]

    scores = {}
    if with_outputs:
        for g, h, cs in units:
            scores[g, h] = _intra_chunk_scores(operand(g, 0, cs), operand(g, 1, cs), groups[g][2][:, cs], g >= 2)
    outs = {}
    for g, h, cs in units:
        ops_ref, v_ref, decay_ref = groups[g]
        u = g * HALF_HEADS + h
        st = state_ref[u]
        vh = v_ref[:, cs]
        if with_outputs:
            outs[g, h] = jnp.dot(jnp.concatenate([scores[g, h], operand(g, 2, cs)], axis=1),
                                 jnp.concatenate([vh, st.astype(BF16)], axis=0), preferred_element_type=F32)
        decay = jnp.broadcast_to(decay_ref[0:1, cs], (HEAD_DIM, HEAD_DIM)).T
        k_st = operand(g, 3, cs) if with_outputs else ops_ref[:, cs]
        state_ref[u] = st * decay + lax.dot_general(
            k_st, vh, (((0,), (0,)), ((), ())), preferred_element_type=F32)
    if with_outputs:
        for g, o_ref in enumerate(o_refs):
            o_ref[...] = jnp.concatenate([outs[g, h] for h in range(HALF_HEADS)], axis=1).astype(o_ref.dtype)


def _scatter_copy(buf_ref, out_hbm_ref, sem_ref, slot, chunk):
    per_col = out_hbm_ref.shape[0] // SCAN_CHUNK
    r0 = (chunk % per_col) * SCAN_CHUNK
    dst = out_hbm_ref.at[pl.ds(r0, SCAN_CHUNK), chunk // per_col, :]
    return pltpu.make_async_copy(buf_ref.at[slot], dst, sem_ref.at[slot])


def _scan_kernel(*refs, emit_outputs):
    groups = [refs[3 * g:3 * g + 3] for g in range(4)]
    s0_ref, rest = refs[12], refs[13:]
    step, n_steps = pl.program_id(0), pl.num_programs(0)
    state_ref = rest[4] if emit_outputs else rest[0]

    @pl.when(step == 0)
    def _():
        state_ref[...] = s0_ref[...]

    if not emit_outputs:
        _scan_chunk(groups, state_ref, None)
        return

    o_refs, (bufs, sems) = rest[:4], (rest[5:7], rest[7:9])
    slot = step % 2
    _scan_chunk(groups, state_ref, [o_refs[0], bufs[0].at[slot], o_refs[2], bufs[1].at[slot]])
    for d, backward in enumerate((False, True)):
        chunk_of = (lambda s: n_steps - 1 - s) if backward else (lambda s: s)
        copy = functools.partial(_scatter_copy, bufs[d], o_refs[2 * d + 1], sems[d])
        copy(slot, chunk_of(step)).start()

        @pl.when(step > 0)
        def _():
            copy(1 - slot, chunk_of(step - 1)).wait()

        @pl.when(step == n_steps - 1)
        def _():
            copy(slot, chunk_of(step)).wait()


def _scan(groups, s0, n_steps, out_maps, grid_rows, name):
    blk = (SCAN_CHUNK, HALF_W)
    args, in_specs = [], []
    for arrays, chunk_of, col in groups:
        args.extend(arrays)
        data_map = lambda s, chunk_of=chunk_of, col=col: (chunk_of(s), col)
        decay_map = lambda s, chunk_of=chunk_of, col=col: (chunk_of(s), 0, col)
        operands, _, _ = arrays
        packed = operands.shape[1] == N_OPERANDS * HALF_W
        in_specs.append(pl.BlockSpec((SCAN_CHUNK, N_OPERANDS * HALF_W), lambda s, chunk_of=chunk_of:
                                     (chunk_of(s), 0)) if packed else pl.BlockSpec(blk, data_map))
        in_specs.append(pl.BlockSpec(blk, data_map))
        in_specs.append(pl.BlockSpec((None, TABLE_ROWS, HALF_W), decay_map))
    args.append(s0)
    state_spec = pl.BlockSpec(s0.shape, lambda s: (0, 0, 0))
    in_specs.append(state_spec)
    out_specs, out_shape, scratch = [], [], []
    if out_maps is not None:
        raster = jax.ShapeDtypeStruct((grid_rows * GRID_W, HALF_W), BF16)
        scattered = jax.ShapeDtypeStruct((grid_rows, GRID_W, HALF_W), F32)
        in_hbm = pl.BlockSpec(memory_space=pl.ANY)
        out_shape = [raster, scattered, raster, scattered]
        out_specs = [pl.BlockSpec(blk, out_maps[0]), in_hbm, pl.BlockSpec(blk, out_maps[1]), in_hbm]
        scratch = [pltpu.VMEM((2,) + blk, F32)] * 2 + [pltpu.SemaphoreType.DMA((2,))] * 2
    out_shape.append(jax.ShapeDtypeStruct(s0.shape, F32))
    out_specs.append(state_spec)
    return pl.pallas_call(
        functools.partial(_scan_kernel, emit_outputs=out_maps is not None),
        grid=(n_steps,),
        in_specs=in_specs,
        out_specs=out_specs,
        out_shape=out_shape,
        scratch_shapes=scratch,
        compiler_params=_params(1),
        name=name,
    )(*args)


def _scan_ctx(v, kf, df, kb, db):
    n = v.shape[0] // SCAN_CHUNK
    up, down = (lambda s: s), (lambda s: n - 1 - s)
    groups = [((kf, v, df), up, 0), ((kf, v, df), up, 1), ((kb, v, db), down, 0), ((kb, v, db), down, 1)]
    s0 = jnp.zeros((2 * HEADS, HEAD_DIM, HEAD_DIM), F32)
    (state,) = _scan(groups, s0, n, None, 0, "scan_ctx")
    return state


def _scan_latent(raster, colmajor, s0):
    t = raster[0].shape[0]
    n = t // SCAN_CHUNK
    up, down = (lambda s: s), (lambda s: n - 1 - s)
    groups = []
    for d, chunk_of in enumerate((up, down)):
        for v, *operands in (raster, colmajor):
            packed, decay = operands[2 * d:2 * d + 2]
            groups.append(((packed, v, decay), chunk_of, 0))
    out_maps = [lambda s: (s, 0), lambda s: (n - 1 - s, 0)]
    o_fr, o_fc, o_br, o_bc, _ = _scan(groups, s0, n, out_maps, t // GRID_W, "scan_latent")
    return o_fr, o_fc.reshape(t, HALF_W), o_br, o_bc.reshape(t, HALF_W)


def _outproj_kernel(a_ref, ofr_ref, ofc_ref, obr_ref, obc_ref, sg_ref, x_ref, w_ref,
                    g1_ref, sh2_ref, sc2_ref, og_ref, n2g_ref, x1_ref, h2_ref):
    def readout(rs):
        o_r = ofr_ref[rs, :].astype(F32) + obr_ref[rs, :].astype(F32)
        o_c = ofc_ref[rs, :] + obc_ref[rs, :]
        ys = []
        for h in range(HEADS):
            o = o_r if h < HALF_HEADS else o_c
            oh = o[:, (h % HALF_HEADS) * HEAD_DIM:(h % HALF_HEADS + 1) * HEAD_DIM]
            cs = slice(h * HEAD_DIM, (h + 1) * HEAD_DIM)
            ys.append((oh * _rms_scale(oh) * og_ref[:, cs] * sg_ref[rs, cs].astype(F32)).astype(BF16))
        return jnp.concatenate(ys, axis=1)

    def finish(rs, proj):
        x1 = x_ref[rs, :] + g1_ref[0:1, :] * proj
        x1_ref[rs, :] = x1
        h2 = x1 * _rms_scale(x1) * n2g_ref[...]
        h2_ref[rs, :] = (h2 * (1.0 + sc2_ref[0:1, :]) + sh2_ref[0:1, :]).astype(h2_ref.dtype)

    half = x_ref.shape[0] // 2
    top, bottom = slice(0, half), slice(half, 2 * half)
    gmlp_side = lambda rs: jnp.dot(a_ref[rs, :], w_ref[:GROUP_W, :], preferred_element_type=F32)
    hgrn_side = lambda y: jnp.dot(y, w_ref[GROUP_W:, :], preferred_element_type=F32)
    p_top = gmlp_side(top)
    y_top = readout(top)
    p_bottom = gmlp_side(bottom)
    y_bottom = readout(bottom)
    p_top = p_top + hgrn_side(y_top)
    p_bottom = p_bottom + hgrn_side(y_bottom)
    finish(top, p_top)
    finish(bottom, p_bottom)


def _outproj(a, o_fr, o_fc, o_br, o_bc, sg, x, w_bf, mod, o_gain, n2_gain):
    t, d = x.shape
    tm = 512
    row = lambda i: (i, 0)
    const2 = lambda i: (0, 0)
    half = pl.BlockSpec((tm, HALF_W), row)
    full = pl.BlockSpec((tm, GROUP_W), row)
    mod_col = lambda c: pl.BlockSpec((2, d), lambda i: (0, c))
    return pl.pallas_call(
        _outproj_kernel,
        grid=(t // tm,),
        in_specs=[full, half, half, half, half, full,
                  pl.BlockSpec((tm, d), row),
                  pl.BlockSpec(w_bf.shape, const2, pipeline_mode=pl.Buffered(1)),
                  mod_col(2), mod_col(3), mod_col(4),
                  pl.BlockSpec((1, GROUP_W), const2),
                  pl.BlockSpec((1, d), const2)],
        out_specs=[pl.BlockSpec((tm, d), row), pl.BlockSpec((tm, d), row)],
        out_shape=[jax.ShapeDtypeStruct((t, d), F32), jax.ShapeDtypeStruct((t, d), BF16)],
        compiler_params=_params(1),
        name="outproj",
    )(a, o_fr, o_fc, o_br, o_bc, sg, x, w_bf, mod, mod, mod, o_gain, n2_gain)


def _ffn_kernel(h_ref, x1_ref, w1_ref, w2_ref, g2_ref, fg_ref, out_ref):
    f = pl.program_id(1)

    @pl.when(f == 0)
    def _():
        out_ref[...] = jnp.zeros_like(out_ref)

    hid = jnp.dot(h_ref[...], w1_ref[...], preferred_element_type=F32)
    hid = jnp.square(jnp.maximum(hid, 0.0)).astype(BF16)
    out_ref[...] += jnp.dot(hid, w2_ref[...], preferred_element_type=F32)

    @pl.when(f == pl.num_programs(1) - 1)
    def _():
        x2 = x1_ref[...] + g2_ref[0:1, :] * out_ref[...]
        out_ref[...] = x2 * _rms_scale(x2) * fg_ref[...]


def _ffn(h2, x1, w1_bf, w2_bf, mod, final_gain):
    t, d = x1.shape
    d_ff = w1_bf.shape[1]
    tm, tf = 1024, 1024
    row = lambda i, f: (i, 0)
    return pl.pallas_call(
        _ffn_kernel,
        grid=(t // tm, d_ff // tf),
        in_specs=[pl.BlockSpec((tm, d), row),
                  pl.BlockSpec((tm, d), row, pipeline_mode=pl.Buffered(1)),
                  pl.BlockSpec((d, tf), lambda i, f: (0, f)),
                  pl.BlockSpec((tf, d), lambda i, f: (f, 0)),
                  pl.BlockSpec((2, d), lambda i, f: (0, 5)),
                  pl.BlockSpec((1, d), lambda i, f: (0, 0))],
        out_specs=pl.BlockSpec((tm, d), row, pipeline_mode=pl.Buffered(1)),
        out_shape=jax.ShapeDtypeStruct((t, d), F32),
        compiler_params=_params(2),
        name="ffn",
    )(h2, x1, w1_bf, w2_bf, mod, final_gain)


def kernel(x, c, ctx, c_ctx, w_ada, b_ada, norm1_gain, w_in, gmlp_w_s, gmlp_b_s, gmlp_v_gain,
           hgrn_lb_logits, hgrn_o_gain, w_out, norm2_gain, w_ff1, w_ff2, final_gain):
    bsz, t, d = x.shape
    assert bsz == 1 and w_ada.shape[0] == 1, "single sample, single layer"
    assert w_in.shape[2] == 7 * GROUP_W and t % (GRID_W * SCAN_CHUNK) == 0

    cond_t = jnp.stack([c[0], c_ctx], axis=1)
    mod = _adaln(cond_t, w_ada[0], b_ada)
    w_in_bf = _cast_bf16(w_in[0], "cast_w_in")

    b_s_b = jnp.broadcast_to(gmlp_b_s[0][:, :, None], (HEADS, GMLP_CHUNK, HEAD_DIM))
    a, sg, *rest = _inproj_raster(x[0], mod, norm1_gain, w_in_bf, gmlp_w_s[0], b_s_b, gmlp_v_gain[0],
                                  hgrn_lb_logits, [w_out[0], w_ff1[0], w_ff2[0]])
    raster, (w_out_bf, w1_bf, w2_bf) = rest[:-N_CASTS], rest[-N_CASTS:]
    colmajor = _inproj_colmajor(x[0], mod, norm1_gain, w_in_bf, hgrn_lb_logits)
    ctx_ops = _inproj_ctx(ctx[0], mod, norm1_gain, w_in_bf, hgrn_lb_logits)

    s0 = _scan_ctx(*ctx_ops)
    o_fr, o_fc, o_br, o_bc = _scan_latent(raster, colmajor, s0)

    x1, h2 = _outproj(a, o_fr, o_fc, o_br, o_bc, sg, x[0], w_out_bf, mod,
                      hgrn_o_gain.reshape(1, GROUP_W), norm2_gain)
    out = _ffn(h2, x1, w1_bf, w2_bf, mod, final_gain.reshape(1, d))
    return out[None]
```

```python
import functools

import jax
import jax.numpy as jnp
from jax import lax
from jax.experimental import pallas as pl
from jax.experimental.pallas import tpu as pltpu

EPS = 1e-6
GRID_W = 64
HEADS = 8
HALF_HEADS = HEADS // 2
HEAD_DIM = 128
GROUP_W = HEADS * HEAD_DIM
HALF_W = HALF_HEADS * HEAD_DIM
GMLP_CHUNK = 128
SCAN_CHUNK = 128
LANES = 128
N_LEVELS = 2
N_BLOCKS = 1 << N_LEVELS
SCORE_BLOCK = SCAN_CHUNK // N_BLOCKS
TABLE_ROWS = 16
VMEM_LIMIT = 56 * 1024 * 1024

F32 = jnp.float32
BF16 = jnp.bfloat16


def _params(n_axes, vmem=VMEM_LIMIT):
    return pltpu.CompilerParams(dimension_semantics=("arbitrary",) * n_axes, vmem_limit_bytes=vmem)


def _sigmoid(x):
    return 1.0 / (1.0 + jnp.exp(-x))


def _gelu_tanh(x):
    return x * (0.5 * (1.0 + jnp.tanh(0.7978845608028654 * (x + 0.044715 * (x * x * x)))))


def _rms_scale(x):
    return lax.rsqrt(jnp.mean(x * x, axis=-1, keepdims=True) + EPS)


def _adaln_kernel(cond_t_ref, w_ref, b_ref, w_in_ref, out_ref, w_in_bf_ref, s_ref):
    w_in_bf_ref[...] = w_in_ref[...].astype(w_in_bf_ref.dtype)

    @pl.when(pl.program_id(0) == 0)
    def _():
        ct = cond_t_ref[...]
        s = ct * _sigmoid(ct)
        for r in range(2):
            s_ref[r] = jnp.broadcast_to(s[:, r:r + 1], s_ref.shape[1:])

    d, tn = w_ref.shape
    rows_per_step = 8

    def body(i, acc):
        rows = pl.ds(pl.multiple_of(i * rows_per_step, rows_per_step), rows_per_step)
        w = w_ref[rows, :]
        return tuple(a + w * jnp.tile(s_ref[r, rows, :], (1, tn // LANES)) for r, a in enumerate(acc))

    zeros = jnp.zeros((rows_per_step, tn), F32)
    acc = lax.fori_loop(0, d // rows_per_step, body, (zeros, zeros), unroll=8)
    out_ref[...] = jnp.concatenate([jnp.sum(a, axis=0, keepdims=True) for a in acc], axis=0) + b_ref[...]


def _adaln(cond_t, w_ada, b_ada, w_in):
    d, n = w_ada.shape
    n_steps = 16
    tn, slab = n // n_steps, w_in.shape[0] // n_steps
    return pl.pallas_call(
        _adaln_kernel,
        grid=(n_steps,),
        in_specs=[pl.BlockSpec((d, 2), lambda j: (0, 0)),
                  pl.BlockSpec((d, tn), lambda j: (0, j)),
                  pl.BlockSpec((1, tn), lambda j: (0, j)),
                  pl.BlockSpec((slab, w_in.shape[1]), lambda j: (j, 0))],
        out_specs=[pl.BlockSpec((2, tn), lambda j: (0, j)),
                   pl.BlockSpec((slab, w_in.shape[1]), lambda j: (j, 0))],
        out_shape=[jax.ShapeDtypeStruct((2, n), F32), jax.ShapeDtypeStruct(w_in.shape, BF16)],
        scratch_shapes=[pltpu.VMEM((2, d, LANES), F32)],
        compiler_params=_params(1),
        name="adaln",
    )(cond_t, w_ada, b_ada, w_in)


def _lower_bounds(lbl_ref):
    l0, l1 = lbl_ref[0], lbl_ref[1]
    m = jnp.maximum(l0, l1)
    e0, e1 = jnp.exp(l0 - m), jnp.exp(l1 - m)
    return e1 / (e0 + e1)


def _visible(backward):
    ti = lax.broadcasted_iota(jnp.int32, (SCAN_CHUNK, SCAN_CHUNK), 0)
    si = lax.broadcasted_iota(jnp.int32, (SCAN_CHUNK, SCAN_CHUNK), 1)
    return (si >= ti) if backward else (si <= ti)


def _gate_terms(z, lb):
    f = lb + (1.0 - lb) * _sigmoid(z)
    lf = jnp.log(f)
    hi = lf.astype(BF16)
    lo = (lf - hi.astype(F32)).astype(BF16)
    return 1.0 - f, hi, lo


def _chunk_log_decay(gate, backward):
    _, hi, lo = gate
    tri = _visible(backward).astype(BF16)
    c = SCAN_CHUNK
    return [jnp.dot(tri, hi[j * c:(j + 1) * c], preferred_element_type=F32)
            + jnp.dot(tri, lo[j * c:(j + 1) * c], preferred_element_type=F32)
            for j in range(hi.shape[0] // c)]


N_OPERANDS = 4


def _scan_block(r, backward):
    return N_BLOCKS - 1 - r if backward else r


def _is_later(i, level):
    return (i >> (level - 1)) & 1 == 1


def _level_edge(i, level):
    group = i >> level
    return (group * (1 << level) + (1 << (level - 1))) * SCORE_BLOCK - 1


def _emit_scan_operands(q, gate, bs, backward, out_ref, table_ref):
    k = gate[0]
    w = k.shape[1]
    row_of = lambda pos: SCAN_CHUNK - 1 - pos if backward else pos
    per_block = lambda vals: jnp.concatenate(
        [jnp.broadcast_to(vals[_scan_block(r, backward)], (SCORE_BLOCK, w)) for r in range(N_BLOCKS)], axis=0)
    for j, b in enumerate(bs):
        rs = slice(j * SCAN_CHUNK, (j + 1) * SCAN_CHUNK)
        at = lambda pos: b[row_of(pos):row_of(pos) + 1, :]
        b_last = at(SCAN_CHUNK - 1)
        if q is None:
            out_ref[rs, :] = (k[rs] * jnp.exp(b_last - b)).astype(BF16)
            table_ref[j] = jnp.broadcast_to(jnp.exp(b_last), table_ref.shape[1:])
            continue
        centre = [0.5 * (at(i * SCORE_BLOCK) + at((i + 1) * SCORE_BLOCK - 1)) for i in range(N_BLOCKS)]
        centre_rows = per_block(centre)
        q_c = q[rs] * jnp.exp(b - centre_rows)
        k_c = k[rs] * jnp.exp(centre_rows - b)
        out_ref[rs, 0:w] = q_c.astype(BF16)
        out_ref[rs, w:2 * w] = k_c.astype(BF16)
        out_ref[rs, 2 * w:3 * w] = (q_c * per_block([jnp.exp(c) for c in centre])).astype(BF16)
        out_ref[rs, 3 * w:4 * w] = (k_c * per_block([jnp.exp(b_last - c) for c in centre])).astype(BF16)
        rows = [jnp.exp(b_last)]
        for level in range(1, N_LEVELS + 1):
            to_edge = []
            for i in range(N_BLOCKS):
                edge = at(_level_edge(i, level))
                to_edge.append(jnp.exp(centre[i] - edge) if _is_later(i, level) else jnp.exp(edge - centre[i]))
            rows.extend(to_edge[_scan_block(r, backward)] for r in range(N_BLOCKS))
        rows.append(jnp.zeros((TABLE_ROWS - len(rows), w), F32))
        table_ref[j] = jnp.concatenate(rows, axis=0)


def _scan_operands(q, z, lb, backward, out_ref, decay_ref):
    gate = _gate_terms(z, lb)
    _emit_scan_operands(q, gate, _chunk_log_decay(gate, backward), backward, out_ref, decay_ref)


def _normed_input(x, gain_ref, sh_ref, sc_ref, row):
    h = x * _rms_scale(x) * gain_ref[...]
    h = h * (1.0 + sc_ref[row:row + 1, :]) + sh_ref[row:row + 1, :]
    return h.astype(BF16)


N_CASTS = 3


def _inproj_raster_kernel(x_ref, sh_ref, sc_ref, gain_ref, wuv_ref, wg_ref, wq_ref, wi_ref, wf_ref, wb_ref,
                          ws_ref, bs_ref, vg_ref, lbl_ref, *rest):
    cast_ins, rest = rest[:N_CASTS], rest[N_CASTS:]
    a_ref, sg_ref, v_ref = rest[:3]
    fwd_refs, bwd_refs, cast_outs = rest[3:5], rest[5:7], rest[7:]
    hb = _normed_input(x_ref[...], gain_ref, sh_ref, sc_ref, 0)
    proj = lambda w_ref: jnp.dot(hb, w_ref[...], preferred_element_type=F32)

    q = proj(wq_ref)
    zf = proj(wf_ref)
    zb = proj(wb_ref)
    lb = _lower_bounds(lbl_ref)
    gate_f = _gate_terms(zf, lb[0:1, :HALF_W])
    u = jnp.dot(hb, wuv_ref[:, :GROUP_W], preferred_element_type=F32)
    b_f = _chunk_log_decay(gate_f, False)
    gate_b = _gate_terms(zb, lb[1:2, :HALF_W])
    v = jnp.dot(hb, wuv_ref[:, GROUP_W:], preferred_element_type=F32)
    _emit_scan_operands(q, gate_f, b_f, False, *fwd_refs)
    b_b = _chunk_log_decay(gate_b, True)
    g = proj(wg_ref)
    _emit_scan_operands(q, gate_b, b_b, True, *bwd_refs)
    for w_ref, wb16_ref in zip(cast_ins, cast_outs):
        wb16_ref[...] = w_ref[...].astype(wb16_ref.dtype)
    u = _gelu_tanh(u)
    v = _gelu_tanh(v)

    chunks = [slice(c * GMLP_CHUNK, (c + 1) * GMLP_CHUNK) for c in range(u.shape[0] // GMLP_CHUNK)]
    for h in range(HEADS):
        cs = slice(h * HEAD_DIM, (h + 1) * HEAD_DIM)
        vh = v[:, cs]
        vn = (vh * _rms_scale(vh) * vg_ref[h:h + 1, :]).astype(BF16)
        mixed = jnp.dot(ws_ref[h].astype(BF16), jnp.concatenate([vn[rs] for rs in chunks], axis=1),
                        preferred_element_type=F32)
        for c, rs in enumerate(chunks):
            a_ref[rs, cs] = (u[rs, cs] * (mixed[:, c * HEAD_DIM:(c + 1) * HEAD_DIM] + bs_ref[h])).astype(a_ref.dtype)

    vi = proj(wi_ref)
    sg_ref[...] = (g * _sigmoid(g)).astype(sg_ref.dtype)
    v_ref[...] = vi.astype(v_ref.dtype)


def _scan_operand_outputs(t, tm, width, index_map, decay_map):
    data = jax.ShapeDtypeStruct((t, N_OPERANDS * width), BF16)
    decay = jax.ShapeDtypeStruct((t // SCAN_CHUNK, TABLE_ROWS,width), F32)
    data_spec = pl.BlockSpec((tm, N_OPERANDS * width), index_map)
    decay_spec = pl.BlockSpec((tm // SCAN_CHUNK, TABLE_ROWS,width), decay_map)
    return [data, decay], [data_spec, decay_spec]


def _inproj_raster(x, mod, gain, w_bf, w_s, b_s_b, v_gain, lb_logits, casts):
    t, d = x.shape
    tm = 256
    n_steps = t // tm
    row = lambda i: (i, 0)
    const2 = lambda i: (0, 0)
    const3 = lambda i: (0, 0, 0)
    full = pl.BlockSpec((tm, GROUP_W), row)
    half = pl.BlockSpec((tm, HALF_W), row)
    resident = lambda width, blk: pl.BlockSpec((d, width), lambda i: (0, blk), pipeline_mode=pl.Buffered(1))
    cast_specs = [pl.BlockSpec((w.shape[0] // n_steps, w.shape[1]), row) for w in casts]
    op_shapes, op_specs = _scan_operand_outputs(t, tm, HALF_W, row, lambda i: (i, 0, 0))
    return pl.pallas_call(
        _inproj_raster_kernel,
        grid=(n_steps,),
        in_specs=[pl.BlockSpec((tm, d), row),
                  pl.BlockSpec((2, d), lambda i: (0, 0)),
                  pl.BlockSpec((2, d), lambda i: (0, 1)),
                  pl.BlockSpec((1, d), const2),
                  resident(2 * GROUP_W, 0), resident(GROUP_W, 6),
                  resident(HALF_W, 4), resident(HALF_W, 6), resident(HALF_W, 8), resident(HALF_W, 10),
                  pl.BlockSpec(w_s.shape, const3),
                  pl.BlockSpec(b_s_b.shape, const3),
                  pl.BlockSpec(v_gain.shape, const2),
                  pl.BlockSpec(lb_logits.shape, const3)] + cast_specs,
        out_specs=[full, full, half] + op_specs + op_specs + cast_specs,
        out_shape=([jax.ShapeDtypeStruct((t, GROUP_W), BF16)] * 2 + [jax.ShapeDtypeStruct((t, HALF_W), BF16)]
                   + op_shapes + op_shapes + [jax.ShapeDtypeStruct(w.shape, BF16) for w in casts]),
        compiler_params=_params(1),
        name="inproj_raster",
    )(x, mod, mod, gain, *([w_bf] * 6), w_s, b_s_b, v_gain, lb_logits, *casts)


COLS_PER_STEP = 4


def _gather_copies(x_hbm_ref, buf_ref, sem_ref, slot, step):
    rows = x_hbm_ref.shape[0]
    return [pltpu.make_async_copy(x_hbm_ref.at[:, step * COLS_PER_STEP + wi, :],
                                  buf_ref.at[slot, pl.ds(wi * rows, rows), :],
                                  sem_ref.at[slot, wi])
            for wi in range(COLS_PER_STEP)]


def _inproj_colmajor_kernel(x_hbm_ref, sh_ref, sc_ref, gain_ref, wq_ref, wi_ref, wf_ref, wb_ref, lbl_ref,
                            v_ref, *rest):
    fwd_refs, bwd_refs, (xbuf_ref, sem_ref) = rest[:2], rest[2:4], rest[4:]
    step, n_steps = pl.program_id(0), pl.num_programs(0)
    slot = step % 2
    copies = functools.partial(_gather_copies, x_hbm_ref, xbuf_ref, sem_ref)

    @pl.when(step == 0)
    def _():
        for cp in copies(slot, step):
            cp.start()

    @pl.when(step + 1 < n_steps)
    def _():
        for cp in copies(1 - slot, step + 1):
            cp.start()

    for cp in copies(slot, step):
        cp.wait()
    hb = _normed_input(xbuf_ref[slot], gain_ref, sh_ref, sc_ref, 0)
    proj = lambda w_ref: jnp.dot(hb, w_ref[...], preferred_element_type=F32)
    zf = proj(wf_ref)
    zb = proj(wb_ref)
    lb = _lower_bounds(lbl_ref)
    gate_f = _gate_terms(zf, lb[0:1, HALF_W:])
    q = proj(wq_ref)
    b_f = _chunk_log_decay(gate_f, False)
    gate_b = _gate_terms(zb, lb[1:2, HALF_W:])
    vi = proj(wi_ref)
    _emit_scan_operands(q, gate_f, b_f, False, *fwd_refs)
    b_b = _chunk_log_decay(gate_b, True)
    _emit_scan_operands(q, gate_b, b_b, True, *bwd_refs)
    v_ref[...] = vi.astype(v_ref.dtype)


def _inproj_colmajor(x, mod, gain, w_bf, lb_logits):
    t, d = x.shape
    rows = t // GRID_W
    tm = COLS_PER_STEP * rows
    row = lambda i: (i, 0)
    const2 = lambda i: (0, 0)
    w_spec = lambda g: pl.BlockSpec((d, HALF_W), lambda i: (0, 2 * g + 1), pipeline_mode=pl.Buffered(1))
    op_shapes, op_specs = _scan_operand_outputs(t, tm, HALF_W, row, lambda i: (i, 0, 0))
    return pl.pallas_call(
        _inproj_colmajor_kernel,
        grid=(GRID_W // COLS_PER_STEP,),
        in_specs=[pl.BlockSpec(memory_space=pl.ANY),
                  pl.BlockSpec((2, d), lambda i: (0, 0)),
                  pl.BlockSpec((2, d), lambda i: (0, 1)),
                  pl.BlockSpec((1, d), const2),
                  w_spec(2), w_spec(3), w_spec(4), w_spec(5),
                  pl.BlockSpec(lb_logits.shape, lambda i: (0, 0, 0))],
        out_specs=[pl.BlockSpec((tm, HALF_W), row)] + op_specs + op_specs,
        out_shape=[jax.ShapeDtypeStruct((t, HALF_W), BF16)] + op_shapes + op_shapes,
        scratch_shapes=[pltpu.VMEM((2, tm, d), F32), pltpu.SemaphoreType.DMA((2, COLS_PER_STEP))],
        compiler_params=_params(1),
        name="inproj_colmajor",
    )(x.reshape(rows, GRID_W, d), mod, mod, gain, w_bf, w_bf, w_bf, w_bf, lb_logits)


def _inproj_ctx_kernel(x_ref, sh_ref, sc_ref, gain_ref, w_ref, lbl_ref,
                       v_ref, kf_ref, df_ref, kb_ref, db_ref):
    hb = _normed_input(x_ref[...], gain_ref, sh_ref, sc_ref, 1)

    def proj(g):
        return jnp.dot(hb, w_ref[:, g * GROUP_W:(g + 1) * GROUP_W], preferred_element_type=F32)

    v_ref[...] = proj(0).astype(v_ref.dtype)
    lb = _lower_bounds(lbl_ref)
    _scan_operands(None, proj(1), lb[0:1, :], False, kf_ref, df_ref)
    _scan_operands(None, proj(2), lb[1:2, :], True, kb_ref, db_ref)


def _inproj_ctx(ctx, mod, gain, w_bf, lb_logits):
    t, d = ctx.shape
    const2 = lambda i: (0, 0)
    const3 = lambda i: (0, 0, 0)
    data = jax.ShapeDtypeStruct((t, GROUP_W), BF16)
    decay = jax.ShapeDtypeStruct((t // SCAN_CHUNK, TABLE_ROWS,GROUP_W), F32)
    data_spec = pl.BlockSpec(data.shape, const2)
    decay_spec = pl.BlockSpec(decay.shape, const3)
    return pl.pallas_call(
        _inproj_ctx_kernel,
        grid=(1,),
        in_specs=[pl.BlockSpec((t, d), const2),
                  pl.BlockSpec((2, d), lambda i: (0, 0)),
                  pl.BlockSpec((2, d), lambda i: (0, 1)),
                  pl.BlockSpec((1, d), const2),
                  pl.BlockSpec((d, 3 * GROUP_W), lambda i: (0, 1)),
                  pl.BlockSpec(lb_logits.shape, const3)],
        out_specs=[data_spec, data_spec, decay_spec, data_spec, decay_spec],
        out_shape=[data, data, decay, data, decay],
        compiler_params=_params(1),
        name="inproj_ctx",
    )(ctx, mod, mod, gain, w_bf, lb_logits)


def _intra_chunk_scores(q_c, k_c, table, backward):
    nt = (((1,), (1,)), ((), ()))
    blk = SCORE_BLOCK
    ti = lax.broadcasted_iota(jnp.int32, (SCAN_CHUNK, SCAN_CHUNK), 0)
    si = lax.broadcasted_iota(jnp.int32, (SCAN_CHUNK, SCAN_CHUNK), 1)
    if backward:
        ti, si = SCAN_CHUNK - 1 - ti, SCAN_CHUNK - 1 - si
    bt, bs = ti // blk, si // blk
    scores = jnp.where((bt == bs) & (si <= ti), lax.dot_general(q_c, k_c, nt, preferred_element_type=F32), 0.0)
    for level in range(1, N_LEVELS + 1):
        first = 1 + (level - 1) * N_BLOCKS
        to_edge = jnp.concatenate([jnp.broadcast_to(table[first + r:first + r + 1, :], (blk, HEAD_DIM))
                                   for r in range(N_BLOCKS)], axis=0).astype(BF16)
        sees = ((bs >> level == bt >> level) & ((bt >> (level - 1)) & 1 == 1) & ((bs >> (level - 1)) & 1 == 0))
        s_level = lax.dot_general(q_c * to_edge, k_c * to_edge, nt, preferred_element_type=F32)
        scores = jnp.where(sees, s_level, scores)
    return scores.astype(BF16)


CHUNKS_PER_STEP = 4


def _sub_chunk_block(backward, j, n_sub):
    return n_sub - 1 - j if backward else j


def _sub_chunk_rows(backward, j, n_sub):
    p = _sub_chunk_block(backward, j, n_sub)
    return slice(p * SCAN_CHUNK, (p + 1) * SCAN_CHUNK)


def _scan_chunks(groups, state_ref, o_refs):
    units = [(g, h, slice(h * HEAD_DIM, (h + 1) * HEAD_DIM)) for g in range(4) for h in range(HALF_HEADS)]
    with_outputs = o_refs is not None
    n_sub = groups[0][1].shape[0] // SCAN_CHUNK

    def operand(g, j, n, cs):
        return groups[g][0][_sub_chunk_rows(g >= 2, j, n_sub), n * HALF_W + cs.start:n * HALF_W + cs.stop]

    scores = {}
    if with_outputs:
        for j in range(n_sub):
            for g, h, cs in units:
                table = groups[g][2][_sub_chunk_block(g >= 2, j, n_sub), :, cs]
                scores[j, g, h] = _intra_chunk_scores(operand(g, j, 0, cs), operand(g, j, 1, cs), table, g >= 2)
    for j in range(n_sub):
        outs = {}
        for g, h, cs in units:
            ops_ref, v_ref, table_ref = groups[g]
            rows = _sub_chunk_rows(g >= 2, j, n_sub)
            u = g * HALF_HEADS + h
            st = state_ref[u]
            vh = v_ref[rows, cs]
            if with_outputs:
                outs[g, h] = jnp.dot(jnp.concatenate([scores[j, g, h], operand(g, j, 2, cs)], axis=1),
                                     jnp.concatenate([vh, st.astype(BF16)], axis=0), preferred_element_type=F32)
            decay_row = table_ref[_sub_chunk_block(g >= 2, j, n_sub), 0:1, cs]
            decay = jnp.broadcast_to(decay_row, (HEAD_DIM, HEAD_DIM)).T
            k_st = operand(g, j, 3, cs) if with_outputs else ops_ref[rows, cs]
            state_ref[u] = st * decay + lax.dot_general(
                k_st, vh, (((0,), (0,)), ((), ())), preferred_element_type=F32)
        if with_outputs:
            for g in range(4):
                o_ref = o_refs[g][j]
                o_ref[...] = jnp.concatenate([outs[g, h] for h in range(HALF_HEADS)], axis=1).astype(o_ref.dtype)


def _scatter_copy(buf_ref, out_hbm_ref, sem_ref, slot, j, chunk):
    per_col = out_hbm_ref.shape[0] // SCAN_CHUNK
    r0 = (chunk % per_col) * SCAN_CHUNK
    dst = out_hbm_ref.at[pl.ds(r0, SCAN_CHUNK), chunk // per_col, :]
    return pltpu.make_async_copy(buf_ref.at[slot, j], dst, sem_ref.at[slot, j])


def _scan_kernel(*refs, emit_outputs):
    groups = [refs[3 * g:3 * g + 3] for g in range(4)]
    s0_ref, rest = refs[12], refs[13:]
    step, n_steps = pl.program_id(0), pl.num_programs(0)
    state_ref = rest[4] if emit_outputs else rest[0]

    @pl.when(step == 0)
    def _():
        state_ref[...] = s0_ref[...]

    if not emit_outputs:
        _scan_chunks(groups, state_ref, None)
        return

    o_refs, (bufs, sems) = rest[:4], (rest[5:7], rest[7:9])
    slot = step % 2
    n_sub = bufs[0].shape[1]
    subs = range(n_sub)
    _scan_chunks(groups, state_ref,
                 [[o_refs[0].at[_sub_chunk_rows(False, j, n_sub), :] for j in subs], [bufs[0].at[slot, j] for j in subs],
                  [o_refs[2].at[_sub_chunk_rows(True, j, n_sub), :] for j in subs], [bufs[1].at[slot, j] for j in subs]])
    n_chunks = n_steps * n_sub
    for d, backward in enumerate((False, True)):
        chunk_of = ((lambda s, j: n_chunks - 1 - (s * n_sub + j)) if backward
                    else (lambda s, j: s * n_sub + j))
        copy = functools.partial(_scatter_copy, bufs[d], o_refs[2 * d + 1], sems[d])
        for j in subs:
            copy(slot, j, chunk_of(step, j)).start()

        @pl.when(step > 0)
        def _():
            for j in subs:
                copy(1 - slot, j, chunk_of(step - 1, j)).wait()

        @pl.when(step == n_steps - 1)
        def _():
            for j in subs:
                copy(slot, j, chunk_of(step, j)).wait()


def _scan(groups, s0, n_steps, n_sub, emit_outputs, grid_rows, name):
    blk = (n_sub * SCAN_CHUNK, HALF_W)
    args, in_specs = [], []
    for arrays, backward, col in groups:
        args.extend(arrays)
        block_of = (lambda s: n_steps - 1 - s) if backward else (lambda s: s)
        data_map = lambda s, block_of=block_of, col=col: (block_of(s), col)
        table_map = lambda s, block_of=block_of, col=col: (block_of(s), 0, col)
        operands, _, _ = arrays
        packed = operands.shape[1] == N_OPERANDS * HALF_W
        in_specs.append(pl.BlockSpec((blk[0], N_OPERANDS * HALF_W), lambda s, block_of=block_of:
                                     (block_of(s), 0)) if packed else pl.BlockSpec(blk, data_map))
        in_specs.append(pl.BlockSpec(blk, data_map))
        in_specs.append(pl.BlockSpec((n_sub, TABLE_ROWS, HALF_W), table_map))
    args.append(s0)
    state_spec = pl.BlockSpec(s0.shape, lambda s: (0, 0, 0))
    in_specs.append(state_spec)
    out_specs, out_shape, scratch = [], [], []
    if emit_outputs:
        raster = jax.ShapeDtypeStruct((grid_rows * GRID_W, HALF_W), BF16)
        scattered = jax.ShapeDtypeStruct((grid_rows, GRID_W, HALF_W), F32)
        in_hbm = pl.BlockSpec(memory_space=pl.ANY)
        out_shape = [raster, scattered, raster, scattered]
        out_specs = [pl.BlockSpec(blk, lambda s: (s, 0)), in_hbm,
                     pl.BlockSpec(blk, lambda s: (n_steps - 1 - s, 0)), in_hbm]
        staging = pltpu.VMEM((2, n_sub, SCAN_CHUNK, HALF_W), F32)
        scratch = [staging] * 2 + [pltpu.SemaphoreType.DMA((2, n_sub))] * 2
    out_shape.append(jax.ShapeDtypeStruct(s0.shape, F32))
    out_specs.append(state_spec)
    return pl.pallas_call(
        functools.partial(_scan_kernel, emit_outputs=emit_outputs),
        grid=(n_steps,),
        in_specs=in_specs,
        out_specs=out_specs,
        out_shape=out_shape,
        scratch_shapes=scratch,
        compiler_params=_params(1),
        name=name,
    )(*args)


def _scan_ctx(v, kf, df, kb, db):
    n_chunks = v.shape[0] // SCAN_CHUNK
    groups = [((kf, v, df), False, 0), ((kf, v, df), False, 1), ((kb, v, db), True, 0), ((kb, v, db), True, 1)]
    s0 = jnp.zeros((2 * HEADS, HEAD_DIM, HEAD_DIM), F32)
    (state,) = _scan(groups, s0, 1, n_chunks, False, 0, "scan_ctx")
    return state


def _scan_latent(raster, colmajor, s0):
    t = raster[0].shape[0]
    n_steps = t // (CHUNKS_PER_STEP * SCAN_CHUNK)
    groups = []
    for d, backward in enumerate((False, True)):
        for v, *operands in (raster, colmajor):
            packed, table = operands[2 * d:2 * d + 2]
            groups.append(((packed, v, table), backward, 0))
    o_fr, o_fc, o_br, o_bc, _ = _scan(groups, s0, n_steps, CHUNKS_PER_STEP, True, t // GRID_W, "scan_latent")
    return o_fr, o_fc.reshape(t, HALF_W), o_br, o_bc.reshape(t, HALF_W)


def _outproj_kernel(a_ref, ofr_ref, ofc_ref, obr_ref, obc_ref, sg_ref, x_ref, w_ref,
                    g1_ref, sh2_ref, sc2_ref, og_ref, n2g_ref, x1_ref, h2_ref):
    def readout(rs):
        o_r = ofr_ref[rs, :].astype(F32) + obr_ref[rs, :].astype(F32)
        o_c = ofc_ref[rs, :] + obc_ref[rs, :]
        ys = []
        for h in range(HEADS):
            o = o_r if h < HALF_HEADS else o_c
            oh = o[:, (h % HALF_HEADS) * HEAD_DIM:(h % HALF_HEADS + 1) * HEAD_DIM]
            cs = slice(h * HEAD_DIM, (h + 1) * HEAD_DIM)
            ys.append((oh * _rms_scale(oh) * og_ref[:, cs] * sg_ref[rs, cs].astype(F32)).astype(BF16))
        return jnp.concatenate(ys, axis=1)

    def finish(rs, proj):
        x1 = x_ref[rs, :] + g1_ref[0:1, :] * proj
        x1_ref[rs, :] = x1
        h2 = x1 * _rms_scale(x1) * n2g_ref[...]
        h2_ref[rs, :] = (h2 * (1.0 + sc2_ref[0:1, :]) + sh2_ref[0:1, :]).astype(h2_ref.dtype)

    half = x_ref.shape[0] // 2
    top, bottom = slice(0, half), slice(half, 2 * half)
    gmlp_side = lambda rs: jnp.dot(a_ref[rs, :], w_ref[:GROUP_W, :], preferred_element_type=F32)
    hgrn_side = lambda y: jnp.dot(y, w_ref[GROUP_W:, :], preferred_element_type=F32)
    p_top = gmlp_side(top)
    y_top = readout(top)
    p_bottom = gmlp_side(bottom)
    y_bottom = readout(bottom)
    p_top = p_top + hgrn_side(y_top)
    p_bottom = p_bottom + hgrn_side(y_bottom)
    finish(top, p_top)
    finish(bottom, p_bottom)


def _outproj(a, o_fr, o_fc, o_br, o_bc, sg, x, w_bf, mod, o_gain, n2_gain):
    t, d = x.shape
    tm = 512
    row = lambda i: (i, 0)
    const2 = lambda i: (0, 0)
    half = pl.BlockSpec((tm, HALF_W), row)
    full = pl.BlockSpec((tm, GROUP_W), row)
    mod_col = lambda c: pl.BlockSpec((2, d), lambda i: (0, c))
    return pl.pallas_call(
        _outproj_kernel,
        grid=(t // tm,),
        in_specs=[full, half, half, half, half, full,
                  pl.BlockSpec((tm, d), row),
                  pl.BlockSpec(w_bf.shape, const2, pipeline_mode=pl.Buffered(1)),
                  mod_col(2), mod_col(3), mod_col(4),
                  pl.BlockSpec((1, GROUP_W), const2),
                  pl.BlockSpec((1, d), const2)],
        out_specs=[pl.BlockSpec((tm, d), row), pl.BlockSpec((tm, d), row)],
        out_shape=[jax.ShapeDtypeStruct((t, d), F32), jax.ShapeDtypeStruct((t, d), BF16)],
        compiler_params=_params(1),
        name="outproj",
    )(a, o_fr, o_fc, o_br, o_bc, sg, x, w_bf, mod, mod, mod, o_gain, n2_gain)


def _ffn_kernel(h_ref, x1_ref, w1_ref, w2_ref, g2_ref, fg_ref, out_ref):
    f = pl.program_id(1)

    @pl.when(f == 0)
    def _():
        out_ref[...] = jnp.zeros_like(out_ref)

    hid = jnp.dot(h_ref[...], w1_ref[...], preferred_element_type=F32)
    hid = jnp.square(jnp.maximum(hid, 0.0)).astype(BF16)
    out_ref[...] += jnp.dot(hid, w2_ref[...], preferred_element_type=F32)

    @pl.when(f == pl.num_programs(1) - 1)
    def _():
        x2 = x1_ref[...] + g2_ref[0:1, :] * out_ref[...]
        out_ref[...] = x2 * _rms_scale(x2) * fg_ref[...]


def _ffn(h2, x1, w1_bf, w2_bf, mod, final_gain):
    t, d = x1.shape
    d_ff = w1_bf.shape[1]
    tm, tf = 1024, 1024
    row = lambda i, f: (i, 0)
    return pl.pallas_call(
        _ffn_kernel,
        grid=(t // tm, d_ff // tf),
        in_specs=[pl.BlockSpec((tm, d), row),
                  pl.BlockSpec((tm, d), row, pipeline_mode=pl.Buffered(1)),
                  pl.BlockSpec((d, tf), lambda i, f: (0, f)),
                  pl.BlockSpec((tf, d), lambda i, f: (f, 0)),
                  pl.BlockSpec((2, d), lambda i, f: (0, 5)),
                  pl.BlockSpec((1, d), lambda i, f: (0, 0))],
        out_specs=pl.BlockSpec((tm, d), row, pipeline_mode=pl.Buffered(1)),
        out_shape=jax.ShapeDtypeStruct((t, d), F32),
        compiler_params=_params(2),
        name="ffn",
    )(h2, x1, w1_bf, w2_bf, mod, final_gain)


def kernel(x, c, ctx, c_ctx, w_ada, b_ada, norm1_gain, w_in, gmlp_w_s, gmlp_b_s, gmlp_v_gain,
           hgrn_lb_logits, hgrn_o_gain, w_out, norm2_gain, w_ff1, w_ff2, final_gain):
    bsz, t, d = x.shape
    assert bsz == 1 and w_ada.shape[0] == 1, "single sample, single layer"
    assert w_in.shape[2] == 7 * GROUP_W and t % (GRID_W * SCAN_CHUNK) == 0

    cond_t = jnp.stack([c[0], c_ctx], axis=1)
    mod, w_in_bf = _adaln(cond_t, w_ada[0], b_ada, w_in[0])

    b_s_b = jnp.broadcast_to(gmlp_b_s[0][:, :, None], (HEADS, GMLP_CHUNK, HEAD_DIM))
    a, sg, *rest = _inproj_raster(x[0], mod, norm1_gain, w_in_bf, gmlp_w_s[0], b_s_b, gmlp_v_gain[0],
                                  hgrn_lb_logits, [w_out[0], w_ff1[0], w_ff2[0]])
    raster, (w_out_bf, w1_bf, w2_bf) = rest[:-N_CASTS], rest[-N_CASTS:]
    colmajor = _inproj_colmajor(x[0], mod, norm1_gain, w_in_bf, hgrn_lb_logits)
    ctx_ops = _inproj_ctx(ctx[0], mod, norm1_gain, w_in_bf, hgrn_lb_logits)

    s0 = _scan_ctx(*ctx_ops)
    o_fr, o_fc, o_br, o_bc = _scan_latent(raster, colmajor, s0)

    x1, h2 = _outproj(a, o_fr, o_fc, o_br, o_bc, sg, x[0], w_out_bf, mod,
                      hgrn_o_gain.reshape(1, GROUP_W), norm2_gain)
    out = _ffn(h2, x1, w1_bf, w2_bf, mod, final_gain.reshape(1, d))
    return out[None]
```

```python
import functools

import jax
import jax.numpy as jnp
from jax import lax
from jax.experimental import pallas as pl
from jax.experimental.pallas import tpu as pltpu

EPS = 1e-6
GRID_W = 64
HEADS = 8
HALF_HEADS = HEADS // 2
HEAD_DIM = 128
GROUP_W = HEADS * HEAD_DIM
HALF_W = HALF_HEADS * HEAD_DIM
GMLP_CHUNK = 128
SCAN_CHUNK = 128
LANES = 128
N_LEVELS = 2
N_BLOCKS = 1 << N_LEVELS
SCORE_BLOCK = SCAN_CHUNK // N_BLOCKS
TABLE_ROWS = 24
VMEM_LIMIT = 56 * 1024 * 1024

F32 = jnp.float32
BF16 = jnp.bfloat16


def _params(n_axes, vmem=VMEM_LIMIT):
    return pltpu.CompilerParams(dimension_semantics=("arbitrary",) * n_axes, vmem_limit_bytes=vmem)


def _sigmoid(x):
    return 1.0 / (1.0 + jnp.exp(-x))


def _gelu_tanh(x):
    return x * (0.5 * (1.0 + jnp.tanh(0.7978845608028654 * (x + 0.044715 * (x * x * x)))))


def _rms_scale(x):
    return lax.rsqrt(jnp.mean(x * x, axis=-1, keepdims=True) + EPS)


def _adaln_kernel(cond_t_ref, w_ref, b_ref, w_in_ref, out_ref, w_in_bf_ref, s_ref):
    w_in_bf_ref[...] = w_in_ref[...].astype(w_in_bf_ref.dtype)

    @pl.when(pl.program_id(0) == 0)
    def _():
        ct = cond_t_ref[...]
        s = ct * _sigmoid(ct)
        for r in range(2):
            s_ref[r] = jnp.broadcast_to(s[:, r:r + 1], s_ref.shape[1:])

    d, tn = w_ref.shape
    rows_per_step = 8

    def body(i, acc):
        rows = pl.ds(pl.multiple_of(i * rows_per_step, rows_per_step), rows_per_step)
        w = w_ref[rows, :]
        return tuple(a + w * jnp.tile(s_ref[r, rows, :], (1, tn // LANES)) for r, a in enumerate(acc))

    zeros = jnp.zeros((rows_per_step, tn), F32)
    acc = lax.fori_loop(0, d // rows_per_step, body, (zeros, zeros), unroll=8)
    out_ref[...] = jnp.concatenate([jnp.sum(a, axis=0, keepdims=True) for a in acc], axis=0) + b_ref[...]


def _adaln(cond_t, w_ada, b_ada, w_in):
    d, n = w_ada.shape
    n_steps = 16
    tn, slab = n // n_steps, w_in.shape[0] // n_steps
    return pl.pallas_call(
        _adaln_kernel,
        grid=(n_steps,),
        in_specs=[pl.BlockSpec((d, 2), lambda j: (0, 0)),
                  pl.BlockSpec((d, tn), lambda j: (0, j)),
                  pl.BlockSpec((1, tn), lambda j: (0, j)),
                  pl.BlockSpec((slab, w_in.shape[1]), lambda j: (j, 0))],
        out_specs=[pl.BlockSpec((2, tn), lambda j: (0, j)),
                   pl.BlockSpec((slab, w_in.shape[1]), lambda j: (j, 0))],
        out_shape=[jax.ShapeDtypeStruct((2, n), F32), jax.ShapeDtypeStruct(w_in.shape, BF16)],
        scratch_shapes=[pltpu.VMEM((2, d, LANES), F32)],
        compiler_params=_params(1),
        name="adaln",
    )(cond_t, w_ada, b_ada, w_in)


def _lower_bounds(lbl_ref):
    l0, l1 = lbl_ref[0], lbl_ref[1]
    m = jnp.maximum(l0, l1)
    e0, e1 = jnp.exp(l0 - m), jnp.exp(l1 - m)
    return e1 / (e0 + e1)


def _visible(backward):
    ti = lax.broadcasted_iota(jnp.int32, (SCAN_CHUNK, SCAN_CHUNK), 0)
    si = lax.broadcasted_iota(jnp.int32, (SCAN_CHUNK, SCAN_CHUNK), 1)
    return (si >= ti) if backward else (si <= ti)


def _gate_terms(z, lb):
    f = lb + (1.0 - lb) * _sigmoid(z)
    lf = jnp.log(f)
    hi = lf.astype(BF16)
    lo = (lf - hi.astype(F32)).astype(BF16)
    return 1.0 - f, hi, lo


def _chunk_log_decay(gate, backward):
    _, hi, lo = gate
    tri = _visible(backward).astype(BF16)
    c = SCAN_CHUNK
    return [jnp.dot(tri, hi[j * c:(j + 1) * c], preferred_element_type=F32)
            + jnp.dot(tri, lo[j * c:(j + 1) * c], preferred_element_type=F32)
            for j in range(hi.shape[0] // c)]


N_OPERANDS = 2


def _scan_block(r, backward):
    return N_BLOCKS - 1 - r if backward else r


def _is_later(i, level):
    return (i >> (level - 1)) & 1 == 1


def _level_edge(i, level):
    group = i >> level
    return (group * (1 << level) + (1 << (level - 1))) * SCORE_BLOCK - 1


def _emit_scan_operands(q, gate, bs, backward, out_ref, table_ref):
    k = gate[0]
    w = k.shape[1]
    row_of = lambda pos: SCAN_CHUNK - 1 - pos if backward else pos
    per_block = lambda vals: jnp.concatenate(
        [jnp.broadcast_to(vals[_scan_block(r, backward)], (SCORE_BLOCK, w)) for r in range(N_BLOCKS)], axis=0)
    for j, b in enumerate(bs):
        rs = slice(j * SCAN_CHUNK, (j + 1) * SCAN_CHUNK)
        at = lambda pos: b[row_of(pos):row_of(pos) + 1, :]
        b_last = at(SCAN_CHUNK - 1)
        if q is None:
            out_ref[rs, :] = (k[rs] * jnp.exp(b_last - b)).astype(BF16)
            table_ref[j] = jnp.broadcast_to(jnp.exp(b_last), table_ref.shape[1:])
            continue
        centre = [0.5 * (at(i * SCORE_BLOCK) + at((i + 1) * SCORE_BLOCK - 1)) for i in range(N_BLOCKS)]
        centre_rows = per_block(centre)
        q_c = q[rs] * jnp.exp(b - centre_rows)
        k_c = k[rs] * jnp.exp(centre_rows - b)
        out_ref[rs, 0:w] = q_c.astype(BF16)
        out_ref[rs, w:2 * w] = k_c.astype(BF16)
        rows = [jnp.exp(b_last)]
        for level in range(1, N_LEVELS + 1):
            to_edge = []
            for i in range(N_BLOCKS):
                edge = at(_level_edge(i, level))
                to_edge.append(jnp.exp(centre[i] - edge) if _is_later(i, level) else jnp.exp(edge - centre[i]))
            rows.extend(to_edge[_scan_block(r, backward)] for r in range(N_BLOCKS))
        rows.extend(jnp.exp(centre[_scan_block(r, backward)]) for r in range(N_BLOCKS))
        rows.extend(jnp.exp(b_last - centre[_scan_block(r, backward)]) for r in range(N_BLOCKS))
        rows.append(jnp.zeros((TABLE_ROWS - len(rows), w), F32))
        table_ref[j] = jnp.concatenate(rows, axis=0)


def _scan_operands(q, z, lb, backward, out_ref, decay_ref):
    gate = _gate_terms(z, lb)
    _emit_scan_operands(q, gate, _chunk_log_decay(gate, backward), backward, out_ref, decay_ref)


def _normed_input(x, gain_ref, sh_ref, sc_ref, row):
    h = x * _rms_scale(x) * gain_ref[...]
    h = h * (1.0 + sc_ref[row:row + 1, :]) + sh_ref[row:row + 1, :]
    return h.astype(BF16)


N_CASTS = 3


def _inproj_raster_kernel(x_ref, sh_ref, sc_ref, gain_ref, wuv_ref, wg_ref, wq_ref, wi_ref, wf_ref, wb_ref,
                          ws_ref, bs_ref, vg_ref, lbl_ref, *rest):
    cast_ins, rest = rest[:N_CASTS], rest[N_CASTS:]
    a_ref, sg_ref, v_ref = rest[:3]
    fwd_refs, bwd_refs, cast_outs = rest[3:5], rest[5:7], rest[7:]
    hb = _normed_input(x_ref[...], gain_ref, sh_ref, sc_ref, 0)
    proj = lambda w_ref: jnp.dot(hb, w_ref[...], preferred_element_type=F32)

    q = proj(wq_ref)
    zf = proj(wf_ref)
    zb = proj(wb_ref)
    lb = _lower_bounds(lbl_ref)
    gate_f = _gate_terms(zf, lb[0:1, :HALF_W])
    u = jnp.dot(hb, wuv_ref[:, :GROUP_W], preferred_element_type=F32)
    b_f = _chunk_log_decay(gate_f, False)
    gate_b = _gate_terms(zb, lb[1:2, :HALF_W])
    v = jnp.dot(hb, wuv_ref[:, GROUP_W:], preferred_element_type=F32)
    _emit_scan_operands(q, gate_f, b_f, False, *fwd_refs)
    b_b = _chunk_log_decay(gate_b, True)
    g = proj(wg_ref)
    _emit_scan_operands(q, gate_b, b_b, True, *bwd_refs)
    for w_ref, wb16_ref in zip(cast_ins, cast_outs):
        wb16_ref[...] = w_ref[...].astype(wb16_ref.dtype)
    u = _gelu_tanh(u)
    v = _gelu_tanh(v)

    chunks = [slice(c * GMLP_CHUNK, (c + 1) * GMLP_CHUNK) for c in range(u.shape[0] // GMLP_CHUNK)]
    for h in range(HEADS):
        cs = slice(h * HEAD_DIM, (h + 1) * HEAD_DIM)
        vh = v[:, cs]
        vn = (vh * _rms_scale(vh) * vg_ref[h:h + 1, :]).astype(BF16)
        mixed = jnp.dot(ws_ref[h].astype(BF16), jnp.concatenate([vn[rs] for rs in chunks], axis=1),
                        preferred_element_type=F32)
        for c, rs in enumerate(chunks):
            a_ref[rs, cs] = (u[rs, cs] * (mixed[:, c * HEAD_DIM:(c + 1) * HEAD_DIM] + bs_ref[h])).astype(a_ref.dtype)

    vi = proj(wi_ref)
    sg_ref[...] = (g * _sigmoid(g)).astype(sg_ref.dtype)
    v_ref[...] = vi.astype(v_ref.dtype)


def _scan_operand_outputs(t, tm, width, index_map, decay_map):
    data = jax.ShapeDtypeStruct((t, N_OPERANDS * width), BF16)
    decay = jax.ShapeDtypeStruct((t // SCAN_CHUNK, TABLE_ROWS,width), F32)
    data_spec = pl.BlockSpec((tm, N_OPERANDS * width), index_map)
    decay_spec = pl.BlockSpec((tm // SCAN_CHUNK, TABLE_ROWS,width), decay_map)
    return [data, decay], [data_spec, decay_spec]


def _inproj_raster(x, mod, gain, w_bf, w_s, b_s_b, v_gain, lb_logits, casts):
    t, d = x.shape
    tm = 256
    n_steps = t // tm
    row = lambda i: (i, 0)
    const2 = lambda i: (0, 0)
    const3 = lambda i: (0, 0, 0)
    full = pl.BlockSpec((tm, GROUP_W), row)
    half = pl.BlockSpec((tm, HALF_W), row)
    resident = lambda width, blk: pl.BlockSpec((d, width), lambda i: (0, blk), pipeline_mode=pl.Buffered(1))
    cast_specs = [pl.BlockSpec((w.shape[0] // n_steps, w.shape[1]), row) for w in casts]
    op_shapes, op_specs = _scan_operand_outputs(t, tm, HALF_W, row, lambda i: (i, 0, 0))
    return pl.pallas_call(
        _inproj_raster_kernel,
        grid=(n_steps,),
        in_specs=[pl.BlockSpec((tm, d), row),
                  pl.BlockSpec((2, d), lambda i: (0, 0)),
                  pl.BlockSpec((2, d), lambda i: (0, 1)),
                  pl.BlockSpec((1, d), const2),
                  resident(2 * GROUP_W, 0), resident(GROUP_W, 6),
                  resident(HALF_W, 4), resident(HALF_W, 6), resident(HALF_W, 8), resident(HALF_W, 10),
                  pl.BlockSpec(w_s.shape, const3),
                  pl.BlockSpec(b_s_b.shape, const3),
                  pl.BlockSpec(v_gain.shape, const2),
                  pl.BlockSpec(lb_logits.shape, const3)] + cast_specs,
        out_specs=[full, full, half] + op_specs + op_specs + cast_specs,
        out_shape=([jax.ShapeDtypeStruct((t, GROUP_W), BF16)] * 2 + [jax.ShapeDtypeStruct((t, HALF_W), BF16)]
                   + op_shapes + op_shapes + [jax.ShapeDtypeStruct(w.shape, BF16) for w in casts]),
        compiler_params=_params(1),
        name="inproj_raster",
    )(x, mod, mod, gain, *([w_bf] * 6), w_s, b_s_b, v_gain, lb_logits, *casts)


COLS_PER_STEP = 4


def _gather_copies(x_hbm_ref, buf_ref, sem_ref, slot, step):
    rows = x_hbm_ref.shape[0]
    return [pltpu.make_async_copy(x_hbm_ref.at[:, step * COLS_PER_STEP + wi, :],
                                  buf_ref.at[slot, pl.ds(wi * rows, rows), :],
                                  sem_ref.at[slot, wi])
            for wi in range(COLS_PER_STEP)]


def _inproj_colmajor_kernel(x_hbm_ref, sh_ref, sc_ref, gain_ref, wq_ref, wi_ref, wf_ref, wb_ref, lbl_ref,
                            v_ref, *rest):
    fwd_refs, bwd_refs, (xbuf_ref, sem_ref) = rest[:2], rest[2:4], rest[4:]
    step, n_steps = pl.program_id(0), pl.num_programs(0)
    slot = step % 2
    copies = functools.partial(_gather_copies, x_hbm_ref, xbuf_ref, sem_ref)

    @pl.when(step == 0)
    def _():
        for cp in copies(slot, step):
            cp.start()

    @pl.when(step + 1 < n_steps)
    def _():
        for cp in copies(1 - slot, step + 1):
            cp.start()

    for cp in copies(slot, step):
        cp.wait()
    hb = _normed_input(xbuf_ref[slot], gain_ref, sh_ref, sc_ref, 0)
    proj = lambda w_ref: jnp.dot(hb, w_ref[...], preferred_element_type=F32)
    zf = proj(wf_ref)
    zb = proj(wb_ref)
    lb = _lower_bounds(lbl_ref)
    gate_f = _gate_terms(zf, lb[0:1, HALF_W:])
    q = proj(wq_ref)
    b_f = _chunk_log_decay(gate_f, False)
    gate_b = _gate_terms(zb, lb[1:2, HALF_W:])
    vi = proj(wi_ref)
    _emit_scan_operands(q, gate_f, b_f, False, *fwd_refs)
    b_b = _chunk_log_decay(gate_b, True)
    _emit_scan_operands(q, gate_b, b_b, True, *bwd_refs)
    v_ref[...] = vi.astype(v_ref.dtype)


def _inproj_colmajor(x, mod, gain, w_bf, lb_logits):
    t, d = x.shape
    rows = t // GRID_W
    tm = COLS_PER_STEP * rows
    row = lambda i: (i, 0)
    const2 = lambda i: (0, 0)
    w_spec = lambda g: pl.BlockSpec((d, HALF_W), lambda i: (0, 2 * g + 1), pipeline_mode=pl.Buffered(1))
    op_shapes, op_specs = _scan_operand_outputs(t, tm, HALF_W, row, lambda i: (i, 0, 0))
    return pl.pallas_call(
        _inproj_colmajor_kernel,
        grid=(GRID_W // COLS_PER_STEP,),
        in_specs=[pl.BlockSpec(memory_space=pl.ANY),
                  pl.BlockSpec((2, d), lambda i: (0, 0)),
                  pl.BlockSpec((2, d), lambda i: (0, 1)),
                  pl.BlockSpec((1, d), const2),
                  w_spec(2), w_spec(3), w_spec(4), w_spec(5),
                  pl.BlockSpec(lb_logits.shape, lambda i: (0, 0, 0))],
        out_specs=[pl.BlockSpec((tm, HALF_W), row)] + op_specs + op_specs,
        out_shape=[jax.ShapeDtypeStruct((t, HALF_W), BF16)] + op_shapes + op_shapes,
        scratch_shapes=[pltpu.VMEM((2, tm, d), F32), pltpu.SemaphoreType.DMA((2, COLS_PER_STEP))],
        compiler_params=_params(1),
        name="inproj_colmajor",
    )(x.reshape(rows, GRID_W, d), mod, mod, gain, w_bf, w_bf, w_bf, w_bf, lb_logits)


def _inproj_ctx_kernel(x_ref, sh_ref, sc_ref, gain_ref, w_ref, lbl_ref,
                       v_ref, kf_ref, df_ref, kb_ref, db_ref):
    hb = _normed_input(x_ref[...], gain_ref, sh_ref, sc_ref, 1)

    def proj(g):
        return jnp.dot(hb, w_ref[:, g * GROUP_W:(g + 1) * GROUP_W], preferred_element_type=F32)

    v_ref[...] = proj(0).astype(v_ref.dtype)
    lb = _lower_bounds(lbl_ref)
    _scan_operands(None, proj(1), lb[0:1, :], False, kf_ref, df_ref)
    _scan_operands(None, proj(2), lb[1:2, :], True, kb_ref, db_ref)


def _inproj_ctx(ctx, mod, gain, w_bf, lb_logits):
    t, d = ctx.shape
    const2 = lambda i: (0, 0)
    const3 = lambda i: (0, 0, 0)
    data = jax.ShapeDtypeStruct((t, GROUP_W), BF16)
    decay = jax.ShapeDtypeStruct((t // SCAN_CHUNK, TABLE_ROWS,GROUP_W), F32)
    data_spec = pl.BlockSpec(data.shape, const2)
    decay_spec = pl.BlockSpec(decay.shape, const3)
    return pl.pallas_call(
        _inproj_ctx_kernel,
        grid=(1,),
        in_specs=[pl.BlockSpec((t, d), const2),
                  pl.BlockSpec((2, d), lambda i: (0, 0)),
                  pl.BlockSpec((2, d), lambda i: (0, 1)),
                  pl.BlockSpec((1, d), const2),
                  pl.BlockSpec((d, 3 * GROUP_W), lambda i: (0, 1)),
                  pl.BlockSpec(lb_logits.shape, const3)],
        out_specs=[data_spec, data_spec, decay_spec, data_spec, decay_spec],
        out_shape=[data, data, decay, data, decay],
        compiler_params=_params(1),
        name="inproj_ctx",
    )(ctx, mod, mod, gain, w_bf, lb_logits)


def _block_factor(table, first):
    return jnp.concatenate([jnp.broadcast_to(table[first + r:first + r + 1, :], (SCORE_BLOCK, HEAD_DIM))
                            for r in range(N_BLOCKS)], axis=0).astype(BF16)


def _intra_chunk_scores(q_c, k_c, table, backward):
    nt = (((1,), (1,)), ((), ()))
    blk = SCORE_BLOCK
    ti = lax.broadcasted_iota(jnp.int32, (SCAN_CHUNK, SCAN_CHUNK), 0)
    si = lax.broadcasted_iota(jnp.int32, (SCAN_CHUNK, SCAN_CHUNK), 1)
    if backward:
        ti, si = SCAN_CHUNK - 1 - ti, SCAN_CHUNK - 1 - si
    bt, bs = ti // blk, si // blk
    scores = jnp.where((bt == bs) & (si <= ti), lax.dot_general(q_c, k_c, nt, preferred_element_type=F32), 0.0)
    for level in range(1, N_LEVELS + 1):
        to_edge = _block_factor(table, 1 + (level - 1) * N_BLOCKS)
        sees = ((bs >> level == bt >> level) & ((bt >> (level - 1)) & 1 == 1) & ((bs >> (level - 1)) & 1 == 0))
        s_level = lax.dot_general(q_c * to_edge, k_c * to_edge, nt, preferred_element_type=F32)
        scores = jnp.where(sees, s_level, scores)
    return scores.astype(BF16)


CHUNKS_PER_STEP = 4


def _sub_chunk_block(backward, j, n_sub):
    return n_sub - 1 - j if backward else j


def _sub_chunk_rows(backward, j, n_sub):
    p = _sub_chunk_block(backward, j, n_sub)
    return slice(p * SCAN_CHUNK, (p + 1) * SCAN_CHUNK)


def _scan_chunks(groups, state_ref, o_refs):
    units = [(g, h, slice(h * HEAD_DIM, (h + 1) * HEAD_DIM)) for g in range(4) for h in range(HALF_HEADS)]
    with_outputs = o_refs is not None
    n_sub = groups[0][1].shape[0] // SCAN_CHUNK

    def operand(g, j, n, cs):
        return groups[g][0][_sub_chunk_rows(g >= 2, j, n_sub), n * HALF_W + cs.start:n * HALF_W + cs.stop]

    scores = {}
    if with_outputs:
        for j in range(n_sub):
            for g, h, cs in units:
                table = groups[g][2][_sub_chunk_block(g >= 2, j, n_sub), :, cs]
                scores[j, g, h] = _intra_chunk_scores(operand(g, j, 0, cs), operand(g, j, 1, cs), table, g >= 2)
    for j in range(n_sub):
        outs = {}
        for g, h, cs in units:
            ops_ref, v_ref, table_ref = groups[g]
            rows = _sub_chunk_rows(g >= 2, j, n_sub)
            u = g * HALF_HEADS + h
            st = state_ref[u]
            vh = v_ref[rows, cs]
            table = table_ref[_sub_chunk_block(g >= 2, j, n_sub), :, cs]
            if with_outputs:
                q_st = operand(g, j, 0, cs) * _block_factor(table, 1 + N_LEVELS * N_BLOCKS)
                k_st = operand(g, j, 1, cs) * _block_factor(table, 1 + (N_LEVELS + 1) * N_BLOCKS)
                outs[g, h] = jnp.dot(jnp.concatenate([scores[j, g, h], q_st], axis=1),
                                     jnp.concatenate([vh, st.astype(BF16)], axis=0), preferred_element_type=F32)
            else:
                k_st = ops_ref[rows, cs]
            decay = jnp.broadcast_to(table[0:1, :], (HEAD_DIM, HEAD_DIM)).T
            state_ref[u] = st * decay + lax.dot_general(
                k_st, vh, (((0,), (0,)), ((), ())), preferred_element_type=F32)
        if with_outputs:
            for g in range(4):
                o_ref = o_refs[g][j]
                o_ref[...] = jnp.concatenate([outs[g, h] for h in range(HALF_HEADS)], axis=1).astype(o_ref.dtype)


def _scatter_copy(buf_ref, out_hbm_ref, sem_ref, slot, j, chunk):
    per_col = out_hbm_ref.shape[0] // SCAN_CHUNK
    r0 = (chunk % per_col) * SCAN_CHUNK
    dst = out_hbm_ref.at[pl.ds(r0, SCAN_CHUNK), chunk // per_col, :]
    return pltpu.make_async_copy(buf_ref.at[slot, j], dst, sem_ref.at[slot, j])


def _scan_kernel(*refs, emit_outputs):
    groups = [refs[3 * g:3 * g + 3] for g in range(4)]
    s0_ref, rest = refs[12], refs[13:]
    step, n_steps = pl.program_id(0), pl.num_programs(0)
    state_ref = rest[4] if emit_outputs else rest[0]

    @pl.when(step == 0)
    def _():
        state_ref[...] = s0_ref[...]

    if not emit_outputs:
        _scan_chunks(groups, state_ref, None)
        return

    o_refs, (bufs, sems) = rest[:4], (rest[5:7], rest[7:9])
    slot = step % 2
    n_sub = bufs[0].shape[1]
    subs = range(n_sub)
    _scan_chunks(groups, state_ref,
                 [[o_refs[0].at[_sub_chunk_rows(False, j, n_sub), :] for j in subs], [bufs[0].at[slot, j] for j in subs],
                  [o_refs[2].at[_sub_chunk_rows(True, j, n_sub), :] for j in subs], [bufs[1].at[slot, j] for j in subs]])
    n_chunks = n_steps * n_sub
    for d, backward in enumerate((False, True)):
        chunk_of = ((lambda s, j: n_chunks - 1 - (s * n_sub + j)) if backward
                    else (lambda s, j: s * n_sub + j))
        copy = functools.partial(_scatter_copy, bufs[d], o_refs[2 * d + 1], sems[d])
        for j in subs:
            copy(slot, j, chunk_of(step, j)).start()

        @pl.when(step > 0)
        def _():
            for j in subs:
                copy(1 - slot, j, chunk_of(step - 1, j)).wait()

        @pl.when(step == n_steps - 1)
        def _():
            for j in subs:
                copy(slot, j, chunk_of(step, j)).wait()


def _scan(groups, s0, n_steps, n_sub, emit_outputs, grid_rows, name):
    blk = (n_sub * SCAN_CHUNK, HALF_W)
    args, in_specs = [], []
    for arrays, backward, col in groups:
        args.extend(arrays)
        block_of = (lambda s: n_steps - 1 - s) if backward else (lambda s: s)
        data_map = lambda s, block_of=block_of, col=col: (block_of(s), col)
        table_map = lambda s, block_of=block_of, col=col: (block_of(s), 0, col)
        in_specs.append(pl.BlockSpec((blk[0], N_OPERANDS * HALF_W), lambda s, block_of=block_of:
                                     (block_of(s), 0)) if emit_outputs else pl.BlockSpec(blk, data_map))
        in_specs.append(pl.BlockSpec(blk, data_map))
        in_specs.append(pl.BlockSpec((n_sub, TABLE_ROWS, HALF_W), table_map))
    args.append(s0)
    state_spec = pl.BlockSpec(s0.shape, lambda s: (0, 0, 0))
    in_specs.append(state_spec)
    out_specs, out_shape, scratch = [], [], []
    if emit_outputs:
        raster = jax.ShapeDtypeStruct((grid_rows * GRID_W, HALF_W), BF16)
        scattered = jax.ShapeDtypeStruct((grid_rows, GRID_W, HALF_W), F32)
        in_hbm = pl.BlockSpec(memory_space=pl.ANY)
        out_shape = [raster, scattered, raster, scattered]
        out_specs = [pl.BlockSpec(blk, lambda s: (s, 0)), in_hbm,
                     pl.BlockSpec(blk, lambda s: (n_steps - 1 - s, 0)), in_hbm]
        staging = pltpu.VMEM((2, n_sub, SCAN_CHUNK, HALF_W), F32)
        scratch = [staging] * 2 + [pltpu.SemaphoreType.DMA((2, n_sub))] * 2
    out_shape.append(jax.ShapeDtypeStruct(s0.shape, F32))
    out_specs.append(state_spec)
    return pl.pallas_call(
        functools.partial(_scan_kernel, emit_outputs=emit_outputs),
        grid=(n_steps,),
        in_specs=in_specs,
        out_specs=out_specs,
        out_shape=out_shape,
        scratch_shapes=scratch,
        compiler_params=_params(1),
        name=name,
    )(*args)


def _scan_ctx(v, kf, df, kb, db):
    n_chunks = v.shape[0] // SCAN_CHUNK
    groups = [((kf, v, df), False, 0), ((kf, v, df), False, 1), ((kb, v, db), True, 0), ((kb, v, db), True, 1)]
    s0 = jnp.zeros((2 * HEADS, HEAD_DIM, HEAD_DIM), F32)
    (state,) = _scan(groups, s0, 1, n_chunks, False, 0, "scan_ctx")
    return state


def _scan_latent(raster, colmajor, s0):
    t = raster[0].shape[0]
    n_steps = t // (CHUNKS_PER_STEP * SCAN_CHUNK)
    groups = []
    for d, backward in enumerate((False, True)):
        for v, *operands in (raster, colmajor):
            packed, table = operands[2 * d:2 * d + 2]
            groups.append(((packed, v, table), backward, 0))
    o_fr, o_fc, o_br, o_bc, _ = _scan(groups, s0, n_steps, CHUNKS_PER_STEP, True, t // GRID_W, "scan_latent")
    return o_fr, o_fc.reshape(t, HALF_W), o_br, o_bc.reshape(t, HALF_W)


def _outproj_kernel(a_ref, ofr_ref, ofc_ref, obr_ref, obc_ref, sg_ref, x_ref, w_ref,
                    g1_ref, sh2_ref, sc2_ref, og_ref, n2g_ref, x1_ref, h2_ref):
    def readout(rs):
        o_r = ofr_ref[rs, :].astype(F32) + obr_ref[rs, :].astype(F32)
        o_c = ofc_ref[rs, :] + obc_ref[rs, :]
        ys = []
        for h in range(HEADS):
            o = o_r if h < HALF_HEADS else o_c
            oh = o[:, (h % HALF_HEADS) * HEAD_DIM:(h % HALF_HEADS + 1) * HEAD_DIM]
            cs = slice(h * HEAD_DIM, (h + 1) * HEAD_DIM)
            ys.append((oh * _rms_scale(oh) * og_ref[:, cs] * sg_ref[rs, cs].astype(F32)).astype(BF16))
        return jnp.concatenate(ys, axis=1)

    def finish(rs, proj):
        x1 = x_ref[rs, :] + g1_ref[0:1, :] * proj
        x1_ref[rs, :] = x1
        h2 = x1 * _rms_scale(x1) * n2g_ref[...]
        h2_ref[rs, :] = (h2 * (1.0 + sc2_ref[0:1, :]) + sh2_ref[0:1, :]).astype(h2_ref.dtype)

    half = x_ref.shape[0] // 2
    top, bottom = slice(0, half), slice(half, 2 * half)
    gmlp_side = lambda rs: jnp.dot(a_ref[rs, :], w_ref[:GROUP_W, :], preferred_element_type=F32)
    hgrn_side = lambda y: jnp.dot(y, w_ref[GROUP_W:, :], preferred_element_type=F32)
    p_top = gmlp_side(top)
    y_top = readout(top)
    p_bottom = gmlp_side(bottom)
    y_bottom = readout(bottom)
    p_top = p_top + hgrn_side(y_top)
    p_bottom = p_bottom + hgrn_side(y_bottom)
    finish(top, p_top)
    finish(bottom, p_bottom)


def _outproj(a, o_fr, o_fc, o_br, o_bc, sg, x, w_bf, mod, o_gain, n2_gain):
    t, d = x.shape
    tm = 512
    row = lambda i: (i, 0)
    const2 = lambda i: (0, 0)
    half = pl.BlockSpec((tm, HALF_W), row)
    full = pl.BlockSpec((tm, GROUP_W), row)
    mod_col = lambda c: pl.BlockSpec((2, d), lambda i: (0, c))
    return pl.pallas_call(
        _outproj_kernel,
        grid=(t // tm,),
        in_specs=[full, half, half, half, half, full,
                  pl.BlockSpec((tm, d), row),
                  pl.BlockSpec(w_bf.shape, const2, pipeline_mode=pl.Buffered(1)),
                  mod_col(2), mod_col(3), mod_col(4),
                  pl.BlockSpec((1, GROUP_W), const2),
                  pl.BlockSpec((1, d), const2)],
        out_specs=[pl.BlockSpec((tm, d), row), pl.BlockSpec((tm, d), row)],
        out_shape=[jax.ShapeDtypeStruct((t, d), F32), jax.ShapeDtypeStruct((t, d), BF16)],
        compiler_params=_params(1),
        name="outproj",
    )(a, o_fr, o_fc, o_br, o_bc, sg, x, w_bf, mod, mod, mod, o_gain, n2_gain)


def _ffn_kernel(h_ref, x1_ref, w1_ref, w2_ref, g2_ref, fg_ref, out_ref):
    f = pl.program_id(1)

    @pl.when(f == 0)
    def _():
        out_ref[...] = jnp.zeros_like(out_ref)

    hid = jnp.dot(h_ref[...], w1_ref[...], preferred_element_type=F32)
    hid = jnp.square(jnp.maximum(hid, 0.0)).astype(BF16)
    out_ref[...] += jnp.dot(hid, w2_ref[...], preferred_element_type=F32)

    @pl.when(f == pl.num_programs(1) - 1)
    def _():
        x2 = x1_ref[...] + g2_ref[0:1, :] * out_ref[...]
        out_ref[...] = x2 * _rms_scale(x2) * fg_ref[...]


def _ffn(h2, x1, w1_bf, w2_bf, mod, final_gain):
    t, d = x1.shape
    d_ff = w1_bf.shape[1]
    tm, tf = 1024, 1024
    row = lambda i, f: (i, 0)
    return pl.pallas_call(
        _ffn_kernel,
        grid=(t // tm, d_ff // tf),
        in_specs=[pl.BlockSpec((tm, d), row),
                  pl.BlockSpec((tm, d), row, pipeline_mode=pl.Buffered(1)),
                  pl.BlockSpec((d, tf), lambda i, f: (0, f)),
                  pl.BlockSpec((tf, d), lambda i, f: (f, 0)),
                  pl.BlockSpec((2, d), lambda i, f: (0, 5)),
                  pl.BlockSpec((1, d), lambda i, f: (0, 0))],
        out_specs=pl.BlockSpec((tm, d), row, pipeline_mode=pl.Buffered(1)),
        out_shape=jax.ShapeDtypeStruct((t, d), F32),
        compiler_params=_params(2),
        name="ffn",
    )(h2, x1, w1_bf, w2_bf, mod, final_gain)


def kernel(x, c, ctx, c_ctx, w_ada, b_ada, norm1_gain, w_in, gmlp_w_s, gmlp_b_s, gmlp_v_gain,
           hgrn_lb_logits, hgrn_o_gain, w_out, norm2_gain, w_ff1, w_ff2, final_gain):
    bsz, t, d = x.shape
    assert bsz == 1 and w_ada.shape[0] == 1, "single sample, single layer"
    assert w_in.shape[2] == 7 * GROUP_W and t % (GRID_W * SCAN_CHUNK) == 0

    cond_t = jnp.stack([c[0], c_ctx], axis=1)
    mod, w_in_bf = _adaln(cond_t, w_ada[0], b_ada, w_in[0])

    b_s_b = jnp.broadcast_to(gmlp_b_s[0][:, :, None], (HEADS, GMLP_CHUNK, HEAD_DIM))
    a, sg, *rest = _inproj_raster(x[0], mod, norm1_gain, w_in_bf, gmlp_w_s[0], b_s_b, gmlp_v_gain[0],
                                  hgrn_lb_logits, [w_out[0], w_ff1[0], w_ff2[0]])
    raster, (w_out_bf, w1_bf, w2_bf) = rest[:-N_CASTS], rest[-N_CASTS:]
    colmajor = _inproj_colmajor(x[0], mod, norm1_gain, w_in_bf, hgrn_lb_logits)
    ctx_ops = _inproj_ctx(ctx[0], mod, norm1_gain, w_in_bf, hgrn_lb_logits)

    s0 = _scan_ctx(*ctx_ops)
    o_fr, o_fc, o_br, o_bc = _scan_latent(raster, colmajor, s0)

    x1, h2 = _outproj(a, o_fr, o_fc, o_br, o_bc, sg, x[0], w_out_bf, mod,
                      hgrn_o_gain.reshape(1, GROUP_W), norm2_gain)
    out = _ffn(h2, x1, w1_bf, w2_bf, mod, final_gain.reshape(1, d))
    return out[None]
```

```python
import functools

import jax
import jax.numpy as jnp
from jax import lax
from jax.experimental import pallas as pl
from jax.experimental.pallas import tpu as pltpu

EPS = 1e-6
GRID_W = 64
HEADS = 8
HALF_HEADS = HEADS // 2
HEAD_DIM = 128
GROUP_W = HEADS * HEAD_DIM
HALF_W = HALF_HEADS * HEAD_DIM
GMLP_CHUNK = 128
SCAN_CHUNK = 128
LANES = 128
N_LEVELS = 2
N_BLOCKS = 1 << N_LEVELS
SCORE_BLOCK = SCAN_CHUNK // N_BLOCKS
TABLE_ROWS = 24
VMEM_LIMIT = 56 * 1024 * 1024

F32 = jnp.float32
BF16 = jnp.bfloat16


def _params(n_axes, vmem=VMEM_LIMIT):
    return pltpu.CompilerParams(dimension_semantics=("arbitrary",) * n_axes, vmem_limit_bytes=vmem)


def _sigmoid(x):
    return 1.0 / (1.0 + jnp.exp(-x))


def _gelu_tanh(x):
    half = 0.5 * x
    return half + half * jnp.tanh(x * (0.7978845608028654 + (0.7978845608028654 * 0.044715) * (x * x)))


def _rms_scale(x):
    return lax.rsqrt(jnp.mean(x * x, axis=-1, keepdims=True) + EPS)


def _adaln_kernel(cond_t_ref, w_ref, b_ref, w_in_ref, out_ref, w_in_bf_ref, s_ref):
    w_in_bf_ref[...] = w_in_ref[...].astype(w_in_bf_ref.dtype)

    @pl.when(pl.program_id(0) == 0)
    def _():
        ct = cond_t_ref[...]
        s = ct * _sigmoid(ct)
        for r in range(2):
            s_ref[r] = jnp.broadcast_to(s[:, r:r + 1], s_ref.shape[1:])

    d, tn = w_ref.shape
    rows_per_step = 8

    def body(i, acc):
        rows = pl.ds(pl.multiple_of(i * rows_per_step, rows_per_step), rows_per_step)
        w = w_ref[rows, :]
        return tuple(a + w * jnp.tile(s_ref[r, rows, :], (1, tn // LANES)) for r, a in enumerate(acc))

    zeros = jnp.zeros((rows_per_step, tn), F32)
    acc = lax.fori_loop(0, d // rows_per_step, body, (zeros, zeros), unroll=8)
    out_ref[...] = jnp.concatenate([jnp.sum(a, axis=0, keepdims=True) for a in acc], axis=0) + b_ref[...]


def _adaln(cond_t, w_ada, b_ada, w_in):
    d, n = w_ada.shape
    n_steps = 16
    tn, slab = n // n_steps, w_in.shape[0] // n_steps
    return pl.pallas_call(
        _adaln_kernel,
        grid=(n_steps,),
        in_specs=[pl.BlockSpec((d, 2), lambda j: (0, 0)),
                  pl.BlockSpec((d, tn), lambda j: (0, j)),
                  pl.BlockSpec((1, tn), lambda j: (0, j)),
                  pl.BlockSpec((slab, w_in.shape[1]), lambda j: (j, 0))],
        out_specs=[pl.BlockSpec((2, tn), lambda j: (0, j)),
                   pl.BlockSpec((slab, w_in.shape[1]), lambda j: (j, 0))],
        out_shape=[jax.ShapeDtypeStruct((2, n), F32), jax.ShapeDtypeStruct(w_in.shape, BF16)],
        scratch_shapes=[pltpu.VMEM((2, d, LANES), F32)],
        compiler_params=_params(1),
        name="adaln",
    )(cond_t, w_ada, b_ada, w_in)


def _lower_bounds(lbl_ref):
    l0, l1 = lbl_ref[0], lbl_ref[1]
    m = jnp.maximum(l0, l1)
    e0, e1 = jnp.exp(l0 - m), jnp.exp(l1 - m)
    return e1 / (e0 + e1)


def _visible(backward):
    ti = lax.broadcasted_iota(jnp.int32, (SCAN_CHUNK, SCAN_CHUNK), 0)
    si = lax.broadcasted_iota(jnp.int32, (SCAN_CHUNK, SCAN_CHUNK), 1)
    return (si >= ti) if backward else (si <= ti)


def _gate_terms(z, lb):
    f = lb + (1.0 - lb) * _sigmoid(z)
    lf = jnp.log(f)
    hi = lf.astype(BF16)
    lo = (lf - hi.astype(F32)).astype(BF16)
    return 1.0 - f, hi, lo


def _chunk_log_decay(gate, backward):
    _, hi, lo = gate
    tri = _visible(backward).astype(BF16)
    c = SCAN_CHUNK
    return [jnp.dot(tri, hi[j * c:(j + 1) * c], preferred_element_type=F32)
            + jnp.dot(tri, lo[j * c:(j + 1) * c], preferred_element_type=F32)
            for j in range(hi.shape[0] // c)]


N_OPERANDS = 2


def _scan_block(r, backward):
    return N_BLOCKS - 1 - r if backward else r


def _is_later(i, level):
    return (i >> (level - 1)) & 1 == 1


def _level_edge(i, level):
    group = i >> level
    return (group * (1 << level) + (1 << (level - 1))) * SCORE_BLOCK - 1


def _emit_scan_operands(q, gate, bs, backward, out_ref, table_ref):
    k = gate[0]
    w = k.shape[1]
    row_of = lambda pos: SCAN_CHUNK - 1 - pos if backward else pos
    per_block = lambda vals: jnp.concatenate(
        [jnp.broadcast_to(vals[_scan_block(r, backward)], (SCORE_BLOCK, w)) for r in range(N_BLOCKS)], axis=0)
    for j, b in enumerate(bs):
        rs = slice(j * SCAN_CHUNK, (j + 1) * SCAN_CHUNK)
        at = lambda pos: b[row_of(pos):row_of(pos) + 1, :]
        b_last = at(SCAN_CHUNK - 1)
        if q is None:
            out_ref[rs, :] = (k[rs] * jnp.exp(b_last - b)).astype(BF16)
            table_ref[j] = jnp.broadcast_to(jnp.exp(b_last), table_ref.shape[1:])
            continue
        centre = [0.5 * (at(i * SCORE_BLOCK) + at((i + 1) * SCORE_BLOCK - 1)) for i in range(N_BLOCKS)]
        centre_rows = per_block(centre)
        q_c = q[rs] * jnp.exp(b - centre_rows)
        k_c = k[rs] * jnp.exp(centre_rows - b)
        out_ref[rs, 0:w] = q_c.astype(BF16)
        out_ref[rs, w:2 * w] = k_c.astype(BF16)
        rows = [jnp.exp(b_last)]
        for level in range(1, N_LEVELS + 1):
            to_edge = []
            for i in range(N_BLOCKS):
                edge = at(_level_edge(i, level))
                to_edge.append(jnp.exp(centre[i] - edge) if _is_later(i, level) else jnp.exp(edge - centre[i]))
            rows.extend(to_edge[_scan_block(r, backward)] for r in range(N_BLOCKS))
        rows.extend(jnp.exp(centre[_scan_block(r, backward)]) for r in range(N_BLOCKS))
        rows.extend(jnp.exp(b_last - centre[_scan_block(r, backward)]) for r in range(N_BLOCKS))
        rows.append(jnp.zeros((TABLE_ROWS - len(rows), w), F32))
        table_ref[j] = jnp.concatenate(rows, axis=0)


def _scan_operands(q, z, lb, backward, out_ref, decay_ref):
    gate = _gate_terms(z, lb)
    _emit_scan_operands(q, gate, _chunk_log_decay(gate, backward), backward, out_ref, decay_ref)


def _normed_input(x, gain_ref, sh_ref, sc_ref, row):
    scale = gain_ref[...] * (1.0 + sc_ref[row:row + 1, :])
    return (x * _rms_scale(x) * scale + sh_ref[row:row + 1, :]).astype(BF16)


N_CASTS = 3


def _inproj_raster_kernel(x_ref, sh_ref, sc_ref, gain_ref, wuv_ref, wg_ref, wq_ref, wi_ref, wf_ref, wb_ref,
                          ws_ref, bs_ref, vg_ref, lbl_ref, *rest):
    cast_ins, rest = rest[:N_CASTS], rest[N_CASTS:]
    a_ref, sg_ref, v_ref = rest[:3]
    fwd_refs, bwd_refs, cast_outs = rest[3:5], rest[5:7], rest[7:]
    hb = _normed_input(x_ref[...], gain_ref, sh_ref, sc_ref, 0)
    proj = lambda w_ref: jnp.dot(hb, w_ref[...], preferred_element_type=F32)

    q = proj(wq_ref)
    zf = proj(wf_ref)
    zb = proj(wb_ref)
    lb = _lower_bounds(lbl_ref)
    gate_f = _gate_terms(zf, lb[0:1, :HALF_W])
    u = jnp.dot(hb, wuv_ref[:, :GROUP_W], preferred_element_type=F32)
    b_f = _chunk_log_decay(gate_f, False)
    gate_b = _gate_terms(zb, lb[1:2, :HALF_W])
    v = jnp.dot(hb, wuv_ref[:, GROUP_W:], preferred_element_type=F32)
    _emit_scan_operands(q, gate_f, b_f, False, *fwd_refs)
    b_b = _chunk_log_decay(gate_b, True)
    g = proj(wg_ref)
    _emit_scan_operands(q, gate_b, b_b, True, *bwd_refs)
    for w_ref, wb16_ref in zip(cast_ins, cast_outs):
        wb16_ref[...] = w_ref[...].astype(wb16_ref.dtype)
    u = _gelu_tanh(u)
    v = _gelu_tanh(v)

    chunks = [slice(c * GMLP_CHUNK, (c + 1) * GMLP_CHUNK) for c in range(u.shape[0] // GMLP_CHUNK)]
    for h in range(HEADS):
        cs = slice(h * HEAD_DIM, (h + 1) * HEAD_DIM)
        vh = v[:, cs]
        vn = (vh * _rms_scale(vh) * vg_ref[h:h + 1, :]).astype(BF16)
        mixed = jnp.dot(ws_ref[h].astype(BF16), jnp.concatenate([vn[rs] for rs in chunks], axis=1),
                        preferred_element_type=F32)
        for c, rs in enumerate(chunks):
            a_ref[rs, cs] = (u[rs, cs] * (mixed[:, c * HEAD_DIM:(c + 1) * HEAD_DIM] + bs_ref[h])).astype(a_ref.dtype)

    vi = proj(wi_ref)
    sg_ref[...] = (g * _sigmoid(g)).astype(sg_ref.dtype)
    v_ref[...] = vi.astype(v_ref.dtype)


def _scan_operand_outputs(t, tm, width, index_map, decay_map):
    data = jax.ShapeDtypeStruct((t, N_OPERANDS * width), BF16)
    decay = jax.ShapeDtypeStruct((t // SCAN_CHUNK, TABLE_ROWS,width), F32)
    data_spec = pl.BlockSpec((tm, N_OPERANDS * width), index_map)
    decay_spec = pl.BlockSpec((tm // SCAN_CHUNK, TABLE_ROWS,width), decay_map)
    return [data, decay], [data_spec, decay_spec]


def _inproj_raster(x, mod, gain, w_bf, w_s, b_s_b, v_gain, lb_logits, casts):
    t, d = x.shape
    tm = 256
    n_steps = t // tm
    row = lambda i: (i, 0)
    const2 = lambda i: (0, 0)
    const3 = lambda i: (0, 0, 0)
    full = pl.BlockSpec((tm, GROUP_W), row)
    half = pl.BlockSpec((tm, HALF_W), row)
    resident = lambda width, blk: pl.BlockSpec((d, width), lambda i: (0, blk), pipeline_mode=pl.Buffered(1))
    cast_specs = [pl.BlockSpec((w.shape[0] // n_steps, w.shape[1]), row) for w in casts]
    op_shapes, op_specs = _scan_operand_outputs(t, tm, HALF_W, row, lambda i: (i, 0, 0))
    return pl.pallas_call(
        _inproj_raster_kernel,
        grid=(n_steps,),
        in_specs=[pl.BlockSpec((tm, d), row),
                  pl.BlockSpec((2, d), lambda i: (0, 0)),
                  pl.BlockSpec((2, d), lambda i: (0, 1)),
                  pl.BlockSpec((1, d), const2),
                  resident(2 * GROUP_W, 0), resident(GROUP_W, 6),
                  resident(HALF_W, 4), resident(HALF_W, 6), resident(HALF_W, 8), resident(HALF_W, 10),
                  pl.BlockSpec(w_s.shape, const3),
                  pl.BlockSpec(b_s_b.shape, const3),
                  pl.BlockSpec(v_gain.shape, const2),
                  pl.BlockSpec(lb_logits.shape, const3)] + cast_specs,
        out_specs=[full, full, half] + op_specs + op_specs + cast_specs,
        out_shape=([jax.ShapeDtypeStruct((t, GROUP_W), BF16)] * 2 + [jax.ShapeDtypeStruct((t, HALF_W), BF16)]
                   + op_shapes + op_shapes + [jax.ShapeDtypeStruct(w.shape, BF16) for w in casts]),
        compiler_params=_params(1),
        name="inproj_raster",
    )(x, mod, mod, gain, *([w_bf] * 6), w_s, b_s_b, v_gain, lb_logits, *casts)


COLS_PER_STEP = 4


def _gather_copies(x_hbm_ref, buf_ref, sem_ref, slot, step):
    rows = x_hbm_ref.shape[0]
    return [pltpu.make_async_copy(x_hbm_ref.at[:, step * COLS_PER_STEP + wi, :],
                                  buf_ref.at[slot, pl.ds(wi * rows, rows), :],
                                  sem_ref.at[slot, wi])
            for wi in range(COLS_PER_STEP)]


def _inproj_colmajor_kernel(x_hbm_ref, sh_ref, sc_ref, gain_ref, wq_ref, wi_ref, wf_ref, wb_ref, lbl_ref,
                            v_ref, *rest):
    fwd_refs, bwd_refs, (xbuf_ref, sem_ref) = rest[:2], rest[2:4], rest[4:]
    step, n_steps = pl.program_id(0), pl.num_programs(0)
    slot = step % 2
    copies = functools.partial(_gather_copies, x_hbm_ref, xbuf_ref, sem_ref)

    @pl.when(step == 0)
    def _():
        for cp in copies(slot, step):
            cp.start()

    @pl.when(step + 1 < n_steps)
    def _():
        for cp in copies(1 - slot, step + 1):
            cp.start()

    for cp in copies(slot, step):
        cp.wait()
    hb = _normed_input(xbuf_ref[slot], gain_ref, sh_ref, sc_ref, 0)
    proj = lambda w_ref: jnp.dot(hb, w_ref[...], preferred_element_type=F32)
    zf = proj(wf_ref)
    zb = proj(wb_ref)
    lb = _lower_bounds(lbl_ref)
    gate_f = _gate_terms(zf, lb[0:1, HALF_W:])
    q = proj(wq_ref)
    b_f = _chunk_log_decay(gate_f, False)
    gate_b = _gate_terms(zb, lb[1:2, HALF_W:])
    vi = proj(wi_ref)
    _emit_scan_operands(q, gate_f, b_f, False, *fwd_refs)
    b_b = _chunk_log_decay(gate_b, True)
    _emit_scan_operands(q, gate_b, b_b, True, *bwd_refs)
    v_ref[...] = vi.astype(v_ref.dtype)


def _inproj_colmajor(x, mod, gain, w_bf, lb_logits):
    t, d = x.shape
    rows = t // GRID_W
    tm = COLS_PER_STEP * rows
    row = lambda i: (i, 0)
    const2 = lambda i: (0, 0)
    w_spec = lambda g: pl.BlockSpec((d, HALF_W), lambda i: (0, 2 * g + 1), pipeline_mode=pl.Buffered(1))
    op_shapes, op_specs = _scan_operand_outputs(t, tm, HALF_W, row, lambda i: (i, 0, 0))
    return pl.pallas_call(
        _inproj_colmajor_kernel,
        grid=(GRID_W // COLS_PER_STEP,),
        in_specs=[pl.BlockSpec(memory_space=pl.ANY),
                  pl.BlockSpec((2, d), lambda i: (0, 0)),
                  pl.BlockSpec((2, d), lambda i: (0, 1)),
                  pl.BlockSpec((1, d), const2),
                  w_spec(2), w_spec(3), w_spec(4), w_spec(5),
                  pl.BlockSpec(lb_logits.shape, lambda i: (0, 0, 0))],
        out_specs=[pl.BlockSpec((tm, HALF_W), row)] + op_specs + op_specs,
        out_shape=[jax.ShapeDtypeStruct((t, HALF_W), BF16)] + op_shapes + op_shapes,
        scratch_shapes=[pltpu.VMEM((2, tm, d), F32), pltpu.SemaphoreType.DMA((2, COLS_PER_STEP))],
        compiler_params=_params(1),
        name="inproj_colmajor",
    )(x.reshape(rows, GRID_W, d), mod, mod, gain, w_bf, w_bf, w_bf, w_bf, lb_logits)


def _inproj_ctx_kernel(x_ref, sh_ref, sc_ref, gain_ref, w_ref, lbl_ref,
                       v_ref, kf_ref, df_ref, kb_ref, db_ref, hb_ref):
    step = pl.program_id(0)

    @pl.when(step == 0)
    def _():
        hb_ref[...] = _normed_input(x_ref[...], gain_ref, sh_ref, sc_ref, 1)

    z = jnp.dot(hb_ref[...], w_ref[...], preferred_element_type=F32)

    @pl.when(step == 0)
    def _():
        v_ref[...] = z.astype(v_ref.dtype)

    @pl.when(step == 1)
    def _():
        _scan_operands(None, z, _lower_bounds(lbl_ref)[0:1, :], False, kf_ref, df_ref)

    @pl.when(step == 2)
    def _():
        _scan_operands(None, z, _lower_bounds(lbl_ref)[1:2, :], True, kb_ref, db_ref)


def _inproj_ctx(ctx, mod, gain, w_bf, lb_logits):
    t, d = ctx.shape
    const2 = lambda i: (0, 0)
    const3 = lambda i: (0, 0, 0)
    data = jax.ShapeDtypeStruct((t, GROUP_W), BF16)
    decay = jax.ShapeDtypeStruct((t // SCAN_CHUNK, TABLE_ROWS,GROUP_W), F32)
    data_spec = pl.BlockSpec(data.shape, const2)
    decay_spec = pl.BlockSpec(decay.shape, const3)
    return pl.pallas_call(
        _inproj_ctx_kernel,
        grid=(3,),
        in_specs=[pl.BlockSpec((t, d), const2),
                  pl.BlockSpec((2, d), lambda i: (0, 0)),
                  pl.BlockSpec((2, d), lambda i: (0, 1)),
                  pl.BlockSpec((1, d), const2),
                  pl.BlockSpec((d, GROUP_W), lambda i: (0, 3 + i)),
                  pl.BlockSpec(lb_logits.shape, const3)],
        out_specs=[data_spec, data_spec, decay_spec, data_spec, decay_spec],
        out_shape=[data, data, decay, data, decay],
        scratch_shapes=[pltpu.VMEM((t, d), BF16)],
        compiler_params=_params(1),
        name="inproj_ctx",
    )(ctx, mod, mod, gain, w_bf, lb_logits)


def _block_factor(table, first):
    return jnp.concatenate([jnp.broadcast_to(table[first + r:first + r + 1, :], (SCORE_BLOCK, HEAD_DIM))
                            for r in range(N_BLOCKS)], axis=0).astype(BF16)


def _intra_chunk_scores(q_c, k_c, table, backward):
    nt = (((1,), (1,)), ((), ()))
    blk = SCORE_BLOCK
    ti = lax.broadcasted_iota(jnp.int32, (SCAN_CHUNK, SCAN_CHUNK), 0)
    si = lax.broadcasted_iota(jnp.int32, (SCAN_CHUNK, SCAN_CHUNK), 1)
    if backward:
        ti, si = SCAN_CHUNK - 1 - ti, SCAN_CHUNK - 1 - si
    bt, bs = ti // blk, si // blk
    scores = jnp.where((bt == bs) & (si <= ti), lax.dot_general(q_c, k_c, nt, preferred_element_type=F32), 0.0)
    for level in range(1, N_LEVELS + 1):
        to_edge = _block_factor(table, 1 + (level - 1) * N_BLOCKS)
        sees = ((bs >> level == bt >> level) & ((bt >> (level - 1)) & 1 == 1) & ((bs >> (level - 1)) & 1 == 0))
        s_level = lax.dot_general(q_c * to_edge, k_c * to_edge, nt, preferred_element_type=F32)
        scores = jnp.where(sees, s_level, scores)
    return scores.astype(BF16)


CHUNKS_PER_STEP = 4


def _sub_chunk_block(backward, j, n_sub):
    return n_sub - 1 - j if backward else j


def _sub_chunk_rows(backward, j, n_sub):
    p = _sub_chunk_block(backward, j, n_sub)
    return slice(p * SCAN_CHUNK, (p + 1) * SCAN_CHUNK)


def _scan_chunks(groups, state_ref, o_refs):
    units = [(g, h, slice(h * HEAD_DIM, (h + 1) * HEAD_DIM)) for g in range(4) for h in range(HALF_HEADS)]
    with_outputs = o_refs is not None
    n_sub = groups[0][1].shape[0] // SCAN_CHUNK

    def operand(g, j, n, cs):
        return groups[g][0][_sub_chunk_rows(g >= 2, j, n_sub), n * HALF_W + cs.start:n * HALF_W + cs.stop]

    scores = {}
    if with_outputs:
        for j in range(n_sub):
            for g, h, cs in units:
                table = groups[g][2][_sub_chunk_block(g >= 2, j, n_sub), :, cs]
                scores[j, g, h] = _intra_chunk_scores(operand(g, j, 0, cs), operand(g, j, 1, cs), table, g >= 2)
    for j in range(n_sub):
        outs = {}
        for g, h, cs in units:
            ops_ref, v_ref, table_ref = groups[g]
            rows = _sub_chunk_rows(g >= 2, j, n_sub)
            u = g * HALF_HEADS + h
            st = state_ref[u]
            vh = v_ref[rows, cs]
            table = table_ref[_sub_chunk_block(g >= 2, j, n_sub), :, cs]
            if with_outputs:
                q_st = operand(g, j, 0, cs) * _block_factor(table, 1 + N_LEVELS * N_BLOCKS)
                k_st = operand(g, j, 1, cs) * _block_factor(table, 1 + (N_LEVELS + 1) * N_BLOCKS)
                outs[g, h] = jnp.dot(jnp.concatenate([scores[j, g, h], q_st], axis=1),
                                     jnp.concatenate([vh, st.astype(BF16)], axis=0), preferred_element_type=F32)
            else:
                k_st = ops_ref[rows, cs]
            decay = jnp.broadcast_to(table[0:1, :], (HEAD_DIM, HEAD_DIM)).T
            state_ref[u] = st * decay + lax.dot_general(
                k_st, vh, (((0,), (0,)), ((), ())), preferred_element_type=F32)
        if with_outputs:
            for g in range(4):
                o_ref = o_refs[g][j]
                o_ref[...] = jnp.concatenate([outs[g, h] for h in range(HALF_HEADS)], axis=1).astype(o_ref.dtype)


def _scatter_copy(buf_ref, out_hbm_ref, sem_ref, slot, j, chunk):
    per_col = out_hbm_ref.shape[0] // SCAN_CHUNK
    r0 = (chunk % per_col) * SCAN_CHUNK
    dst = out_hbm_ref.at[pl.ds(r0, SCAN_CHUNK), chunk // per_col, :]
    return pltpu.make_async_copy(buf_ref.at[slot, j], dst, sem_ref.at[slot, j])


def _scan_kernel(*refs, emit_outputs):
    groups = [refs[3 * g:3 * g + 3] for g in range(4)]
    s0_ref, rest = refs[12], refs[13:]
    step, n_steps = pl.program_id(0), pl.num_programs(0)
    state_ref = rest[4] if emit_outputs else rest[0]

    @pl.when(step == 0)
    def _():
        state_ref[...] = s0_ref[...]

    if not emit_outputs:
        _scan_chunks(groups, state_ref, None)
        return

    o_refs, (bufs, sems) = rest[:4], (rest[5:7], rest[7:9])
    slot = step % 2
    n_sub = bufs[0].shape[1]
    subs = range(n_sub)
    _scan_chunks(groups, state_ref,
                 [[o_refs[0].at[_sub_chunk_rows(False, j, n_sub), :] for j in subs], [bufs[0].at[slot, j] for j in subs],
                  [o_refs[2].at[_sub_chunk_rows(True, j, n_sub), :] for j in subs], [bufs[1].at[slot, j] for j in subs]])
    n_chunks = n_steps * n_sub
    for d, backward in enumerate((False, True)):
        chunk_of = ((lambda s, j: n_chunks - 1 - (s * n_sub + j)) if backward
                    else (lambda s, j: s * n_sub + j))
        copy = functools.partial(_scatter_copy, bufs[d], o_refs[2 * d + 1], sems[d])
        for j in subs:
            copy(slot, j, chunk_of(step, j)).start()

        @pl.when(step > 0)
        def _():
            for j in subs:
                copy(1 - slot, j, chunk_of(step - 1, j)).wait()

        @pl.when(step == n_steps - 1)
        def _():
            for j in subs:
                copy(slot, j, chunk_of(step, j)).wait()


def _scan(groups, s0, n_steps, n_sub, emit_outputs, grid_rows, name):
    blk = (n_sub * SCAN_CHUNK, HALF_W)
    args, in_specs = [], []
    for arrays, backward, col in groups:
        args.extend(arrays)
        block_of = (lambda s: n_steps - 1 - s) if backward else (lambda s: s)
        data_map = lambda s, block_of=block_of, col=col: (block_of(s), col)
        table_map = lambda s, block_of=block_of, col=col: (block_of(s), 0, col)
        in_specs.append(pl.BlockSpec((blk[0], N_OPERANDS * HALF_W), lambda s, block_of=block_of:
                                     (block_of(s), 0)) if emit_outputs else pl.BlockSpec(blk, data_map))
        in_specs.append(pl.BlockSpec(blk, data_map))
        in_specs.append(pl.BlockSpec((n_sub, TABLE_ROWS, HALF_W), table_map))
    args.append(s0)
    state_spec = pl.BlockSpec(s0.shape, lambda s: (0, 0, 0))
    in_specs.append(state_spec)
    out_specs, out_shape, scratch = [], [], []
    if emit_outputs:
        raster = jax.ShapeDtypeStruct((grid_rows * GRID_W, HALF_W), BF16)
        scattered = jax.ShapeDtypeStruct((grid_rows, GRID_W, HALF_W), F32)
        in_hbm = pl.BlockSpec(memory_space=pl.ANY)
        out_shape = [raster, scattered, raster, scattered]
        out_specs = [pl.BlockSpec(blk, lambda s: (s, 0)), in_hbm,
                     pl.BlockSpec(blk, lambda s: (n_steps - 1 - s, 0)), in_hbm]
        staging = pltpu.VMEM((2, n_sub, SCAN_CHUNK, HALF_W), F32)
        scratch = [staging] * 2 + [pltpu.SemaphoreType.DMA((2, n_sub))] * 2
    out_shape.append(jax.ShapeDtypeStruct(s0.shape, F32))
    out_specs.append(state_spec)
    return pl.pallas_call(
        functools.partial(_scan_kernel, emit_outputs=emit_outputs),
        grid=(n_steps,),
        in_specs=in_specs,
        out_specs=out_specs,
        out_shape=out_shape,
        scratch_shapes=scratch,
        compiler_params=_params(1),
        name=name,
    )(*args)


def _scan_ctx(v, kf, df, kb, db):
    n_chunks = v.shape[0] // SCAN_CHUNK
    groups = [((kf, v, df), False, 0), ((kf, v, df), False, 1), ((kb, v, db), True, 0), ((kb, v, db), True, 1)]
    s0 = jnp.zeros((2 * HEADS, HEAD_DIM, HEAD_DIM), F32)
    (state,) = _scan(groups, s0, 1, n_chunks, False, 0, "scan_ctx")
    return state


def _scan_latent(raster, colmajor, s0):
    t = raster[0].shape[0]
    n_steps = t // (CHUNKS_PER_STEP * SCAN_CHUNK)
    groups = []
    for d, backward in enumerate((False, True)):
        for v, *operands in (raster, colmajor):
            packed, table = operands[2 * d:2 * d + 2]
            groups.append(((packed, v, table), backward, 0))
    o_fr, o_fc, o_br, o_bc, _ = _scan(groups, s0, n_steps, CHUNKS_PER_STEP, True, t // GRID_W, "scan_latent")
    return o_fr, o_fc.reshape(t, HALF_W), o_br, o_bc.reshape(t, HALF_W)


def _outproj_kernel(a_ref, ofr_ref, ofc_ref, obr_ref, obc_ref, sg_ref, x_ref, w_ref,
                    g1_ref, sh2_ref, sc2_ref, og_ref, n2g_ref, x1_ref, h2_ref):
    def readout(rs):
        o_r = ofr_ref[rs, :].astype(F32) + obr_ref[rs, :].astype(F32)
        o_c = ofc_ref[rs, :] + obc_ref[rs, :]
        ys = []
        for h in range(HEADS):
            o = o_r if h < HALF_HEADS else o_c
            oh = o[:, (h % HALF_HEADS) * HEAD_DIM:(h % HALF_HEADS + 1) * HEAD_DIM]
            cs = slice(h * HEAD_DIM, (h + 1) * HEAD_DIM)
            ys.append((oh * _rms_scale(oh) * og_ref[:, cs] * sg_ref[rs, cs].astype(F32)).astype(BF16))
        return jnp.concatenate(ys, axis=1)

    def finish(rs, proj):
        x1 = x_ref[rs, :] + g1_ref[0:1, :] * proj
        x1_ref[rs, :] = x1
        scale = n2g_ref[...] * (1.0 + sc2_ref[0:1, :])
        h2_ref[rs, :] = (x1 * _rms_scale(x1) * scale + sh2_ref[0:1, :]).astype(h2_ref.dtype)

    half = x_ref.shape[0] // 2
    top, bottom = slice(0, half), slice(half, 2 * half)
    gmlp_side = lambda rs: jnp.dot(a_ref[rs, :], w_ref[:GROUP_W, :], preferred_element_type=F32)
    hgrn_side = lambda y: jnp.dot(y, w_ref[GROUP_W:, :], preferred_element_type=F32)
    p_top = gmlp_side(top)
    y_top = readout(top)
    p_bottom = gmlp_side(bottom)
    y_bottom = readout(bottom)
    p_top = p_top + hgrn_side(y_top)
    p_bottom = p_bottom + hgrn_side(y_bottom)
    finish(top, p_top)
    finish(bottom, p_bottom)


def _outproj(a, o_fr, o_fc, o_br, o_bc, sg, x, w_bf, mod, o_gain, n2_gain):
    t, d = x.shape
    tm = 512
    row = lambda i: (i, 0)
    const2 = lambda i: (0, 0)
    half = pl.BlockSpec((tm, HALF_W), row)
    full = pl.BlockSpec((tm, GROUP_W), row)
    mod_col = lambda c: pl.BlockSpec((2, d), lambda i: (0, c))
    return pl.pallas_call(
        _outproj_kernel,
        grid=(t // tm,),
        in_specs=[full, half, half, half, half, full,
                  pl.BlockSpec((tm, d), row),
                  pl.BlockSpec(w_bf.shape, const2, pipeline_mode=pl.Buffered(1)),
                  mod_col(2), mod_col(3), mod_col(4),
                  pl.BlockSpec((1, GROUP_W), const2),
                  pl.BlockSpec((1, d), const2)],
        out_specs=[pl.BlockSpec((tm, d), row), pl.BlockSpec((tm, d), row)],
        out_shape=[jax.ShapeDtypeStruct((t, d), F32), jax.ShapeDtypeStruct((t, d), BF16)],
        compiler_params=_params(1),
        name="outproj",
    )(a, o_fr, o_fc, o_br, o_bc, sg, x, w_bf, mod, mod, mod, o_gain, n2_gain)


def _ffn_kernel(h_ref, x1_ref, w1_ref, w2_ref, g2_ref, fg_ref, out_ref):
    f = pl.program_id(1)

    @pl.when(f == 0)
    def _():
        out_ref[...] = jnp.zeros_like(out_ref)

    hid = jnp.dot(h_ref[...], w1_ref[...], preferred_element_type=F32)
    hid = jnp.square(jnp.maximum(hid, 0.0)).astype(BF16)
    out_ref[...] += jnp.dot(hid, w2_ref[...], preferred_element_type=F32)

    @pl.when(f == pl.num_programs(1) - 1)
    def _():
        x2 = x1_ref[...] + g2_ref[0:1, :] * out_ref[...]
        out_ref[...] = x2 * _rms_scale(x2) * fg_ref[...]


def _ffn(h2, x1, w1_bf, w2_bf, mod, final_gain):
    t, d = x1.shape
    d_ff = w1_bf.shape[1]
    tm, tf = 1024, 1024
    row = lambda i, f: (i, 0)
    return pl.pallas_call(
        _ffn_kernel,
        grid=(t // tm, d_ff // tf),
        in_specs=[pl.BlockSpec((tm, d), row),
                  pl.BlockSpec((tm, d), row, pipeline_mode=pl.Buffered(1)),
                  pl.BlockSpec((d, tf), lambda i, f: (0, f)),
                  pl.BlockSpec((tf, d), lambda i, f: (f, 0)),
                  pl.BlockSpec((2, d), lambda i, f: (0, 5)),
                  pl.BlockSpec((1, d), lambda i, f: (0, 0))],
        out_specs=pl.BlockSpec((tm, d), row, pipeline_mode=pl.Buffered(1)),
        out_shape=jax.ShapeDtypeStruct((t, d), F32),
        compiler_params=_params(2),
        name="ffn",
    )(h2, x1, w1_bf, w2_bf, mod, final_gain)


def kernel(x, c, ctx, c_ctx, w_ada, b_ada, norm1_gain, w_in, gmlp_w_s, gmlp_b_s, gmlp_v_gain,
           hgrn_lb_logits, hgrn_o_gain, w_out, norm2_gain, w_ff1, w_ff2, final_gain):
    bsz, t, d = x.shape
    assert bsz == 1 and w_ada.shape[0] == 1, "single sample, single layer"
    assert w_in.shape[2] == 7 * GROUP_W and t % (GRID_W * SCAN_CHUNK) == 0

    cond_t = jnp.stack([c[0], c_ctx], axis=1)
    mod, w_in_bf = _adaln(cond_t, w_ada[0], b_ada, w_in[0])

    b_s_b = jnp.broadcast_to(gmlp_b_s[0][:, :, None], (HEADS, GMLP_CHUNK, HEAD_DIM))
    a, sg, *rest = _inproj_raster(x[0], mod, norm1_gain, w_in_bf, gmlp_w_s[0], b_s_b, gmlp_v_gain[0],
                                  hgrn_lb_logits, [w_out[0], w_ff1[0], w_ff2[0]])
    raster, (w_out_bf, w1_bf, w2_bf) = rest[:-N_CASTS], rest[-N_CASTS:]
    colmajor = _inproj_colmajor(x[0], mod, norm1_gain, w_in_bf, hgrn_lb_logits)
    ctx_ops = _inproj_ctx(ctx[0], mod, norm1_gain, w_in_bf, hgrn_lb_logits)

    s0 = _scan_ctx(*ctx_ops)
    o_fr, o_fc, o_br, o_bc = _scan_latent(raster, colmajor, s0)

    x1, h2 = _outproj(a, o_fr, o_fc, o_br, o_bc, sg, x[0], w_out_bf, mod,
                      hgrn_o_gain.reshape(1, GROUP_W), norm2_gain)
    out = _ffn(h2, x1, w1_bf, w2_bf, mod, final_gain.reshape(1, d))
    return out[None]
```

```python
import functools

import jax
import jax.numpy as jnp
from jax import lax
from jax.experimental import pallas as pl
from jax.experimental.pallas import tpu as pltpu

EPS = 1e-6
GRID_W = 64
HEADS = 8
HALF_HEADS = HEADS // 2
HEAD_DIM = 128
GROUP_W = HEADS * HEAD_DIM
HALF_W = HALF_HEADS * HEAD_DIM
GMLP_CHUNK = 128
SCAN_CHUNK = 128
LANES = 128
N_LEVELS = 2
N_BLOCKS = 1 << N_LEVELS
SCORE_BLOCK = SCAN_CHUNK // N_BLOCKS
TABLE_ROWS = 24
VMEM_LIMIT = 56 * 1024 * 1024

F32 = jnp.float32
BF16 = jnp.bfloat16


def _params(n_axes, vmem=VMEM_LIMIT):
    return pltpu.CompilerParams(dimension_semantics=("arbitrary",) * n_axes, vmem_limit_bytes=vmem)


def _sigmoid(x):
    return 1.0 / (1.0 + jnp.exp(-x))


def _gelu_tanh(x):
    half = 0.5 * x
    return half + half * jnp.tanh(x * (0.7978845608028654 + (0.7978845608028654 * 0.044715) * (x * x)))


def _rms_scale(x):
    return lax.rsqrt(jnp.mean(x * x, axis=-1, keepdims=True) + EPS)


def _adaln_kernel(cond_t_ref, w_ref, b_ref, w_in_ref, out_ref, w_in_bf_ref, s_ref):
    w_in_bf_ref[...] = w_in_ref[...].astype(w_in_bf_ref.dtype)

    @pl.when(pl.program_id(0) == 0)
    def _():
        ct = cond_t_ref[...]
        s = ct * _sigmoid(ct)
        for r in range(2):
            s_ref[r] = jnp.broadcast_to(s[:, r:r + 1], s_ref.shape[1:])

    d, tn = w_ref.shape
    rows_per_step = 8

    def body(i, acc):
        rows = pl.ds(pl.multiple_of(i * rows_per_step, rows_per_step), rows_per_step)
        w = w_ref[rows, :]
        return tuple(a + w * jnp.tile(s_ref[r, rows, :], (1, tn // LANES)) for r, a in enumerate(acc))

    zeros = jnp.zeros((rows_per_step, tn), F32)
    acc = lax.fori_loop(0, d // rows_per_step, body, (zeros, zeros), unroll=8)
    out_ref[...] = jnp.concatenate([jnp.sum(a, axis=0, keepdims=True) for a in acc], axis=0) + b_ref[...]


def _adaln(cond_t, w_ada, b_ada, w_in):
    d, n = w_ada.shape
    n_steps = 16
    tn, slab = n // n_steps, w_in.shape[0] // n_steps
    return pl.pallas_call(
        _adaln_kernel,
        grid=(n_steps,),
        in_specs=[pl.BlockSpec((d, 2), lambda j: (0, 0)),
                  pl.BlockSpec((d, tn), lambda j: (0, j)),
                  pl.BlockSpec((1, tn), lambda j: (0, j)),
                  pl.BlockSpec((slab, w_in.shape[1]), lambda j: (j, 0))],
        out_specs=[pl.BlockSpec((2, tn), lambda j: (0, j)),
                   pl.BlockSpec((slab, w_in.shape[1]), lambda j: (j, 0))],
        out_shape=[jax.ShapeDtypeStruct((2, n), F32), jax.ShapeDtypeStruct(w_in.shape, BF16)],
        scratch_shapes=[pltpu.VMEM((2, d, LANES), F32)],
        compiler_params=_params(1),
        name="adaln",
    )(cond_t, w_ada, b_ada, w_in)


def _lower_bounds(lbl_ref):
    l0, l1 = lbl_ref[0], lbl_ref[1]
    m = jnp.maximum(l0, l1)
    e0, e1 = jnp.exp(l0 - m), jnp.exp(l1 - m)
    return e1 / (e0 + e1)


def _visible(backward):
    ti = lax.broadcasted_iota(jnp.int32, (SCAN_CHUNK, SCAN_CHUNK), 0)
    si = lax.broadcasted_iota(jnp.int32, (SCAN_CHUNK, SCAN_CHUNK), 1)
    return (si >= ti) if backward else (si <= ti)


def _gate_terms(z, lb):
    f = lb + (1.0 - lb) * _sigmoid(z)
    lf = jnp.log(f)
    hi = lf.astype(BF16)
    lo = (lf - hi.astype(F32)).astype(BF16)
    return 1.0 - f, hi, lo


def _chunk_log_decay(gate, backward):
    _, hi, lo = gate
    tri = _visible(backward).astype(BF16)
    c = SCAN_CHUNK
    return [jnp.dot(tri, hi[j * c:(j + 1) * c], preferred_element_type=F32)
            + jnp.dot(tri, lo[j * c:(j + 1) * c], preferred_element_type=F32)
            for j in range(hi.shape[0] // c)]


N_OPERANDS = 2


def _scan_block(r, backward):
    return N_BLOCKS - 1 - r if backward else r


def _is_later(i, level):
    return (i >> (level - 1)) & 1 == 1


def _level_edge(i, level):
    group = i >> level
    return (group * (1 << level) + (1 << (level - 1))) * SCORE_BLOCK - 1


def _emit_scan_operands(q, gate, bs, backward, out_ref, table_ref):
    k = gate[0]
    w = k.shape[1]
    row_of = lambda pos: SCAN_CHUNK - 1 - pos if backward else pos
    per_block = lambda vals: jnp.concatenate(
        [jnp.broadcast_to(vals[_scan_block(r, backward)], (SCORE_BLOCK, w)) for r in range(N_BLOCKS)], axis=0)
    for j, b in enumerate(bs):
        rs = slice(j * SCAN_CHUNK, (j + 1) * SCAN_CHUNK)
        at = lambda pos: b[row_of(pos):row_of(pos) + 1, :]
        b_last = at(SCAN_CHUNK - 1)
        if q is None:
            out_ref[rs, :] = (k[rs] * jnp.exp(b_last - b)).astype(BF16)
            table_ref[j] = jnp.broadcast_to(jnp.exp(b_last), table_ref.shape[1:])
            continue
        centre = [0.5 * (at(i * SCORE_BLOCK) + at((i + 1) * SCORE_BLOCK - 1)) for i in range(N_BLOCKS)]
        centre_rows = per_block(centre)
        q_c = q[rs] * jnp.exp(b - centre_rows)
        k_c = k[rs] * jnp.exp(centre_rows - b)
        out_ref[rs, 0:w] = q_c.astype(BF16)
        out_ref[rs, w:2 * w] = k_c.astype(BF16)
        rows = [jnp.exp(b_last)]
        for level in range(1, N_LEVELS + 1):
            to_edge = []
            for i in range(N_BLOCKS):
                edge = at(_level_edge(i, level))
                to_edge.append(jnp.exp(centre[i] - edge) if _is_later(i, level) else jnp.exp(edge - centre[i]))
            rows.extend(to_edge[_scan_block(r, backward)] for r in range(N_BLOCKS))
        rows.extend(jnp.exp(centre[_scan_block(r, backward)]) for r in range(N_BLOCKS))
        rows.extend(jnp.exp(b_last - centre[_scan_block(r, backward)]) for r in range(N_BLOCKS))
        rows.append(jnp.zeros((TABLE_ROWS - len(rows), w), F32))
        table_ref[j] = jnp.concatenate(rows, axis=0)


def _scan_operands(q, z, lb, backward, out_ref, decay_ref):
    gate = _gate_terms(z, lb)
    _emit_scan_operands(q, gate, _chunk_log_decay(gate, backward), backward, out_ref, decay_ref)


def _normed_input(x, gain_ref, sh_ref, sc_ref, row):
    scale = gain_ref[...] * (1.0 + sc_ref[row:row + 1, :])
    return (x * _rms_scale(x) * scale + sh_ref[row:row + 1, :]).astype(BF16)


N_CASTS = 3


def _inproj_raster_kernel(x_ref, sh_ref, sc_ref, gain_ref, wuv_ref, wg_ref, wq_ref, wi_ref, wf_ref, wb_ref,
                          ws_ref, bs_ref, vg_ref, lbl_ref, *rest):
    cast_ins, rest = rest[:N_CASTS], rest[N_CASTS:]
    a_ref, sg_ref, v_ref = rest[:3]
    fwd_refs, bwd_refs, cast_outs = rest[3:5], rest[5:7], rest[7:]
    hb = _normed_input(x_ref[...], gain_ref, sh_ref, sc_ref, 0)
    proj = lambda w_ref: jnp.dot(hb, w_ref[...], preferred_element_type=F32)

    q = proj(wq_ref)
    zf = proj(wf_ref)
    zb = proj(wb_ref)
    lb = _lower_bounds(lbl_ref)
    gate_f = _gate_terms(zf, lb[0:1, :HALF_W])
    u = jnp.dot(hb, wuv_ref[:, :GROUP_W], preferred_element_type=F32)
    b_f = _chunk_log_decay(gate_f, False)
    gate_b = _gate_terms(zb, lb[1:2, :HALF_W])
    v = jnp.dot(hb, wuv_ref[:, GROUP_W:], preferred_element_type=F32)
    _emit_scan_operands(q, gate_f, b_f, False, *fwd_refs)
    b_b = _chunk_log_decay(gate_b, True)
    g = proj(wg_ref)
    _emit_scan_operands(q, gate_b, b_b, True, *bwd_refs)
    for w_ref, wb16_ref in zip(cast_ins, cast_outs):
        wb16_ref[...] = w_ref[...].astype(wb16_ref.dtype)
    u = _gelu_tanh(u)
    v = _gelu_tanh(v)

    chunks = [slice(c * GMLP_CHUNK, (c + 1) * GMLP_CHUNK) for c in range(u.shape[0] // GMLP_CHUNK)]
    for h in range(HEADS):
        cs = slice(h * HEAD_DIM, (h + 1) * HEAD_DIM)
        vh = v[:, cs]
        vn = (vh * _rms_scale(vh) * vg_ref[h:h + 1, :]).astype(BF16)
        mixed = jnp.dot(ws_ref[h].astype(BF16), jnp.concatenate([vn[rs] for rs in chunks], axis=1),
                        preferred_element_type=F32)
        for c, rs in enumerate(chunks):
            a_ref[rs, cs] = (u[rs, cs] * (mixed[:, c * HEAD_DIM:(c + 1) * HEAD_DIM] + bs_ref[h])).astype(a_ref.dtype)

    vi = proj(wi_ref)
    sg_ref[...] = (g * _sigmoid(g)).astype(sg_ref.dtype)
    v_ref[...] = vi.astype(v_ref.dtype)


def _scan_operand_outputs(t, tm, width, index_map, decay_map):
    data = jax.ShapeDtypeStruct((t, N_OPERANDS * width), BF16)
    decay = jax.ShapeDtypeStruct((t // SCAN_CHUNK, TABLE_ROWS,width), F32)
    data_spec = pl.BlockSpec((tm, N_OPERANDS * width), index_map)
    decay_spec = pl.BlockSpec((tm // SCAN_CHUNK, TABLE_ROWS,width), decay_map)
    return [data, decay], [data_spec, decay_spec]


def _inproj_raster(x, mod, gain, w_bf, w_s, b_s_b, v_gain, lb_logits, casts):
    t, d = x.shape
    tm = 256
    n_steps = t // tm
    row = lambda i: (i, 0)
    const2 = lambda i: (0, 0)
    const3 = lambda i: (0, 0, 0)
    full = pl.BlockSpec((tm, GROUP_W), row)
    half = pl.BlockSpec((tm, HALF_W), row)
    resident = lambda width, blk: pl.BlockSpec((d, width), lambda i: (0, blk), pipeline_mode=pl.Buffered(1))
    cast_specs = [pl.BlockSpec((w.shape[0] // n_steps, w.shape[1]), row) for w in casts]
    op_shapes, op_specs = _scan_operand_outputs(t, tm, HALF_W, row, lambda i: (i, 0, 0))
    return pl.pallas_call(
        _inproj_raster_kernel,
        grid=(n_steps,),
        in_specs=[pl.BlockSpec((tm, d), row),
                  pl.BlockSpec((2, d), lambda i: (0, 0)),
                  pl.BlockSpec((2, d), lambda i: (0, 1)),
                  pl.BlockSpec((1, d), const2),
                  resident(2 * GROUP_W, 0), resident(GROUP_W, 6),
                  resident(HALF_W, 4), resident(HALF_W, 6), resident(HALF_W, 8), resident(HALF_W, 10),
                  pl.BlockSpec(w_s.shape, const3),
                  pl.BlockSpec(b_s_b.shape, const3),
                  pl.BlockSpec(v_gain.shape, const2),
                  pl.BlockSpec(lb_logits.shape, const3)] + cast_specs,
        out_specs=[full, full, half] + op_specs + op_specs + cast_specs,
        out_shape=([jax.ShapeDtypeStruct((t, GROUP_W), BF16)] * 2 + [jax.ShapeDtypeStruct((t, HALF_W), BF16)]
                   + op_shapes + op_shapes + [jax.ShapeDtypeStruct(w.shape, BF16) for w in casts]),
        compiler_params=_params(1),
        name="inproj_raster",
    )(x, mod, mod, gain, *([w_bf] * 6), w_s, b_s_b, v_gain, lb_logits, *casts)


COLS_PER_STEP = 4


def _gather_copies(x_hbm_ref, buf_ref, sem_ref, slot, step):
    rows = x_hbm_ref.shape[0]
    return [pltpu.make_async_copy(x_hbm_ref.at[:, step * COLS_PER_STEP + wi, :],
                                  buf_ref.at[slot, pl.ds(wi * rows, rows), :],
                                  sem_ref.at[slot, wi])
            for wi in range(COLS_PER_STEP)]


def _inproj_colmajor_kernel(x_hbm_ref, sh_ref, sc_ref, gain_ref, wq_ref, wi_ref, wf_ref, wb_ref, lbl_ref,
                            v_ref, *rest):
    fwd_refs, bwd_refs, (xbuf_ref, sem_ref) = rest[:2], rest[2:4], rest[4:]
    step, n_steps = pl.program_id(0), pl.num_programs(0)
    slot = step % 2
    copies = functools.partial(_gather_copies, x_hbm_ref, xbuf_ref, sem_ref)

    @pl.when(step == 0)
    def _():
        for cp in copies(slot, step):
            cp.start()

    @pl.when(step + 1 < n_steps)
    def _():
        for cp in copies(1 - slot, step + 1):
            cp.start()

    for cp in copies(slot, step):
        cp.wait()
    hb = _normed_input(xbuf_ref[slot], gain_ref, sh_ref, sc_ref, 0)
    proj = lambda w_ref: jnp.dot(hb, w_ref[...], preferred_element_type=F32)
    zf = proj(wf_ref)
    zb = proj(wb_ref)
    lb = _lower_bounds(lbl_ref)
    gate_f = _gate_terms(zf, lb[0:1, HALF_W:])
    q = proj(wq_ref)
    b_f = _chunk_log_decay(gate_f, False)
    gate_b = _gate_terms(zb, lb[1:2, HALF_W:])
    vi = proj(wi_ref)
    _emit_scan_operands(q, gate_f, b_f, False, *fwd_refs)
    b_b = _chunk_log_decay(gate_b, True)
    _emit_scan_operands(q, gate_b, b_b, True, *bwd_refs)
    v_ref[...] = vi.astype(v_ref.dtype)


def _inproj_colmajor(x, mod, gain, w_bf, lb_logits):
    t, d = x.shape
    rows = t // GRID_W
    tm = COLS_PER_STEP * rows
    row = lambda i: (i, 0)
    const2 = lambda i: (0, 0)
    w_spec = lambda g: pl.BlockSpec((d, HALF_W), lambda i: (0, 2 * g + 1), pipeline_mode=pl.Buffered(1))
    op_shapes, op_specs = _scan_operand_outputs(t, tm, HALF_W, row, lambda i: (i, 0, 0))
    return pl.pallas_call(
        _inproj_colmajor_kernel,
        grid=(GRID_W // COLS_PER_STEP,),
        in_specs=[pl.BlockSpec(memory_space=pl.ANY),
                  pl.BlockSpec((2, d), lambda i: (0, 0)),
                  pl.BlockSpec((2, d), lambda i: (0, 1)),
                  pl.BlockSpec((1, d), const2),
                  w_spec(2), w_spec(3), w_spec(4), w_spec(5),
                  pl.BlockSpec(lb_logits.shape, lambda i: (0, 0, 0))],
        out_specs=[pl.BlockSpec((tm, HALF_W), row)] + op_specs + op_specs,
        out_shape=[jax.ShapeDtypeStruct((t, HALF_W), BF16)] + op_shapes + op_shapes,
        scratch_shapes=[pltpu.VMEM((2, tm, d), F32), pltpu.SemaphoreType.DMA((2, COLS_PER_STEP))],
        compiler_params=_params(1),
        name="inproj_colmajor",
    )(x.reshape(rows, GRID_W, d), mod, mod, gain, w_bf, w_bf, w_bf, w_bf, lb_logits)


def _inproj_ctx_kernel(x_ref, sh_ref, sc_ref, gain_ref, w_ref, lbl_ref,
                       v_ref, kf_ref, df_ref, kb_ref, db_ref, hb_ref):
    step = pl.program_id(0)

    @pl.when(step == 0)
    def _():
        hb_ref[...] = _normed_input(x_ref[...], gain_ref, sh_ref, sc_ref, 1)

    z = jnp.dot(hb_ref[...], w_ref[...], preferred_element_type=F32)

    @pl.when(step == 0)
    def _():
        v_ref[...] = z.astype(v_ref.dtype)

    @pl.when(step == 1)
    def _():
        _scan_operands(None, z, _lower_bounds(lbl_ref)[0:1, :], False, kf_ref, df_ref)

    @pl.when(step == 2)
    def _():
        _scan_operands(None, z, _lower_bounds(lbl_ref)[1:2, :], True, kb_ref, db_ref)


def _inproj_ctx(ctx, mod, gain, w_bf, lb_logits):
    t, d = ctx.shape
    const2 = lambda i: (0, 0)
    const3 = lambda i: (0, 0, 0)
    data = jax.ShapeDtypeStruct((t, GROUP_W), BF16)
    decay = jax.ShapeDtypeStruct((t // SCAN_CHUNK, TABLE_ROWS,GROUP_W), F32)
    data_spec = pl.BlockSpec(data.shape, const2)
    decay_spec = pl.BlockSpec(decay.shape, const3)
    return pl.pallas_call(
        _inproj_ctx_kernel,
        grid=(3,),
        in_specs=[pl.BlockSpec((t, d), const2),
                  pl.BlockSpec((2, d), lambda i: (0, 0)),
                  pl.BlockSpec((2, d), lambda i: (0, 1)),
                  pl.BlockSpec((1, d), const2),
                  pl.BlockSpec((d, GROUP_W), lambda i: (0, 3 + i)),
                  pl.BlockSpec(lb_logits.shape, const3)],
        out_specs=[data_spec, data_spec, decay_spec, data_spec, decay_spec],
        out_shape=[data, data, decay, data, decay],
        scratch_shapes=[pltpu.VMEM((t, d), BF16)],
        compiler_params=_params(1),
        name="inproj_ctx",
    )(ctx, mod, mod, gain, w_bf, lb_logits)


def _block_factor(table, first):
    return jnp.concatenate([jnp.broadcast_to(table[first + r:first + r + 1, :], (SCORE_BLOCK, HEAD_DIM))
                            for r in range(N_BLOCKS)], axis=0).astype(BF16)


def _intra_chunk_scores(q_c, k_c, table, backward):
    nt = (((1,), (1,)), ((), ()))
    blk = SCORE_BLOCK
    ti = lax.broadcasted_iota(jnp.int32, (SCAN_CHUNK, SCAN_CHUNK), 0)
    si = lax.broadcasted_iota(jnp.int32, (SCAN_CHUNK, SCAN_CHUNK), 1)
    if backward:
        ti, si = SCAN_CHUNK - 1 - ti, SCAN_CHUNK - 1 - si
    bt, bs = ti // blk, si // blk
    scores = jnp.where((bt == bs) & (si <= ti), lax.dot_general(q_c, k_c, nt, preferred_element_type=F32), 0.0)
    for level in range(1, N_LEVELS + 1):
        to_edge = _block_factor(table, 1 + (level - 1) * N_BLOCKS)
        sees = ((bs >> level == bt >> level) & ((bt >> (level - 1)) & 1 == 1) & ((bs >> (level - 1)) & 1 == 0))
        s_level = lax.dot_general(q_c * to_edge, k_c * to_edge, nt, preferred_element_type=F32)
        scores = jnp.where(sees, s_level, scores)
    return scores.astype(BF16)


CHUNKS_PER_STEP = 8


def _sub_chunk_block(backward, j, n_sub):
    return n_sub - 1 - j if backward else j


def _sub_chunk_rows(backward, j, n_sub):
    p = _sub_chunk_block(backward, j, n_sub)
    return slice(p * SCAN_CHUNK, (p + 1) * SCAN_CHUNK)


def _scan_chunks(groups, state_ref, o_refs):
    units = [(g, h, slice(h * HEAD_DIM, (h + 1) * HEAD_DIM)) for g in range(4) for h in range(HALF_HEADS)]
    with_outputs = o_refs is not None
    n_sub = groups[0][1].shape[0] // SCAN_CHUNK

    def operand(g, j, n, cs):
        return groups[g][0][_sub_chunk_rows(g >= 2, j, n_sub), n * HALF_W + cs.start:n * HALF_W + cs.stop]

    scores = {}
    if with_outputs:
        for j in range(n_sub):
            for g, h, cs in units:
                table = groups[g][2][_sub_chunk_block(g >= 2, j, n_sub), :, cs]
                scores[j, g, h] = _intra_chunk_scores(operand(g, j, 0, cs), operand(g, j, 1, cs), table, g >= 2)
    for j in range(n_sub):
        outs = {}
        for g, h, cs in units:
            ops_ref, v_ref, table_ref = groups[g]
            rows = _sub_chunk_rows(g >= 2, j, n_sub)
            u = g * HALF_HEADS + h
            st = state_ref[u]
            vh = v_ref[rows, cs]
            table = table_ref[_sub_chunk_block(g >= 2, j, n_sub), :, cs]
            if with_outputs:
                q_st = operand(g, j, 0, cs) * _block_factor(table, 1 + N_LEVELS * N_BLOCKS)
                k_st = operand(g, j, 1, cs) * _block_factor(table, 1 + (N_LEVELS + 1) * N_BLOCKS)
                outs[g, h] = jnp.dot(jnp.concatenate([scores[j, g, h], q_st], axis=1),
                                     jnp.concatenate([vh, st.astype(BF16)], axis=0), preferred_element_type=F32)
            else:
                k_st = ops_ref[rows, cs]
            decay = jnp.broadcast_to(table[0:1, :], (HEAD_DIM, HEAD_DIM)).T
            state_ref[u] = st * decay + lax.dot_general(
                k_st, vh, (((0,), (0,)), ((), ())), preferred_element_type=F32)
        if with_outputs:
            for g in range(4):
                o_ref = o_refs[g][j]
                o_ref[...] = jnp.concatenate([outs[g, h] for h in range(HALF_HEADS)], axis=1).astype(o_ref.dtype)


def _scatter_copy(buf_ref, out_hbm_ref, sem_ref, slot, j, chunk):
    per_col = out_hbm_ref.shape[0] // SCAN_CHUNK
    r0 = (chunk % per_col) * SCAN_CHUNK
    dst = out_hbm_ref.at[pl.ds(r0, SCAN_CHUNK), chunk // per_col, :]
    return pltpu.make_async_copy(buf_ref.at[slot, j], dst, sem_ref.at[slot, j])


def _scan_kernel(*refs, emit_outputs):
    groups = [refs[3 * g:3 * g + 3] for g in range(4)]
    s0_ref, rest = refs[12], refs[13:]
    step, n_steps = pl.program_id(0), pl.num_programs(0)
    state_ref = rest[4] if emit_outputs else rest[0]

    @pl.when(step == 0)
    def _():
        state_ref[...] = s0_ref[...]

    if not emit_outputs:
        _scan_chunks(groups, state_ref, None)
        return

    o_refs, (bufs, sems) = rest[:4], (rest[5:7], rest[7:9])
    slot = step % 2
    n_sub = bufs[0].shape[1]
    subs = range(n_sub)
    _scan_chunks(groups, state_ref,
                 [[o_refs[0].at[_sub_chunk_rows(False, j, n_sub), :] for j in subs], [bufs[0].at[slot, j] for j in subs],
                  [o_refs[2].at[_sub_chunk_rows(True, j, n_sub), :] for j in subs], [bufs[1].at[slot, j] for j in subs]])
    n_chunks = n_steps * n_sub
    for d, backward in enumerate((False, True)):
        chunk_of = ((lambda s, j: n_chunks - 1 - (s * n_sub + j)) if backward
                    else (lambda s, j: s * n_sub + j))
        copy = functools.partial(_scatter_copy, bufs[d], o_refs[2 * d + 1], sems[d])
        for j in subs:
            copy(slot, j, chunk_of(step, j)).start()

        @pl.when(step > 0)
        def _():
            for j in subs:
                copy(1 - slot, j, chunk_of(step - 1, j)).wait()

        @pl.when(step == n_steps - 1)
        def _():
            for j in subs:
                copy(slot, j, chunk_of(step, j)).wait()


def _scan(groups, s0, n_steps, n_sub, emit_outputs, grid_rows, name):
    blk = (n_sub * SCAN_CHUNK, HALF_W)
    args, in_specs = [], []
    for arrays, backward, col in groups:
        args.extend(arrays)
        block_of = (lambda s: n_steps - 1 - s) if backward else (lambda s: s)
        data_map = lambda s, block_of=block_of, col=col: (block_of(s), col)
        table_map = lambda s, block_of=block_of, col=col: (block_of(s), 0, col)
        in_specs.append(pl.BlockSpec((blk[0], N_OPERANDS * HALF_W), lambda s, block_of=block_of:
                                     (block_of(s), 0)) if emit_outputs else pl.BlockSpec(blk, data_map))
        in_specs.append(pl.BlockSpec(blk, data_map))
        in_specs.append(pl.BlockSpec((n_sub, TABLE_ROWS, HALF_W), table_map))
    args.append(s0)
    state_spec = pl.BlockSpec(s0.shape, lambda s: (0, 0, 0))
    in_specs.append(state_spec)
    out_specs, out_shape, scratch = [], [], []
    if emit_outputs:
        raster = jax.ShapeDtypeStruct((grid_rows * GRID_W, HALF_W), BF16)
        scattered = jax.ShapeDtypeStruct((grid_rows, GRID_W, HALF_W), F32)
        in_hbm = pl.BlockSpec(memory_space=pl.ANY)
        out_shape = [raster, scattered, raster, scattered]
        out_specs = [pl.BlockSpec(blk, lambda s: (s, 0)), in_hbm,
                     pl.BlockSpec(blk, lambda s: (n_steps - 1 - s, 0)), in_hbm]
        staging = pltpu.VMEM((2, n_sub, SCAN_CHUNK, HALF_W), F32)
        scratch = [staging] * 2 + [pltpu.SemaphoreType.DMA((2, n_sub))] * 2
    out_shape.append(jax.ShapeDtypeStruct(s0.shape, F32))
    out_specs.append(state_spec)
    return pl.pallas_call(
        functools.partial(_scan_kernel, emit_outputs=emit_outputs),
        grid=(n_steps,),
        in_specs=in_specs,
        out_specs=out_specs,
        out_shape=out_shape,
        scratch_shapes=scratch,
        compiler_params=_params(1),
        name=name,
    )(*args)


def _scan_ctx(v, kf, df, kb, db):
    n_chunks = v.shape[0] // SCAN_CHUNK
    groups = [((kf, v, df), False, 0), ((kf, v, df), False, 1), ((kb, v, db), True, 0), ((kb, v, db), True, 1)]
    s0 = jnp.zeros((2 * HEADS, HEAD_DIM, HEAD_DIM), F32)
    (state,) = _scan(groups, s0, 1, n_chunks, False, 0, "scan_ctx")
    return state


def _scan_latent(raster, colmajor, s0):
    t = raster[0].shape[0]
    n_steps = t // (CHUNKS_PER_STEP * SCAN_CHUNK)
    groups = []
    for d, backward in enumerate((False, True)):
        for v, *operands in (raster, colmajor):
            packed, table = operands[2 * d:2 * d + 2]
            groups.append(((packed, v, table), backward, 0))
    o_fr, o_fc, o_br, o_bc, _ = _scan(groups, s0, n_steps, CHUNKS_PER_STEP, True, t // GRID_W, "scan_latent")
    return o_fr, o_fc.reshape(t, HALF_W), o_br, o_bc.reshape(t, HALF_W)


def _outproj_kernel(a_ref, ofr_ref, ofc_ref, obr_ref, obc_ref, sg_ref, x_ref, w_ref,
                    g1_ref, sh2_ref, sc2_ref, og_ref, n2g_ref, x1_ref, h2_ref):
    def readout(rs):
        o_r = ofr_ref[rs, :].astype(F32) + obr_ref[rs, :].astype(F32)
        o_c = ofc_ref[rs, :] + obc_ref[rs, :]
        ys = []
        for h in range(HEADS):
            o = o_r if h < HALF_HEADS else o_c
            oh = o[:, (h % HALF_HEADS) * HEAD_DIM:(h % HALF_HEADS + 1) * HEAD_DIM]
            cs = slice(h * HEAD_DIM, (h + 1) * HEAD_DIM)
            ys.append((oh * _rms_scale(oh) * og_ref[:, cs] * sg_ref[rs, cs].astype(F32)).astype(BF16))
        return jnp.concatenate(ys, axis=1)

    def finish(rs, proj):
        x1 = x_ref[rs, :] + g1_ref[0:1, :] * proj
        x1_ref[rs, :] = x1
        scale = n2g_ref[...] * (1.0 + sc2_ref[0:1, :])
        h2_ref[rs, :] = (x1 * _rms_scale(x1) * scale + sh2_ref[0:1, :]).astype(h2_ref.dtype)

    half = x_ref.shape[0] // 2
    top, bottom = slice(0, half), slice(half, 2 * half)
    gmlp_side = lambda rs: jnp.dot(a_ref[rs, :], w_ref[:GROUP_W, :], preferred_element_type=F32)
    hgrn_side = lambda y: jnp.dot(y, w_ref[GROUP_W:, :], preferred_element_type=F32)
    p_top = gmlp_side(top)
    y_top = readout(top)
    p_bottom = gmlp_side(bottom)
    y_bottom = readout(bottom)
    p_top = p_top + hgrn_side(y_top)
    p_bottom = p_bottom + hgrn_side(y_bottom)
    finish(top, p_top)
    finish(bottom, p_bottom)


def _outproj(a, o_fr, o_fc, o_br, o_bc, sg, x, w_bf, mod, o_gain, n2_gain):
    t, d = x.shape
    tm = 512
    row = lambda i: (i, 0)
    const2 = lambda i: (0, 0)
    half = pl.BlockSpec((tm, HALF_W), row)
    full = pl.BlockSpec((tm, GROUP_W), row)
    mod_col = lambda c: pl.BlockSpec((2, d), lambda i: (0, c))
    return pl.pallas_call(
        _outproj_kernel,
        grid=(t // tm,),
        in_specs=[full, half, half, half, half, full,
                  pl.BlockSpec((tm, d), row),
                  pl.BlockSpec(w_bf.shape, const2, pipeline_mode=pl.Buffered(1)),
                  mod_col(2), mod_col(3), mod_col(4),
                  pl.BlockSpec((1, GROUP_W), const2),
                  pl.BlockSpec((1, d), const2)],
        out_specs=[pl.BlockSpec((tm, d), row), pl.BlockSpec((tm, d), row)],
        out_shape=[jax.ShapeDtypeStruct((t, d), F32), jax.ShapeDtypeStruct((t, d), BF16)],
        compiler_params=_params(1),
        name="outproj",
    )(a, o_fr, o_fc, o_br, o_bc, sg, x, w_bf, mod, mod, mod, o_gain, n2_gain)


def _ffn_kernel(h_ref, x1_ref, w1_ref, w2_ref, g2_ref, fg_ref, out_ref):
    f = pl.program_id(1)

    @pl.when(f == 0)
    def _():
        out_ref[...] = jnp.zeros_like(out_ref)

    hid = jnp.dot(h_ref[...], w1_ref[...], preferred_element_type=F32)
    hid = jnp.square(jnp.maximum(hid, 0.0)).astype(BF16)
    out_ref[...] += jnp.dot(hid, w2_ref[...], preferred_element_type=F32)

    @pl.when(f == pl.num_programs(1) - 1)
    def _():
        x2 = x1_ref[...] + g2_ref[0:1, :] * out_ref[...]
        out_ref[...] = x2 * _rms_scale(x2) * fg_ref[...]


def _ffn(h2, x1, w1_bf, w2_bf, mod, final_gain):
    t, d = x1.shape
    d_ff = w1_bf.shape[1]
    tm, tf = 1024, 1024
    row = lambda i, f: (i, 0)
    return pl.pallas_call(
        _ffn_kernel,
        grid=(t // tm, d_ff // tf),
        in_specs=[pl.BlockSpec((tm, d), row),
                  pl.BlockSpec((tm, d), row, pipeline_mode=pl.Buffered(1)),
                  pl.BlockSpec((d, tf), lambda i, f: (0, f)),
                  pl.BlockSpec((tf, d), lambda i, f: (f, 0)),
                  pl.BlockSpec((2, d), lambda i, f: (0, 5)),
                  pl.BlockSpec((1, d), lambda i, f: (0, 0))],
        out_specs=pl.BlockSpec((tm, d), row, pipeline_mode=pl.Buffered(1)),
        out_shape=jax.ShapeDtypeStruct((t, d), F32),
        compiler_params=_params(2),
        name="ffn",
    )(h2, x1, w1_bf, w2_bf, mod, final_gain)


def kernel(x, c, ctx, c_ctx, w_ada, b_ada, norm1_gain, w_in, gmlp_w_s, gmlp_b_s, gmlp_v_gain,
           hgrn_lb_logits, hgrn_o_gain, w_out, norm2_gain, w_ff1, w_ff2, final_gain):
    bsz, t, d = x.shape
    assert bsz == 1 and w_ada.shape[0] == 1, "single sample, single layer"
    assert w_in.shape[2] == 7 * GROUP_W and t % (GRID_W * SCAN_CHUNK) == 0

    cond_t = jnp.stack([c[0], c_ctx], axis=1)
    mod, w_in_bf = _adaln(cond_t, w_ada[0], b_ada, w_in[0])

    b_s_b = jnp.broadcast_to(gmlp_b_s[0][:, :, None], (HEADS, GMLP_CHUNK, HEAD_DIM))
    a, sg, *rest = _inproj_raster(x[0], mod, norm1_gain, w_in_bf, gmlp_w_s[0], b_s_b, gmlp_v_gain[0],
                                  hgrn_lb_logits, [w_out[0], w_ff1[0], w_ff2[0]])
    raster, (w_out_bf, w1_bf, w2_bf) = rest[:-N_CASTS], rest[-N_CASTS:]
    colmajor = _inproj_colmajor(x[0], mod, norm1_gain, w_in_bf, hgrn_lb_logits)
    ctx_ops = _inproj_ctx(ctx[0], mod, norm1_gain, w_in_bf, hgrn_lb_logits)

    s0 = _scan_ctx(*ctx_ops)
    o_fr, o_fc, o_br, o_bc = _scan_latent(raster, colmajor, s0)

    x1, h2 = _outproj(a, o_fr, o_fc, o_br, o_bc, sg, x[0], w_out_bf, mod,
                      hgrn_o_gain.reshape(1, GROUP_W), norm2_gain)
    out = _ffn(h2, x1, w1_bf, w2_bf, mod, final_gain.reshape(1, d))
    return out[None]
```

```python
import functools

import jax
import jax.numpy as jnp
from jax import lax
from jax.experimental import pallas as pl
from jax.experimental.pallas import tpu as pltpu

EPS = 1e-6
GRID_W = 64
HEADS = 8
HALF_HEADS = HEADS // 2
HEAD_DIM = 128
GROUP_W = HEADS * HEAD_DIM
HALF_W = HALF_HEADS * HEAD_DIM
GMLP_CHUNK = 128
SCAN_CHUNK = 128
LANES = 128
N_LEVELS = 2
N_BLOCKS = 1 << N_LEVELS
SCORE_BLOCK = SCAN_CHUNK // N_BLOCKS
TABLE_ROWS = 24
VMEM_LIMIT = 56 * 1024 * 1024

F32 = jnp.float32
BF16 = jnp.bfloat16


def _params(n_axes, vmem=VMEM_LIMIT):
    return pltpu.CompilerParams(dimension_semantics=("arbitrary",) * n_axes, vmem_limit_bytes=vmem)


def _sigmoid(x):
    return 1.0 / (1.0 + jnp.exp(-x))


def _gelu_tanh(x):
    half = 0.5 * x
    return half + half * jnp.tanh(x * (0.7978845608028654 + (0.7978845608028654 * 0.044715) * (x * x)))


def _rms_scale(x):
    return lax.rsqrt(jnp.mean(x * x, axis=-1, keepdims=True) + EPS)


def _adaln_kernel(cond_t_ref, w_ref, b_ref, w_in_ref, out_ref, w_in_bf_ref, s_ref):
    w_in_bf_ref[...] = w_in_ref[...].astype(w_in_bf_ref.dtype)

    @pl.when(pl.program_id(0) == 0)
    def _():
        ct = cond_t_ref[...]
        s = ct * _sigmoid(ct)
        for r in range(2):
            s_ref[r] = jnp.broadcast_to(s[:, r:r + 1], s_ref.shape[1:])

    d, tn = w_ref.shape
    rows_per_step = 8

    def body(i, acc):
        rows = pl.ds(pl.multiple_of(i * rows_per_step, rows_per_step), rows_per_step)
        w = w_ref[rows, :]
        return tuple(a + w * jnp.tile(s_ref[r, rows, :], (1, tn // LANES)) for r, a in enumerate(acc))

    zeros = jnp.zeros((rows_per_step, tn), F32)
    acc = lax.fori_loop(0, d // rows_per_step, body, (zeros, zeros), unroll=8)
    out_ref[...] = jnp.concatenate([jnp.sum(a, axis=0, keepdims=True) for a in acc], axis=0) + b_ref[...]


def _adaln(cond_t, w_ada, b_ada, w_in):
    d, n = w_ada.shape
    n_steps = 16
    tn, slab = n // n_steps, w_in.shape[0] // n_steps
    return pl.pallas_call(
        _adaln_kernel,
        grid=(n_steps,),
        in_specs=[pl.BlockSpec((d, 2), lambda j: (0, 0)),
                  pl.BlockSpec((d, tn), lambda j: (0, j)),
                  pl.BlockSpec((1, tn), lambda j: (0, j)),
                  pl.BlockSpec((slab, w_in.shape[1]), lambda j: (j, 0))],
        out_specs=[pl.BlockSpec((2, tn), lambda j: (0, j)),
                   pl.BlockSpec((slab, w_in.shape[1]), lambda j: (j, 0))],
        out_shape=[jax.ShapeDtypeStruct((2, n), F32), jax.ShapeDtypeStruct(w_in.shape, BF16)],
        scratch_shapes=[pltpu.VMEM((2, d, LANES), F32)],
        compiler_params=_params(1),
        name="adaln",
    )(cond_t, w_ada, b_ada, w_in)


def _lower_bounds(lbl_ref):
    l0, l1 = lbl_ref[0], lbl_ref[1]
    m = jnp.maximum(l0, l1)
    e0, e1 = jnp.exp(l0 - m), jnp.exp(l1 - m)
    return e1 / (e0 + e1)


def _visible(backward):
    ti = lax.broadcasted_iota(jnp.int32, (SCAN_CHUNK, SCAN_CHUNK), 0)
    si = lax.broadcasted_iota(jnp.int32, (SCAN_CHUNK, SCAN_CHUNK), 1)
    return (si >= ti) if backward else (si <= ti)


def _gate_terms(z, lb):
    f = lb + (1.0 - lb) * _sigmoid(z)
    lf = jnp.log(f)
    hi = lf.astype(BF16)
    lo = (lf - hi.astype(F32)).astype(BF16)
    return 1.0 - f, hi, lo


def _chunk_log_decay(gate, backward):
    _, hi, lo = gate
    tri = _visible(backward).astype(BF16)
    c = SCAN_CHUNK
    return [jnp.dot(tri, hi[j * c:(j + 1) * c], preferred_element_type=F32)
            + jnp.dot(tri, lo[j * c:(j + 1) * c], preferred_element_type=F32)
            for j in range(hi.shape[0] // c)]


N_OPERANDS = 2


def _scan_block(r, backward):
    return N_BLOCKS - 1 - r if backward else r


def _is_later(i, level):
    return (i >> (level - 1)) & 1 == 1


def _level_edge(i, level):
    group = i >> level
    return (group * (1 << level) + (1 << (level - 1))) * SCORE_BLOCK - 1


def _emit_scan_operands(q, gate, bs, backward, out_ref, table_ref):
    k = gate[0]
    w = k.shape[1]
    row_of = lambda pos: SCAN_CHUNK - 1 - pos if backward else pos
    per_block = lambda vals: jnp.concatenate(
        [jnp.broadcast_to(vals[_scan_block(r, backward)], (SCORE_BLOCK, w)) for r in range(N_BLOCKS)], axis=0)
    for j, b in enumerate(bs):
        rs = slice(j * SCAN_CHUNK, (j + 1) * SCAN_CHUNK)
        at = lambda pos: b[row_of(pos):row_of(pos) + 1, :]
        b_last = at(SCAN_CHUNK - 1)
        if q is None:
            out_ref[rs, :] = (k[rs] * jnp.exp(b_last - b)).astype(BF16)
            table_ref[j] = jnp.broadcast_to(jnp.exp(b_last), table_ref.shape[1:])
            continue
        centre = [0.5 * (at(i * SCORE_BLOCK) + at((i + 1) * SCORE_BLOCK - 1)) for i in range(N_BLOCKS)]
        centre_rows = per_block(centre)
        q_c = q[rs] * jnp.exp(b - centre_rows)
        k_c = k[rs] * jnp.exp(centre_rows - b)
        out_ref[rs, 0:w] = q_c.astype(BF16)
        out_ref[rs, w:2 * w] = k_c.astype(BF16)
        rows = [jnp.exp(b_last)]
        for level in range(1, N_LEVELS + 1):
            to_edge = []
            for i in range(N_BLOCKS):
                edge = at(_level_edge(i, level))
                to_edge.append(jnp.exp(centre[i] - edge) if _is_later(i, level) else jnp.exp(edge - centre[i]))
            rows.extend(to_edge[_scan_block(r, backward)] for r in range(N_BLOCKS))
        rows.extend(jnp.exp(centre[_scan_block(r, backward)]) for r in range(N_BLOCKS))
        rows.extend(jnp.exp(b_last - centre[_scan_block(r, backward)]) for r in range(N_BLOCKS))
        rows.append(jnp.zeros((TABLE_ROWS - len(rows), w), F32))
        table_ref[j] = jnp.concatenate(rows, axis=0)


def _scan_operands(q, z, lb, backward, out_ref, decay_ref):
    gate = _gate_terms(z, lb)
    _emit_scan_operands(q, gate, _chunk_log_decay(gate, backward), backward, out_ref, decay_ref)


def _normed_input(x, gain_ref, sh_ref, sc_ref, row):
    scale = gain_ref[...] * (1.0 + sc_ref[row:row + 1, :])
    return (x * _rms_scale(x) * scale + sh_ref[row:row + 1, :]).astype(BF16)


N_CASTS = 3


def _inproj_raster_kernel(x_ref, sh_ref, sc_ref, gain_ref, wuv_ref, wg_ref, wq_ref, wi_ref, wf_ref, wb_ref,
                          ws_ref, bs_ref, vg_ref, lbl_ref, *rest):
    cast_ins, rest = rest[:N_CASTS], rest[N_CASTS:]
    a_ref, sg_ref, v_ref = rest[:3]
    fwd_refs, bwd_refs, cast_outs = rest[3:5], rest[5:7], rest[7:]
    hb = _normed_input(x_ref[...], gain_ref, sh_ref, sc_ref, 0)
    proj = lambda w_ref: jnp.dot(hb, w_ref[...], preferred_element_type=F32)

    q = proj(wq_ref)
    zf = proj(wf_ref)
    zb = proj(wb_ref)
    lb = _lower_bounds(lbl_ref)
    gate_f = _gate_terms(zf, lb[0:1, :HALF_W])
    u = jnp.dot(hb, wuv_ref[:, :GROUP_W], preferred_element_type=F32)
    b_f = _chunk_log_decay(gate_f, False)
    gate_b = _gate_terms(zb, lb[1:2, :HALF_W])
    v = jnp.dot(hb, wuv_ref[:, GROUP_W:], preferred_element_type=F32)
    _emit_scan_operands(q, gate_f, b_f, False, *fwd_refs)
    b_b = _chunk_log_decay(gate_b, True)
    g = proj(wg_ref)
    _emit_scan_operands(q, gate_b, b_b, True, *bwd_refs)
    for w_ref, wb16_ref in zip(cast_ins, cast_outs):
        wb16_ref[...] = w_ref[...].astype(wb16_ref.dtype)
    u = _gelu_tanh(u)
    v = _gelu_tanh(v)

    chunks = [slice(c * GMLP_CHUNK, (c + 1) * GMLP_CHUNK) for c in range(u.shape[0] // GMLP_CHUNK)]
    for h in range(HEADS):
        cs = slice(h * HEAD_DIM, (h + 1) * HEAD_DIM)
        vh = v[:, cs]
        vn = (vh * _rms_scale(vh) * vg_ref[h:h + 1, :]).astype(BF16)
        mixed = jnp.dot(ws_ref[h].astype(BF16), jnp.concatenate([vn[rs] for rs in chunks], axis=1),
                        preferred_element_type=F32)
        for c, rs in enumerate(chunks):
            a_ref[rs, cs] = (u[rs, cs] * (mixed[:, c * HEAD_DIM:(c + 1) * HEAD_DIM] + bs_ref[h])).astype(a_ref.dtype)

    vi = proj(wi_ref)
    sg_ref[...] = (g * _sigmoid(g)).astype(sg_ref.dtype)
    v_ref[...] = vi.astype(v_ref.dtype)


def _scan_operand_outputs(t, tm, width, index_map, decay_map):
    data = jax.ShapeDtypeStruct((t, N_OPERANDS * width), BF16)
    decay = jax.ShapeDtypeStruct((t // SCAN_CHUNK, TABLE_ROWS,width), F32)
    data_spec = pl.BlockSpec((tm, N_OPERANDS * width), index_map)
    decay_spec = pl.BlockSpec((tm // SCAN_CHUNK, TABLE_ROWS,width), decay_map)
    return [data, decay], [data_spec, decay_spec]


def _inproj_raster(x, mod, gain, w_bf, w_s, b_s_b, v_gain, lb_logits, casts):
    t, d = x.shape
    tm = 256
    n_steps = t // tm
    row = lambda i: (i, 0)
    const2 = lambda i: (0, 0)
    const3 = lambda i: (0, 0, 0)
    full = pl.BlockSpec((tm, GROUP_W), row)
    half = pl.BlockSpec((tm, HALF_W), row)
    resident = lambda width, blk: pl.BlockSpec((d, width), lambda i: (0, blk), pipeline_mode=pl.Buffered(1))
    cast_specs = [pl.BlockSpec((w.shape[0] // n_steps, w.shape[1]), row) for w in casts]
    op_shapes, op_specs = _scan_operand_outputs(t, tm, HALF_W, row, lambda i: (i, 0, 0))
    return pl.pallas_call(
        _inproj_raster_kernel,
        grid=(n_steps,),
        in_specs=[pl.BlockSpec((tm, d), row),
                  pl.BlockSpec((2, d), lambda i: (0, 0)),
                  pl.BlockSpec((2, d), lambda i: (0, 1)),
                  pl.BlockSpec((1, d), const2),
                  resident(2 * GROUP_W, 0), resident(GROUP_W, 6),
                  resident(HALF_W, 4), resident(HALF_W, 6), resident(HALF_W, 8), resident(HALF_W, 10),
                  pl.BlockSpec(w_s.shape, const3),
                  pl.BlockSpec(b_s_b.shape, const3),
                  pl.BlockSpec(v_gain.shape, const2),
                  pl.BlockSpec(lb_logits.shape, const3)] + cast_specs,
        out_specs=[full, full, half] + op_specs + op_specs + cast_specs,
        out_shape=([jax.ShapeDtypeStruct((t, GROUP_W), BF16)] * 2 + [jax.ShapeDtypeStruct((t, HALF_W), BF16)]
                   + op_shapes + op_shapes + [jax.ShapeDtypeStruct(w.shape, BF16) for w in casts]),
        compiler_params=_params(1),
        name="inproj_raster",
    )(x, mod, mod, gain, *([w_bf] * 6), w_s, b_s_b, v_gain, lb_logits, *casts)


COLS_PER_STEP = 4


def _gather_copies(x_hbm_ref, buf_ref, sem_ref, slot, step):
    rows = x_hbm_ref.shape[0]
    return [pltpu.make_async_copy(x_hbm_ref.at[:, step * COLS_PER_STEP + wi, :],
                                  buf_ref.at[slot, pl.ds(wi * rows, rows), :],
                                  sem_ref.at[slot, wi])
            for wi in range(COLS_PER_STEP)]


def _inproj_colmajor_kernel(x_hbm_ref, sh_ref, sc_ref, gain_ref, wq_ref, wi_ref, wf_ref, wb_ref, lbl_ref,
                            v_ref, *rest):
    fwd_refs, bwd_refs, (xbuf_ref, sem_ref) = rest[:2], rest[2:4], rest[4:]
    step, n_steps = pl.program_id(0), pl.num_programs(0)
    slot = step % 2
    copies = functools.partial(_gather_copies, x_hbm_ref, xbuf_ref, sem_ref)

    @pl.when(step == 0)
    def _():
        for cp in copies(slot, step):
            cp.start()

    @pl.when(step + 1 < n_steps)
    def _():
        for cp in copies(1 - slot, step + 1):
            cp.start()

    for cp in copies(slot, step):
        cp.wait()
    hb = _normed_input(xbuf_ref[slot], gain_ref, sh_ref, sc_ref, 0)
    proj = lambda w_ref: jnp.dot(hb, w_ref[...], preferred_element_type=F32)
    zf = proj(wf_ref)
    zb = proj(wb_ref)
    lb = _lower_bounds(lbl_ref)
    gate_f = _gate_terms(zf, lb[0:1, HALF_W:])
    q = proj(wq_ref)
    b_f = _chunk_log_decay(gate_f, False)
    gate_b = _gate_terms(zb, lb[1:2, HALF_W:])
    vi = proj(wi_ref)
    _emit_scan_operands(q, gate_f, b_f, False, *fwd_refs)
    b_b = _chunk_log_decay(gate_b, True)
    _emit_scan_operands(q, gate_b, b_b, True, *bwd_refs)
    v_ref[...] = vi.astype(v_ref.dtype)


def _inproj_colmajor(x, mod, gain, w_bf, lb_logits):
    t, d = x.shape
    rows = t // GRID_W
    tm = COLS_PER_STEP * rows
    row = lambda i: (i, 0)
    const2 = lambda i: (0, 0)
    w_spec = lambda g: pl.BlockSpec((d, HALF_W), lambda i: (0, 2 * g + 1), pipeline_mode=pl.Buffered(1))
    op_shapes, op_specs = _scan_operand_outputs(t, tm, HALF_W, row, lambda i: (i, 0, 0))
    return pl.pallas_call(
        _inproj_colmajor_kernel,
        grid=(GRID_W // COLS_PER_STEP,),
        in_specs=[pl.BlockSpec(memory_space=pl.ANY),
                  pl.BlockSpec((2, d), lambda i: (0, 0)),
                  pl.BlockSpec((2, d), lambda i: (0, 1)),
                  pl.BlockSpec((1, d), const2),
                  w_spec(2), w_spec(3), w_spec(4), w_spec(5),
                  pl.BlockSpec(lb_logits.shape, lambda i: (0, 0, 0))],
        out_specs=[pl.BlockSpec((tm, HALF_W), row)] + op_specs + op_specs,
        out_shape=[jax.ShapeDtypeStruct((t, HALF_W), BF16)] + op_shapes + op_shapes,
        scratch_shapes=[pltpu.VMEM((2, tm, d), F32), pltpu.SemaphoreType.DMA((2, COLS_PER_STEP))],
        compiler_params=_params(1),
        name="inproj_colmajor",
    )(x.reshape(rows, GRID_W, d), mod, mod, gain, w_bf, w_bf, w_bf, w_bf, lb_logits)


def _inproj_ctx_kernel(x_ref, sh_ref, sc_ref, gain_ref, w_ref, lbl_ref,
                       v_ref, kf_ref, df_ref, kb_ref, db_ref, hb_ref):
    step = pl.program_id(0)

    @pl.when(step == 0)
    def _():
        hb_ref[...] = _normed_input(x_ref[...], gain_ref, sh_ref, sc_ref, 1)

    z = jnp.dot(hb_ref[...], w_ref[...], preferred_element_type=F32)

    @pl.when(step == 0)
    def _():
        v_ref[...] = z.astype(v_ref.dtype)

    @pl.when(step == 1)
    def _():
        _scan_operands(None, z, _lower_bounds(lbl_ref)[0:1, :], False, kf_ref, df_ref)

    @pl.when(step == 2)
    def _():
        _scan_operands(None, z, _lower_bounds(lbl_ref)[1:2, :], True, kb_ref, db_ref)


def _inproj_ctx(ctx, mod, gain, w_bf, lb_logits):
    t, d = ctx.shape
    const2 = lambda i: (0, 0)
    const3 = lambda i: (0, 0, 0)
    data = jax.ShapeDtypeStruct((t, GROUP_W), BF16)
    decay = jax.ShapeDtypeStruct((t // SCAN_CHUNK, TABLE_ROWS,GROUP_W), F32)
    data_spec = pl.BlockSpec(data.shape, const2)
    decay_spec = pl.BlockSpec(decay.shape, const3)
    return pl.pallas_call(
        _inproj_ctx_kernel,
        grid=(3,),
        in_specs=[pl.BlockSpec((t, d), const2),
                  pl.BlockSpec((2, d), lambda i: (0, 0)),
                  pl.BlockSpec((2, d), lambda i: (0, 1)),
                  pl.BlockSpec((1, d), const2),
                  pl.BlockSpec((d, GROUP_W), lambda i: (0, 3 + i)),
                  pl.BlockSpec(lb_logits.shape, const3)],
        out_specs=[data_spec, data_spec, decay_spec, data_spec, decay_spec],
        out_shape=[data, data, decay, data, decay],
        scratch_shapes=[pltpu.VMEM((t, d), BF16)],
        compiler_params=_params(1),
        name="inproj_ctx",
    )(ctx, mod, mod, gain, w_bf, lb_logits)


def _block_factor(table, first):
    return jnp.concatenate([jnp.broadcast_to(table[first + r:first + r + 1, :], (SCORE_BLOCK, HEAD_DIM))
                            for r in range(N_BLOCKS)], axis=0).astype(BF16)


def _intra_chunk_scores(q_c, k_c, table, backward):
    nt = (((1,), (1,)), ((), ()))
    blk = SCORE_BLOCK
    ti = lax.broadcasted_iota(jnp.int32, (SCAN_CHUNK, SCAN_CHUNK), 0)
    si = lax.broadcasted_iota(jnp.int32, (SCAN_CHUNK, SCAN_CHUNK), 1)
    if backward:
        ti, si = SCAN_CHUNK - 1 - ti, SCAN_CHUNK - 1 - si
    bt, bs = ti // blk, si // blk
    scores = jnp.where((bt == bs) & (si <= ti), lax.dot_general(q_c, k_c, nt, preferred_element_type=F32), 0.0)
    for level in range(1, N_LEVELS + 1):
        to_edge = _block_factor(table, 1 + (level - 1) * N_BLOCKS)
        sees = ((bs >> level == bt >> level) & ((bt >> (level - 1)) & 1 == 1) & ((bs >> (level - 1)) & 1 == 0))
        s_level = lax.dot_general(q_c * to_edge, k_c * to_edge, nt, preferred_element_type=F32)
        scores = jnp.where(sees, s_level, scores)
    return scores.astype(BF16)


CHUNKS_PER_STEP = 4


def _sub_chunk_block(backward, j, n_sub):
    return n_sub - 1 - j if backward else j


def _sub_chunk_rows(backward, j, n_sub):
    p = _sub_chunk_block(backward, j, n_sub)
    return slice(p * SCAN_CHUNK, (p + 1) * SCAN_CHUNK)


def _scan_chunks(groups, state_ref, o_refs):
    units = [(g, h, slice(h * HEAD_DIM, (h + 1) * HEAD_DIM)) for g in range(4) for h in range(HALF_HEADS)]
    with_outputs = o_refs is not None
    n_sub = groups[0][1].shape[0] // SCAN_CHUNK

    def operand(g, j, n, cs):
        return groups[g][0][_sub_chunk_rows(g >= 2, j, n_sub), n * HALF_W + cs.start:n * HALF_W + cs.stop]

    scores = {}
    if with_outputs:
        for j in range(n_sub):
            for g, h, cs in units:
                table = groups[g][2][_sub_chunk_block(g >= 2, j, n_sub), :, cs]
                scores[j, g, h] = _intra_chunk_scores(operand(g, j, 0, cs), operand(g, j, 1, cs), table, g >= 2)
    for j in range(n_sub):
        outs = {}
        for g, h, cs in units:
            ops_ref, v_ref, table_ref = groups[g]
            rows = _sub_chunk_rows(g >= 2, j, n_sub)
            u = g * HALF_HEADS + h
            st = state_ref[u]
            vh = v_ref[rows, cs]
            table = table_ref[_sub_chunk_block(g >= 2, j, n_sub), :, cs]
            if with_outputs:
                q_st = operand(g, j, 0, cs) * _block_factor(table, 1 + N_LEVELS * N_BLOCKS)
                k_st = operand(g, j, 1, cs) * _block_factor(table, 1 + (N_LEVELS + 1) * N_BLOCKS)
                outs[g, h] = jnp.dot(jnp.concatenate([scores[j, g, h], q_st], axis=1),
                                     jnp.concatenate([vh, st.astype(BF16)], axis=0), preferred_element_type=F32)
            else:
                k_st = ops_ref[rows, cs]
            decay = jnp.broadcast_to(table[0:1, :], (HEAD_DIM, HEAD_DIM)).T
            state_ref[u] = st * decay + lax.dot_general(
                k_st, vh, (((0,), (0,)), ((), ())), preferred_element_type=F32)
        if with_outputs:
            for g in range(4):
                o_ref = o_refs[g][j]
                o_ref[...] = jnp.concatenate([outs[g, h] for h in range(HALF_HEADS)], axis=1).astype(o_ref.dtype)


def _scatter_copy(buf_ref, out_hbm_ref, sem_ref, slot, j, chunk):
    per_col = out_hbm_ref.shape[0] // SCAN_CHUNK
    r0 = (chunk % per_col) * SCAN_CHUNK
    dst = out_hbm_ref.at[pl.ds(r0, SCAN_CHUNK), chunk // per_col, :]
    return pltpu.make_async_copy(buf_ref.at[slot, j], dst, sem_ref.at[slot, j])


def _scan_kernel(*refs, emit_outputs):
    groups = [refs[3 * g:3 * g + 3] for g in range(4)]
    s0_ref, rest = refs[12], refs[13:]
    step, n_steps = pl.program_id(0), pl.num_programs(0)
    state_ref = rest[4] if emit_outputs else rest[0]

    @pl.when(step == 0)
    def _():
        state_ref[...] = s0_ref[...]

    if not emit_outputs:
        _scan_chunks(groups, state_ref, None)
        return

    o_refs, (bufs, sems) = rest[:4], (rest[5:7], rest[7:9])
    slot = step % 2
    n_sub = bufs[0].shape[1]
    subs = range(n_sub)
    _scan_chunks(groups, state_ref,
                 [[o_refs[0].at[_sub_chunk_rows(False, j, n_sub), :] for j in subs], [bufs[0].at[slot, j] for j in subs],
                  [o_refs[2].at[_sub_chunk_rows(True, j, n_sub), :] for j in subs], [bufs[1].at[slot, j] for j in subs]])
    n_chunks = n_steps * n_sub
    for d, backward in enumerate((False, True)):
        chunk_of = ((lambda s, j: n_chunks - 1 - (s * n_sub + j)) if backward
                    else (lambda s, j: s * n_sub + j))
        copy = functools.partial(_scatter_copy, bufs[d], o_refs[2 * d + 1], sems[d])
        for j in subs:
            copy(slot, j, chunk_of(step, j)).start()

        @pl.when(step > 0)
        def _():
            for j in subs:
                copy(1 - slot, j, chunk_of(step - 1, j)).wait()

        @pl.when(step == n_steps - 1)
        def _():
            for j in subs:
                copy(slot, j, chunk_of(step, j)).wait()


def _scan(groups, s0, n_steps, n_sub, emit_outputs, grid_rows, name):
    blk = (n_sub * SCAN_CHUNK, HALF_W)
    args, in_specs = [], []
    for arrays, backward, col in groups:
        args.extend(arrays)
        block_of = (lambda s: n_steps - 1 - s) if backward else (lambda s: s)
        data_map = lambda s, block_of=block_of, col=col: (block_of(s), col)
        table_map = lambda s, block_of=block_of, col=col: (block_of(s), 0, col)
        in_specs.append(pl.BlockSpec((blk[0], N_OPERANDS * HALF_W), lambda s, block_of=block_of:
                                     (block_of(s), 0)) if emit_outputs else pl.BlockSpec(blk, data_map))
        in_specs.append(pl.BlockSpec(blk, data_map))
        in_specs.append(pl.BlockSpec((n_sub, TABLE_ROWS, HALF_W), table_map))
    args.append(s0)
    state_spec = pl.BlockSpec(s0.shape, lambda s: (0, 0, 0))
    in_specs.append(state_spec)
    out_specs, out_shape, scratch = [], [], []
    if emit_outputs:
        raster = jax.ShapeDtypeStruct((grid_rows * GRID_W, HALF_W), BF16)
        scattered = jax.ShapeDtypeStruct((grid_rows, GRID_W, HALF_W), F32)
        in_hbm = pl.BlockSpec(memory_space=pl.ANY)
        out_shape = [raster, scattered, raster, scattered]
        out_specs = [pl.BlockSpec(blk, lambda s: (s, 0)), in_hbm,
                     pl.BlockSpec(blk, lambda s: (n_steps - 1 - s, 0)), in_hbm]
        staging = pltpu.VMEM((2, n_sub, SCAN_CHUNK, HALF_W), F32)
        scratch = [staging] * 2 + [pltpu.SemaphoreType.DMA((2, n_sub))] * 2
    out_shape.append(jax.ShapeDtypeStruct(s0.shape, F32))
    out_specs.append(state_spec)
    return pl.pallas_call(
        functools.partial(_scan_kernel, emit_outputs=emit_outputs),
        grid=(n_steps,),
        in_specs=in_specs,
        out_specs=out_specs,
        out_shape=out_shape,
        scratch_shapes=scratch,
        compiler_params=_params(1),
        name=name,
    )(*args)


def _scan_ctx(v, kf, df, kb, db):
    n_chunks = v.shape[0] // SCAN_CHUNK
    groups = [((kf, v, df), False, 0), ((kf, v, df), False, 1), ((kb, v, db), True, 0), ((kb, v, db), True, 1)]
    s0 = jnp.zeros((2 * HEADS, HEAD_DIM, HEAD_DIM), F32)
    (state,) = _scan(groups, s0, 1, n_chunks, False, 0, "scan_ctx")
    return state


def _scan_latent(raster, colmajor, s0):
    t = raster[0].shape[0]
    n_steps = t // (CHUNKS_PER_STEP * SCAN_CHUNK)
    groups = []
    for d, backward in enumerate((False, True)):
        for v, *operands in (raster, colmajor):
            packed, table = operands[2 * d:2 * d + 2]
            groups.append(((packed, v, table), backward, 0))
    o_fr, o_fc, o_br, o_bc, _ = _scan(groups, s0, n_steps, CHUNKS_PER_STEP, True, t // GRID_W, "scan_latent")
    return o_fr, o_fc.reshape(t, HALF_W), o_br, o_bc.reshape(t, HALF_W)


def _outproj_kernel(a_ref, ofr_ref, ofc_ref, obr_ref, obc_ref, sg_ref, x_ref, w_ref,
                    g1_ref, sh2_ref, sc2_ref, og_ref, n2g_ref, x1_ref, h2_ref):
    def readout(rs):
        o_r = ofr_ref[rs, :].astype(F32) + obr_ref[rs, :].astype(F32)
        o_c = ofc_ref[rs, :] + obc_ref[rs, :]
        ys = []
        for h in range(HEADS):
            o = o_r if h < HALF_HEADS else o_c
            oh = o[:, (h % HALF_HEADS) * HEAD_DIM:(h % HALF_HEADS + 1) * HEAD_DIM]
            cs = slice(h * HEAD_DIM, (h + 1) * HEAD_DIM)
            ys.append((oh * _rms_scale(oh) * og_ref[:, cs] * sg_ref[rs, cs].astype(F32)).astype(BF16))
        return jnp.concatenate(ys, axis=1)

    def finish(rs, proj):
        x1 = x_ref[rs, :] + g1_ref[0:1, :] * proj
        x1_ref[rs, :] = x1
        scale = n2g_ref[...] * (1.0 + sc2_ref[0:1, :])
        h2_ref[rs, :] = (x1 * _rms_scale(x1) * scale + sh2_ref[0:1, :]).astype(h2_ref.dtype)

    half = x_ref.shape[0] // 2
    top, bottom = slice(0, half), slice(half, 2 * half)
    gmlp_side = lambda rs: jnp.dot(a_ref[rs, :], w_ref[:GROUP_W, :], preferred_element_type=F32)
    hgrn_side = lambda y: jnp.dot(y, w_ref[GROUP_W:, :], preferred_element_type=F32)
    p_top = gmlp_side(top)
    y_top = readout(top)
    p_bottom = gmlp_side(bottom)
    y_bottom = readout(bottom)
    p_top = p_top + hgrn_side(y_top)
    p_bottom = p_bottom + hgrn_side(y_bottom)
    finish(top, p_top)
    finish(bottom, p_bottom)


def _outproj(a, o_fr, o_fc, o_br, o_bc, sg, x, w_bf, mod, o_gain, n2_gain):
    t, d = x.shape
    tm = 512
    row = lambda i: (i, 0)
    const2 = lambda i: (0, 0)
    half = pl.BlockSpec((tm, HALF_W), row)
    full = pl.BlockSpec((tm, GROUP_W), row)
    mod_col = lambda c: pl.BlockSpec((2, d), lambda i: (0, c))
    return pl.pallas_call(
        _outproj_kernel,
        grid=(t // tm,),
        in_specs=[full, half, half, half, half, full,
                  pl.BlockSpec((tm, d), row),
                  pl.BlockSpec(w_bf.shape, const2, pipeline_mode=pl.Buffered(1)),
                  mod_col(2), mod_col(3), mod_col(4),
                  pl.BlockSpec((1, GROUP_W), const2),
                  pl.BlockSpec((1, d), const2)],
        out_specs=[pl.BlockSpec((tm, d), row), pl.BlockSpec((tm, d), row)],
        out_shape=[jax.ShapeDtypeStruct((t, d), F32), jax.ShapeDtypeStruct((t, d), BF16)],
        compiler_params=_params(1),
        name="outproj",
    )(a, o_fr, o_fc, o_br, o_bc, sg, x, w_bf, mod, mod, mod, o_gain, n2_gain)


def _ffn_kernel(h_ref, x1_ref, w1_ref, w2_ref, g2_ref, fg_ref, out_ref):
    f = pl.program_id(1)

    @pl.when(f == 0)
    def _():
        out_ref[...] = jnp.zeros_like(out_ref)

    hid = jnp.dot(h_ref[...], w1_ref[...], preferred_element_type=F32)
    hid = jnp.square(jnp.maximum(hid, 0.0)).astype(BF16)
    out_ref[...] += jnp.dot(hid, w2_ref[...], preferred_element_type=F32)

    @pl.when(f == pl.num_programs(1) - 1)
    def _():
        x2 = x1_ref[...] + g2_ref[0:1, :] * out_ref[...]
        out_ref[...] = x2 * _rms_scale(x2) * fg_ref[...]


def _ffn(h2, x1, w1_bf, w2_bf, mod, final_gain):
    t, d = x1.shape
    d_ff = w1_bf.shape[1]
    tm, tf = 1024, 1024
    row = lambda i, f: (i, 0)
    return pl.pallas_call(
        _ffn_kernel,
        grid=(t // tm, d_ff // tf),
        in_specs=[pl.BlockSpec((tm, d), row),
                  pl.BlockSpec((tm, d), row, pipeline_mode=pl.Buffered(1)),
                  pl.BlockSpec((d, tf), lambda i, f: (0, f)),
                  pl.BlockSpec((tf, d), lambda i, f: (f, 0)),
                  pl.BlockSpec((2, d), lambda i, f: (0, 5)),
                  pl.BlockSpec((1, d), lambda i, f: (0, 0))],
        out_specs=pl.BlockSpec((tm, d), row, pipeline_mode=pl.Buffered(1)),
        out_shape=jax.ShapeDtypeStruct((t, d), F32),
        compiler_params=_params(2),
        name="ffn",
    )(h2, x1, w1_bf, w2_bf, mod, final_gain)


def kernel(x, c, ctx, c_ctx, w_ada, b_ada, norm1_gain, w_in, gmlp_w_s, gmlp_b_s, gmlp_v_gain,
           hgrn_lb_logits, hgrn_o_gain, w_out, norm2_gain, w_ff1, w_ff2, final_gain):
    bsz, t, d = x.shape
    assert bsz == 1 and w_ada.shape[0] == 1, "single sample, single layer"
    assert d == 2 * GROUP_W and w_in.shape[2] == 7 * GROUP_W, "head groups of 8 x 128 channels"
    assert t % (GRID_W * SCAN_CHUNK) == 0 and t % (CHUNKS_PER_STEP * SCAN_CHUNK) == 0, "latent grid shape"
    assert ctx.shape[1] % SCAN_CHUNK == 0, "context length must be whole scan chunks"

    cond_t = jnp.stack([c[0], c_ctx], axis=1)
    mod, w_in_bf = _adaln(cond_t, w_ada[0], b_ada, w_in[0])

    b_s_b = jnp.broadcast_to(gmlp_b_s[0][:, :, None], (HEADS, GMLP_CHUNK, HEAD_DIM))
    a, sg, *rest = _inproj_raster(x[0], mod, norm1_gain, w_in_bf, gmlp_w_s[0], b_s_b, gmlp_v_gain[0],
                                  hgrn_lb_logits, [w_out[0], w_ff1[0], w_ff2[0]])
    raster, (w_out_bf, w1_bf, w2_bf) = rest[:-N_CASTS], rest[-N_CASTS:]
    colmajor = _inproj_colmajor(x[0], mod, norm1_gain, w_in_bf, hgrn_lb_logits)
    ctx_ops = _inproj_ctx(ctx[0], mod, norm1_gain, w_in_bf, hgrn_lb_logits)

    s0 = _scan_ctx(*ctx_ops)
    o_fr, o_fc, o_br, o_bc = _scan_latent(raster, colmajor, s0)

    x1, h2 = _outproj(a, o_fr, o_fc, o_br, o_bc, sg, x[0], w_out_bf, mod,
                      hgrn_o_gain.reshape(1, GROUP_W), norm2_gain)
    out = _ffn(h2, x1, w1_bf, w2_bf, mod, final_gain.reshape(1, d))
    return out[None]
```

```python
import functools

import jax
import jax.numpy as jnp
from jax import lax
from jax.experimental import pallas as pl
from jax.experimental.pallas import tpu as pltpu

EPS = 1e-6
GRID_W = 64
HEADS = 8
HALF_HEADS = HEADS // 2
HEAD_DIM = 128
GROUP_W = HEADS * HEAD_DIM
HALF_W = HALF_HEADS * HEAD_DIM
GMLP_CHUNK = 128
SCAN_CHUNK = 128
LANES = 128
N_LEVELS = 2
N_BLOCKS = 1 << N_LEVELS
SCORE_BLOCK = SCAN_CHUNK // N_BLOCKS
TABLE_ROWS = 24
VMEM_LIMIT = 56 * 1024 * 1024

F32 = jnp.float32
BF16 = jnp.bfloat16


def _params(n_axes, vmem=VMEM_LIMIT):
    return pltpu.CompilerParams(dimension_semantics=("arbitrary",) * n_axes, vmem_limit_bytes=vmem)


def _sigmoid(x):
    return 1.0 / (1.0 + jnp.exp(-x))


def _gelu_tanh(x):
    half = 0.5 * x
    return half + half * jnp.tanh(x * (0.7978845608028654 + (0.7978845608028654 * 0.044715) * (x * x)))


def _rms_scale(x):
    return lax.rsqrt(jnp.mean(x * x, axis=-1, keepdims=True) + EPS)


def _adaln_kernel(cond_t_ref, w_ref, b_ref, w_in_ref, out_ref, w_in_bf_ref, s_ref):
    w_in_bf_ref[...] = w_in_ref[...].astype(w_in_bf_ref.dtype)

    @pl.when(pl.program_id(0) == 0)
    def _():
        ct = cond_t_ref[...]
        s = ct * _sigmoid(ct)
        for r in range(2):
            s_ref[r] = jnp.broadcast_to(s[:, r:r + 1], s_ref.shape[1:])

    d, tn = w_ref.shape
    rows_per_step = 8

    def body(i, acc):
        rows = pl.ds(pl.multiple_of(i * rows_per_step, rows_per_step), rows_per_step)
        w = w_ref[rows, :]
        return tuple(a + w * jnp.tile(s_ref[r, rows, :], (1, tn // LANES)) for r, a in enumerate(acc))

    zeros = jnp.zeros((rows_per_step, tn), F32)
    acc = lax.fori_loop(0, d // rows_per_step, body, (zeros, zeros), unroll=8)
    out_ref[...] = jnp.concatenate([jnp.sum(a, axis=0, keepdims=True) for a in acc], axis=0) + b_ref[...]


def _adaln(cond_t, w_ada, b_ada, w_in):
    d, n = w_ada.shape
    n_steps = 16
    tn, slab = n // n_steps, w_in.shape[0] // n_steps
    return pl.pallas_call(
        _adaln_kernel,
        grid=(n_steps,),
        in_specs=[pl.BlockSpec((d, 2), lambda j: (0, 0)),
                  pl.BlockSpec((d, tn), lambda j: (0, j)),
                  pl.BlockSpec((1, tn), lambda j: (0, j)),
                  pl.BlockSpec((slab, w_in.shape[1]), lambda j: (j, 0))],
        out_specs=[pl.BlockSpec((2, tn), lambda j: (0, j)),
                   pl.BlockSpec((slab, w_in.shape[1]), lambda j: (j, 0))],
        out_shape=[jax.ShapeDtypeStruct((2, n), F32), jax.ShapeDtypeStruct(w_in.shape, BF16)],
        scratch_shapes=[pltpu.VMEM((2, d, LANES), F32)],
        compiler_params=_params(1),
        name="adaln",
    )(cond_t, w_ada, b_ada, w_in)


def _lower_bounds(lbl_ref):
    l0, l1 = lbl_ref[0], lbl_ref[1]
    m = jnp.maximum(l0, l1)
    e0, e1 = jnp.exp(l0 - m), jnp.exp(l1 - m)
    return e1 / (e0 + e1)


def _visible(backward):
    ti = lax.broadcasted_iota(jnp.int32, (SCAN_CHUNK, SCAN_CHUNK), 0)
    si = lax.broadcasted_iota(jnp.int32, (SCAN_CHUNK, SCAN_CHUNK), 1)
    return (si >= ti) if backward else (si <= ti)


def _gate_terms(z, lb):
    f = lb + (1.0 - lb) * _sigmoid(z)
    lf = jnp.log(f)
    hi = lf.astype(BF16)
    lo = (lf - hi.astype(F32)).astype(BF16)
    return 1.0 - f, hi, lo


def _chunk_log_decay(gate, backward):
    _, hi, lo = gate
    tri = _visible(backward).astype(BF16)
    c = SCAN_CHUNK
    return [jnp.dot(tri, hi[j * c:(j + 1) * c], preferred_element_type=F32)
            + jnp.dot(tri, lo[j * c:(j + 1) * c], preferred_element_type=F32)
            for j in range(hi.shape[0] // c)]


N_OPERANDS = 2


def _scan_block(r, backward):
    return N_BLOCKS - 1 - r if backward else r


def _is_later(i, level):
    return (i >> (level - 1)) & 1 == 1


def _level_edge(i, level):
    group = i >> level
    return (group * (1 << level) + (1 << (level - 1))) * SCORE_BLOCK - 1


def _emit_scan_operands(q, gate, bs, backward, out_ref, table_ref):
    k = gate[0]
    w = k.shape[1]
    row_of = lambda pos: SCAN_CHUNK - 1 - pos if backward else pos
    per_block = lambda vals: jnp.concatenate(
        [jnp.broadcast_to(vals[_scan_block(r, backward)], (SCORE_BLOCK, w)) for r in range(N_BLOCKS)], axis=0)
    for j, b in enumerate(bs):
        rs = slice(j * SCAN_CHUNK, (j + 1) * SCAN_CHUNK)
        at = lambda pos: b[row_of(pos):row_of(pos) + 1, :]
        b_last = at(SCAN_CHUNK - 1)
        if q is None:
            out_ref[rs, :] = (k[rs] * jnp.exp(b_last - b)).astype(BF16)
            table_ref[j] = jnp.broadcast_to(jnp.exp(b_last), table_ref.shape[1:])
            continue
        centre = [0.5 * (at(i * SCORE_BLOCK) + at((i + 1) * SCORE_BLOCK - 1)) for i in range(N_BLOCKS)]
        centre_rows = per_block(centre)
        q_c = q[rs] * jnp.exp(b - centre_rows)
        k_c = k[rs] * jnp.exp(centre_rows - b)
        out_ref[rs, 0:w] = q_c.astype(BF16)
        out_ref[rs, w:2 * w] = k_c.astype(BF16)
        rows = [jnp.exp(b_last)]
        for level in range(1, N_LEVELS + 1):
            to_edge = []
            for i in range(N_BLOCKS):
                edge = at(_level_edge(i, level))
                to_edge.append(jnp.exp(centre[i] - edge) if _is_later(i, level) else jnp.exp(edge - centre[i]))
            rows.extend(to_edge[_scan_block(r, backward)] for r in range(N_BLOCKS))
        rows.extend(jnp.exp(centre[_scan_block(r, backward)]) for r in range(N_BLOCKS))
        rows.extend(jnp.exp(b_last - centre[_scan_block(r, backward)]) for r in range(N_BLOCKS))
        rows.append(jnp.zeros((TABLE_ROWS - len(rows), w), F32))
        table_ref[j] = jnp.concatenate(rows, axis=0)


def _scan_operands(q, z, lb, backward, out_ref, decay_ref):
    gate = _gate_terms(z, lb)
    _emit_scan_operands(q, gate, _chunk_log_decay(gate, backward), backward, out_ref, decay_ref)


def _normed_input(x, gain_ref, sh_ref, sc_ref, row):
    scale = gain_ref[...] * (1.0 + sc_ref[row:row + 1, :])
    return (x * _rms_scale(x) * scale + sh_ref[row:row + 1, :]).astype(BF16)


N_CASTS = 3


def _inproj_raster_kernel(x_ref, sh_ref, sc_ref, gain_ref, wuv_ref, wg_ref, wq_ref, wi_ref, wf_ref, wb_ref,
                          ws_ref, bs_ref, vg_ref, lbl_ref, *rest):
    cast_ins, rest = rest[:N_CASTS], rest[N_CASTS:]
    a_ref, sg_ref, v_ref = rest[:3]
    fwd_refs, bwd_refs, cast_outs = rest[3:5], rest[5:7], rest[7:]
    hb = _normed_input(x_ref[...], gain_ref, sh_ref, sc_ref, 0)
    proj = lambda w_ref: jnp.dot(hb, w_ref[...], preferred_element_type=F32)

    q = proj(wq_ref)
    zf = proj(wf_ref)
    zb = proj(wb_ref)
    lb = _lower_bounds(lbl_ref)
    gate_f = _gate_terms(zf, lb[0:1, :HALF_W])
    u = jnp.dot(hb, wuv_ref[:, :GROUP_W], preferred_element_type=F32)
    b_f = _chunk_log_decay(gate_f, False)
    gate_b = _gate_terms(zb, lb[1:2, :HALF_W])
    v = jnp.dot(hb, wuv_ref[:, GROUP_W:], preferred_element_type=F32)
    _emit_scan_operands(q, gate_f, b_f, False, *fwd_refs)
    b_b = _chunk_log_decay(gate_b, True)
    g = proj(wg_ref)
    _emit_scan_operands(q, gate_b, b_b, True, *bwd_refs)
    for w_ref, wb16_ref in zip(cast_ins, cast_outs):
        wb16_ref[...] = w_ref[...].astype(wb16_ref.dtype)
    u = _gelu_tanh(u)
    v = _gelu_tanh(v)

    chunks = [slice(c * GMLP_CHUNK, (c + 1) * GMLP_CHUNK) for c in range(u.shape[0] // GMLP_CHUNK)]
    for h in range(HEADS):
        cs = slice(h * HEAD_DIM, (h + 1) * HEAD_DIM)
        vh = v[:, cs]
        vn = (vh * _rms_scale(vh) * vg_ref[h:h + 1, :]).astype(BF16)
        mixed = jnp.dot(ws_ref[h].astype(BF16), jnp.concatenate([vn[rs] for rs in chunks], axis=1),
                        preferred_element_type=F32)
        for c, rs in enumerate(chunks):
            a_ref[rs, cs] = (u[rs, cs] * (mixed[:, c * HEAD_DIM:(c + 1) * HEAD_DIM] + bs_ref[h])).astype(a_ref.dtype)

    vi = proj(wi_ref)
    sg_ref[...] = (g * _sigmoid(g)).astype(sg_ref.dtype)
    v_ref[...] = vi.astype(v_ref.dtype)


def _scan_operand_outputs(t, tm, width, index_map, decay_map):
    data = jax.ShapeDtypeStruct((t, N_OPERANDS * width), BF16)
    decay = jax.ShapeDtypeStruct((t // SCAN_CHUNK, TABLE_ROWS,width), F32)
    data_spec = pl.BlockSpec((tm, N_OPERANDS * width), index_map)
    decay_spec = pl.BlockSpec((tm // SCAN_CHUNK, TABLE_ROWS,width), decay_map)
    return [data, decay], [data_spec, decay_spec]


def _inproj_raster(x, mod, gain, w_bf, w_s, b_s_b, v_gain, lb_logits, casts):
    t, d = x.shape
    tm = 256
    n_steps = t // tm
    row = lambda i: (i, 0)
    const2 = lambda i: (0, 0)
    const3 = lambda i: (0, 0, 0)
    full = pl.BlockSpec((tm, GROUP_W), row)
    half = pl.BlockSpec((tm, HALF_W), row)
    resident = lambda width, blk: pl.BlockSpec((d, width), lambda i: (0, blk), pipeline_mode=pl.Buffered(1))
    cast_specs = [pl.BlockSpec((w.shape[0] // n_steps, w.shape[1]), row) for w in casts]
    op_shapes, op_specs = _scan_operand_outputs(t, tm, HALF_W, row, lambda i: (i, 0, 0))
    return pl.pallas_call(
        _inproj_raster_kernel,
        grid=(n_steps,),
        in_specs=[pl.BlockSpec((tm, d), row),
                  pl.BlockSpec((2, d), lambda i: (0, 0)),
                  pl.BlockSpec((2, d), lambda i: (0, 1)),
                  pl.BlockSpec((1, d), const2),
                  resident(2 * GROUP_W, 0), resident(GROUP_W, 6),
                  resident(HALF_W, 4), resident(HALF_W, 6), resident(HALF_W, 8), resident(HALF_W, 10),
                  pl.BlockSpec(w_s.shape, const3),
                  pl.BlockSpec(b_s_b.shape, const3),
                  pl.BlockSpec(v_gain.shape, const2),
                  pl.BlockSpec(lb_logits.shape, const3)] + cast_specs,
        out_specs=[full, full, half] + op_specs + op_specs + cast_specs,
        out_shape=([jax.ShapeDtypeStruct((t, GROUP_W), BF16)] * 2 + [jax.ShapeDtypeStruct((t, HALF_W), BF16)]
                   + op_shapes + op_shapes + [jax.ShapeDtypeStruct(w.shape, BF16) for w in casts]),
        compiler_params=_params(1),
        name="inproj_raster",
    )(x, mod, mod, gain, *([w_bf] * 6), w_s, b_s_b, v_gain, lb_logits, *casts)


COLS_PER_STEP = 8


def _gather_copies(x_hbm_ref, buf_ref, sem_ref, slot, step):
    rows = x_hbm_ref.shape[0]
    return [pltpu.make_async_copy(x_hbm_ref.at[:, step * COLS_PER_STEP + wi, :],
                                  buf_ref.at[slot, pl.ds(wi * rows, rows), :],
                                  sem_ref.at[slot, wi])
            for wi in range(COLS_PER_STEP)]


def _inproj_colmajor_kernel(x_hbm_ref, sh_ref, sc_ref, gain_ref, wq_ref, wi_ref, wf_ref, wb_ref, lbl_ref,
                            v_ref, *rest):
    fwd_refs, bwd_refs, (xbuf_ref, sem_ref) = rest[:2], rest[2:4], rest[4:]
    step, n_steps = pl.program_id(0), pl.num_programs(0)
    slot = step % 2
    copies = functools.partial(_gather_copies, x_hbm_ref, xbuf_ref, sem_ref)

    @pl.when(step == 0)
    def _():
        for cp in copies(slot, step):
            cp.start()

    @pl.when(step + 1 < n_steps)
    def _():
        for cp in copies(1 - slot, step + 1):
            cp.start()

    for cp in copies(slot, step):
        cp.wait()
    hb = _normed_input(xbuf_ref[slot], gain_ref, sh_ref, sc_ref, 0)
    proj = lambda w_ref: jnp.dot(hb, w_ref[...], preferred_element_type=F32)
    zf = proj(wf_ref)
    zb = proj(wb_ref)
    lb = _lower_bounds(lbl_ref)
    gate_f = _gate_terms(zf, lb[0:1, HALF_W:])
    q = proj(wq_ref)
    b_f = _chunk_log_decay(gate_f, False)
    gate_b = _gate_terms(zb, lb[1:2, HALF_W:])
    vi = proj(wi_ref)
    _emit_scan_operands(q, gate_f, b_f, False, *fwd_refs)
    b_b = _chunk_log_decay(gate_b, True)
    _emit_scan_operands(q, gate_b, b_b, True, *bwd_refs)
    v_ref[...] = vi.astype(v_ref.dtype)


def _inproj_colmajor(x, mod, gain, w_bf, lb_logits):
    t, d = x.shape
    rows = t // GRID_W
    tm = COLS_PER_STEP * rows
    row = lambda i: (i, 0)
    const2 = lambda i: (0, 0)
    w_spec = lambda g: pl.BlockSpec((d, HALF_W), lambda i: (0, 2 * g + 1), pipeline_mode=pl.Buffered(1))
    op_shapes, op_specs = _scan_operand_outputs(t, tm, HALF_W, row, lambda i: (i, 0, 0))
    return pl.pallas_call(
        _inproj_colmajor_kernel,
        grid=(GRID_W // COLS_PER_STEP,),
        in_specs=[pl.BlockSpec(memory_space=pl.ANY),
                  pl.BlockSpec((2, d), lambda i: (0, 0)),
                  pl.BlockSpec((2, d), lambda i: (0, 1)),
                  pl.BlockSpec((1, d), const2),
                  w_spec(2), w_spec(3), w_spec(4), w_spec(5),
                  pl.BlockSpec(lb_logits.shape, lambda i: (0, 0, 0))],
        out_specs=[pl.BlockSpec((tm, HALF_W), row)] + op_specs + op_specs,
        out_shape=[jax.ShapeDtypeStruct((t, HALF_W), BF16)] + op_shapes + op_shapes,
        scratch_shapes=[pltpu.VMEM((2, tm, d), F32), pltpu.SemaphoreType.DMA((2, COLS_PER_STEP))],
        compiler_params=_params(1),
        name="inproj_colmajor",
    )(x.reshape(rows, GRID_W, d), mod, mod, gain, w_bf, w_bf, w_bf, w_bf, lb_logits)


def _inproj_ctx_kernel(x_ref, sh_ref, sc_ref, gain_ref, w_ref, lbl_ref,
                       v_ref, kf_ref, df_ref, kb_ref, db_ref, hb_ref):
    step = pl.program_id(0)

    @pl.when(step == 0)
    def _():
        hb_ref[...] = _normed_input(x_ref[...], gain_ref, sh_ref, sc_ref, 1)

    z = jnp.dot(hb_ref[...], w_ref[...], preferred_element_type=F32)

    @pl.when(step == 0)
    def _():
        v_ref[...] = z.astype(v_ref.dtype)

    @pl.when(step == 1)
    def _():
        _scan_operands(None, z, _lower_bounds(lbl_ref)[0:1, :], False, kf_ref, df_ref)

    @pl.when(step == 2)
    def _():
        _scan_operands(None, z, _lower_bounds(lbl_ref)[1:2, :], True, kb_ref, db_ref)


def _inproj_ctx(ctx, mod, gain, w_bf, lb_logits):
    t, d = ctx.shape
    const2 = lambda i: (0, 0)
    const3 = lambda i: (0, 0, 0)
    data = jax.ShapeDtypeStruct((t, GROUP_W), BF16)
    decay = jax.ShapeDtypeStruct((t // SCAN_CHUNK, TABLE_ROWS,GROUP_W), F32)
    data_spec = pl.BlockSpec(data.shape, const2)
    decay_spec = pl.BlockSpec(decay.shape, const3)
    return pl.pallas_call(
        _inproj_ctx_kernel,
        grid=(3,),
        in_specs=[pl.BlockSpec((t, d), const2),
                  pl.BlockSpec((2, d), lambda i: (0, 0)),
                  pl.BlockSpec((2, d), lambda i: (0, 1)),
                  pl.BlockSpec((1, d), const2),
                  pl.BlockSpec((d, GROUP_W), lambda i: (0, 3 + i)),
                  pl.BlockSpec(lb_logits.shape, const3)],
        out_specs=[data_spec, data_spec, decay_spec, data_spec, decay_spec],
        out_shape=[data, data, decay, data, decay],
        scratch_shapes=[pltpu.VMEM((t, d), BF16)],
        compiler_params=_params(1),
        name="inproj_ctx",
    )(ctx, mod, mod, gain, w_bf, lb_logits)


def _block_factor(table, first):
    return jnp.concatenate([jnp.broadcast_to(table[first + r:first + r + 1, :], (SCORE_BLOCK, HEAD_DIM))
                            for r in range(N_BLOCKS)], axis=0).astype(BF16)


def _intra_chunk_scores(q_c, k_c, table, backward):
    nt = (((1,), (1,)), ((), ()))
    blk = SCORE_BLOCK
    ti = lax.broadcasted_iota(jnp.int32, (SCAN_CHUNK, SCAN_CHUNK), 0)
    si = lax.broadcasted_iota(jnp.int32, (SCAN_CHUNK, SCAN_CHUNK), 1)
    if backward:
        ti, si = SCAN_CHUNK - 1 - ti, SCAN_CHUNK - 1 - si
    bt, bs = ti // blk, si // blk
    scores = jnp.where((bt == bs) & (si <= ti), lax.dot_general(q_c, k_c, nt, preferred_element_type=F32), 0.0)
    for level in range(1, N_LEVELS + 1):
        to_edge = _block_factor(table, 1 + (level - 1) * N_BLOCKS)
        sees = ((bs >> level == bt >> level) & ((bt >> (level - 1)) & 1 == 1) & ((bs >> (level - 1)) & 1 == 0))
        s_level = lax.dot_general(q_c * to_edge, k_c * to_edge, nt, preferred_element_type=F32)
        scores = jnp.where(sees, s_level, scores)
    return scores.astype(BF16)


CHUNKS_PER_STEP = 4


def _sub_chunk_block(backward, j, n_sub):
    return n_sub - 1 - j if backward else j


def _sub_chunk_rows(backward, j, n_sub):
    p = _sub_chunk_block(backward, j, n_sub)
    return slice(p * SCAN_CHUNK, (p + 1) * SCAN_CHUNK)


def _scan_chunks(groups, state_ref, o_refs):
    units = [(g, h, slice(h * HEAD_DIM, (h + 1) * HEAD_DIM)) for g in range(4) for h in range(HALF_HEADS)]
    with_outputs = o_refs is not None
    n_sub = groups[0][1].shape[0] // SCAN_CHUNK

    def operand(g, j, n, cs):
        return groups[g][0][_sub_chunk_rows(g >= 2, j, n_sub), n * HALF_W + cs.start:n * HALF_W + cs.stop]

    scores = {}
    if with_outputs:
        for j in range(n_sub):
            for g, h, cs in units:
                table = groups[g][2][_sub_chunk_block(g >= 2, j, n_sub), :, cs]
                scores[j, g, h] = _intra_chunk_scores(operand(g, j, 0, cs), operand(g, j, 1, cs), table, g >= 2)
    for j in range(n_sub):
        outs = {}
        for g, h, cs in units:
            ops_ref, v_ref, table_ref = groups[g]
            rows = _sub_chunk_rows(g >= 2, j, n_sub)
            u = g * HALF_HEADS + h
            st = state_ref[u]
            vh = v_ref[rows, cs]
            table = table_ref[_sub_chunk_block(g >= 2, j, n_sub), :, cs]
            if with_outputs:
                q_st = operand(g, j, 0, cs) * _block_factor(table, 1 + N_LEVELS * N_BLOCKS)
                k_st = operand(g, j, 1, cs) * _block_factor(table, 1 + (N_LEVELS + 1) * N_BLOCKS)
                outs[g, h] = jnp.dot(jnp.concatenate([scores[j, g, h], q_st], axis=1),
                                     jnp.concatenate([vh, st.astype(BF16)], axis=0), preferred_element_type=F32)
            else:
                k_st = ops_ref[rows, cs]
            decay = jnp.broadcast_to(table[0:1, :], (HEAD_DIM, HEAD_DIM)).T
            state_ref[u] = st * decay + lax.dot_general(
                k_st, vh, (((0,), (0,)), ((), ())), preferred_element_type=F32)
        if with_outputs:
            for g in range(4):
                o_ref = o_refs[g][j]
                o_ref[...] = jnp.concatenate([outs[g, h] for h in range(HALF_HEADS)], axis=1).astype(o_ref.dtype)


def _scatter_copy(buf_ref, out_hbm_ref, sem_ref, slot, j, chunk):
    per_col = out_hbm_ref.shape[0] // SCAN_CHUNK
    r0 = (chunk % per_col) * SCAN_CHUNK
    dst = out_hbm_ref.at[pl.ds(r0, SCAN_CHUNK), chunk // per_col, :]
    return pltpu.make_async_copy(buf_ref.at[slot, j], dst, sem_ref.at[slot, j])


def _scan_kernel(*refs, emit_outputs):
    groups = [refs[3 * g:3 * g + 3] for g in range(4)]
    s0_ref, rest = refs[12], refs[13:]
    step, n_steps = pl.program_id(0), pl.num_programs(0)
    state_ref = rest[4] if emit_outputs else rest[0]

    @pl.when(step == 0)
    def _():
        state_ref[...] = s0_ref[...]

    if not emit_outputs:
        _scan_chunks(groups, state_ref, None)
        return

    o_refs, (bufs, sems) = rest[:4], (rest[5:7], rest[7:9])
    slot = step % 2
    n_sub = bufs[0].shape[1]
    subs = range(n_sub)
    _scan_chunks(groups, state_ref,
                 [[o_refs[0].at[_sub_chunk_rows(False, j, n_sub), :] for j in subs], [bufs[0].at[slot, j] for j in subs],
                  [o_refs[2].at[_sub_chunk_rows(True, j, n_sub), :] for j in subs], [bufs[1].at[slot, j] for j in subs]])
    n_chunks = n_steps * n_sub
    for d, backward in enumerate((False, True)):
        chunk_of = ((lambda s, j: n_chunks - 1 - (s * n_sub + j)) if backward
                    else (lambda s, j: s * n_sub + j))
        copy = functools.partial(_scatter_copy, bufs[d], o_refs[2 * d + 1], sems[d])
        for j in subs:
            copy(slot, j, chunk_of(step, j)).start()

        @pl.when(step > 0)
        def _():
            for j in subs:
                copy(1 - slot, j, chunk_of(step - 1, j)).wait()

        @pl.when(step == n_steps - 1)
        def _():
            for j in subs:
                copy(slot, j, chunk_of(step, j)).wait()


def _scan(groups, s0, n_steps, n_sub, emit_outputs, grid_rows, name):
    blk = (n_sub * SCAN_CHUNK, HALF_W)
    args, in_specs = [], []
    for arrays, backward, col in groups:
        args.extend(arrays)
        block_of = (lambda s: n_steps - 1 - s) if backward else (lambda s: s)
        data_map = lambda s, block_of=block_of, col=col: (block_of(s), col)
        table_map = lambda s, block_of=block_of, col=col: (block_of(s), 0, col)
        in_specs.append(pl.BlockSpec((blk[0], N_OPERANDS * HALF_W), lambda s, block_of=block_of:
                                     (block_of(s), 0)) if emit_outputs else pl.BlockSpec(blk, data_map))
        in_specs.append(pl.BlockSpec(blk, data_map))
        in_specs.append(pl.BlockSpec((n_sub, TABLE_ROWS, HALF_W), table_map))
    args.append(s0)
    state_spec = pl.BlockSpec(s0.shape, lambda s: (0, 0, 0))
    in_specs.append(state_spec)
    out_specs, out_shape, scratch = [], [], []
    if emit_outputs:
        raster = jax.ShapeDtypeStruct((grid_rows * GRID_W, HALF_W), BF16)
        scattered = jax.ShapeDtypeStruct((grid_rows, GRID_W, HALF_W), F32)
        in_hbm = pl.BlockSpec(memory_space=pl.ANY)
        out_shape = [raster, scattered, raster, scattered]
        out_specs = [pl.BlockSpec(blk, lambda s: (s, 0)), in_hbm,
                     pl.BlockSpec(blk, lambda s: (n_steps - 1 - s, 0)), in_hbm]
        staging = pltpu.VMEM((2, n_sub, SCAN_CHUNK, HALF_W), F32)
        scratch = [staging] * 2 + [pltpu.SemaphoreType.DMA((2, n_sub))] * 2
    out_shape.append(jax.ShapeDtypeStruct(s0.shape, F32))
    out_specs.append(state_spec)
    return pl.pallas_call(
        functools.partial(_scan_kernel, emit_outputs=emit_outputs),
        grid=(n_steps,),
        in_specs=in_specs,
        out_specs=out_specs,
        out_shape=out_shape,
        scratch_shapes=scratch,
        compiler_params=_params(1),
        name=name,
    )(*args)


def _scan_ctx(v, kf, df, kb, db):
    n_chunks = v.shape[0] // SCAN_CHUNK
    groups = [((kf, v, df), False, 0), ((kf, v, df), False, 1), ((kb, v, db), True, 0), ((kb, v, db), True, 1)]
    s0 = jnp.zeros((2 * HEADS, HEAD_DIM, HEAD_DIM), F32)
    (state,) = _scan(groups, s0, 1, n_chunks, False, 0, "scan_ctx")
    return state


def _scan_latent(raster, colmajor, s0):
    t = raster[0].shape[0]
    n_steps = t // (CHUNKS_PER_STEP * SCAN_CHUNK)
    groups = []
    for d, backward in enumerate((False, True)):
        for v, *operands in (raster, colmajor):
            packed, table = operands[2 * d:2 * d + 2]
            groups.append(((packed, v, table), backward, 0))
    o_fr, o_fc, o_br, o_bc, _ = _scan(groups, s0, n_steps, CHUNKS_PER_STEP, True, t // GRID_W, "scan_latent")
    return o_fr, o_fc.reshape(t, HALF_W), o_br, o_bc.reshape(t, HALF_W)


def _outproj_kernel(a_ref, ofr_ref, ofc_ref, obr_ref, obc_ref, sg_ref, x_ref, w_ref,
                    g1_ref, sh2_ref, sc2_ref, og_ref, n2g_ref, x1_ref, h2_ref):
    def readout(rs):
        o_r = ofr_ref[rs, :].astype(F32) + obr_ref[rs, :].astype(F32)
        o_c = ofc_ref[rs, :] + obc_ref[rs, :]
        ys = []
        for h in range(HEADS):
            o = o_r if h < HALF_HEADS else o_c
            oh = o[:, (h % HALF_HEADS) * HEAD_DIM:(h % HALF_HEADS + 1) * HEAD_DIM]
            cs = slice(h * HEAD_DIM, (h + 1) * HEAD_DIM)
            ys.append((oh * _rms_scale(oh) * og_ref[:, cs] * sg_ref[rs, cs].astype(F32)).astype(BF16))
        return jnp.concatenate(ys, axis=1)

    def finish(rs, proj):
        x1 = x_ref[rs, :] + g1_ref[0:1, :] * proj
        x1_ref[rs, :] = x1
        scale = n2g_ref[...] * (1.0 + sc2_ref[0:1, :])
        h2_ref[rs, :] = (x1 * _rms_scale(x1) * scale + sh2_ref[0:1, :]).astype(h2_ref.dtype)

    half = x_ref.shape[0] // 2
    top, bottom = slice(0, half), slice(half, 2 * half)
    gmlp_side = lambda rs: jnp.dot(a_ref[rs, :], w_ref[:GROUP_W, :], preferred_element_type=F32)
    hgrn_side = lambda y: jnp.dot(y, w_ref[GROUP_W:, :], preferred_element_type=F32)
    p_top = gmlp_side(top)
    y_top = readout(top)
    p_bottom = gmlp_side(bottom)
    y_bottom = readout(bottom)
    p_top = p_top + hgrn_side(y_top)
    p_bottom = p_bottom + hgrn_side(y_bottom)
    finish(top, p_top)
    finish(bottom, p_bottom)


def _outproj(a, o_fr, o_fc, o_br, o_bc, sg, x, w_bf, mod, o_gain, n2_gain):
    t, d = x.shape
    tm = 512
    row = lambda i: (i, 0)
    const2 = lambda i: (0, 0)
    half = pl.BlockSpec((tm, HALF_W), row)
    full = pl.BlockSpec((tm, GROUP_W), row)
    mod_col = lambda c: pl.BlockSpec((2, d), lambda i: (0, c))
    return pl.pallas_call(
        _outproj_kernel,
        grid=(t // tm,),
        in_specs=[full, half, half, half, half, full,
                  pl.BlockSpec((tm, d), row),
                  pl.BlockSpec(w_bf.shape, const2, pipeline_mode=pl.Buffered(1)),
                  mod_col(2), mod_col(3), mod_col(4),
                  pl.BlockSpec((1, GROUP_W), const2),
                  pl.BlockSpec((1, d), const2)],
        out_specs=[pl.BlockSpec((tm, d), row), pl.BlockSpec((tm, d), row)],
        out_shape=[jax.ShapeDtypeStruct((t, d), F32), jax.ShapeDtypeStruct((t, d), BF16)],
        compiler_params=_params(1),
        name="outproj",
    )(a, o_fr, o_fc, o_br, o_bc, sg, x, w_bf, mod, mod, mod, o_gain, n2_gain)


def _ffn_kernel(h_ref, x1_ref, w1_ref, w2_ref, g2_ref, fg_ref, out_ref):
    f = pl.program_id(1)

    @pl.when(f == 0)
    def _():
        out_ref[...] = jnp.zeros_like(out_ref)

    hid = jnp.dot(h_ref[...], w1_ref[...], preferred_element_type=F32)
    hid = jnp.square(jnp.maximum(hid, 0.0)).astype(BF16)
    out_ref[...] += jnp.dot(hid, w2_ref[...], preferred_element_type=F32)

    @pl.when(f == pl.num_programs(1) - 1)
    def _():
        x2 = x1_ref[...] + g2_ref[0:1, :] * out_ref[...]
        out_ref[...] = x2 * _rms_scale(x2) * fg_ref[...]


def _ffn(h2, x1, w1_bf, w2_bf, mod, final_gain):
    t, d = x1.shape
    d_ff = w1_bf.shape[1]
    tm, tf = 1024, 1024
    row = lambda i, f: (i, 0)
    return pl.pallas_call(
        _ffn_kernel,
        grid=(t // tm, d_ff // tf),
        in_specs=[pl.BlockSpec((tm, d), row),
                  pl.BlockSpec((tm, d), row, pipeline_mode=pl.Buffered(1)),
                  pl.BlockSpec((d, tf), lambda i, f: (0, f)),
                  pl.BlockSpec((tf, d), lambda i, f: (f, 0)),
                  pl.BlockSpec((2, d), lambda i, f: (0, 5)),
                  pl.BlockSpec((1, d), lambda i, f: (0, 0))],
        out_specs=pl.BlockSpec((tm, d), row, pipeline_mode=pl.Buffered(1)),
        out_shape=jax.ShapeDtypeStruct((t, d), F32),
        compiler_params=_params(2),
        name="ffn",
    )(h2, x1, w1_bf, w2_bf, mod, final_gain)


def kernel(x, c, ctx, c_ctx, w_ada, b_ada, norm1_gain, w_in, gmlp_w_s, gmlp_b_s, gmlp_v_gain,
           hgrn_lb_logits, hgrn_o_gain, w_out, norm2_gain, w_ff1, w_ff2, final_gain):
    bsz, t, d = x.shape
    assert bsz == 1 and w_ada.shape[0] == 1, "single sample, single layer"
    assert d == 2 * GROUP_W and w_in.shape[2] == 7 * GROUP_W, "head groups of 8 x 128 channels"
    assert t % (GRID_W * SCAN_CHUNK) == 0 and t % (CHUNKS_PER_STEP * SCAN_CHUNK) == 0, "latent grid shape"
    assert ctx.shape[1] % SCAN_CHUNK == 0, "context length must be whole scan chunks"

    cond_t = jnp.stack([c[0], c_ctx], axis=1)
    mod, w_in_bf = _adaln(cond_t, w_ada[0], b_ada, w_in[0])

    b_s_b = jnp.broadcast_to(gmlp_b_s[0][:, :, None], (HEADS, GMLP_CHUNK, HEAD_DIM))
    a, sg, *rest = _inproj_raster(x[0], mod, norm1_gain, w_in_bf, gmlp_w_s[0], b_s_b, gmlp_v_gain[0],
                                  hgrn_lb_logits, [w_out[0], w_ff1[0], w_ff2[0]])
    raster, (w_out_bf, w1_bf, w2_bf) = rest[:-N_CASTS], rest[-N_CASTS:]
    colmajor = _inproj_colmajor(x[0], mod, norm1_gain, w_in_bf, hgrn_lb_logits)
    ctx_ops = _inproj_ctx(ctx[0], mod, norm1_gain, w_in_bf, hgrn_lb_logits)

    s0 = _scan_ctx(*ctx_ops)
    o_fr, o_fc, o_br, o_bc = _scan_latent(raster, colmajor, s0)

    x1, h2 = _outproj(a, o_fr, o_fc, o_br, o_bc, sg, x[0], w_out_bf, mod,
                      hgrn_o_gain.reshape(1, GROUP_W), norm2_gain)
    out = _ffn(h2, x1, w1_bf, w2_bf, mod, final_gain.reshape(1, d))
    return out[None]
```

```python
import functools

import jax
import jax.numpy as jnp
from jax import lax
from jax.experimental import pallas as pl
from jax.experimental.pallas import tpu as pltpu

EPS = 1e-6
GRID_W = 64
HEADS = 8
HALF_HEADS = HEADS // 2
HEAD_DIM = 128
GROUP_W = HEADS * HEAD_DIM
HALF_W = HALF_HEADS * HEAD_DIM
GMLP_CHUNK = 128
SCAN_CHUNK = 128
LANES = 128
N_LEVELS = 2
N_BLOCKS = 1 << N_LEVELS
SCORE_BLOCK = SCAN_CHUNK // N_BLOCKS
TABLE_ROWS = 24
VMEM_LIMIT = 56 * 1024 * 1024

F32 = jnp.float32
BF16 = jnp.bfloat16


def _params(n_axes, vmem=VMEM_LIMIT):
    return pltpu.CompilerParams(dimension_semantics=("arbitrary",) * n_axes, vmem_limit_bytes=vmem)


def _sigmoid(x):
    return 1.0 / (1.0 + jnp.exp(-x))


def _gelu_tanh(x):
    half = 0.5 * x
    return half + half * jnp.tanh(x * (0.7978845608028654 + (0.7978845608028654 * 0.044715) * (x * x)))


def _rms_scale(x):
    return lax.rsqrt(jnp.mean(x * x, axis=-1, keepdims=True) + EPS)


ADALN_RING = 3


def _adaln_block_copy(w_hbm_ref, ring_ref, sem_ref, block):
    tn = ring_ref.shape[2]
    slot = block % ADALN_RING
    return pltpu.make_async_copy(w_hbm_ref.at[:, pl.ds(block * tn, tn)], ring_ref.at[slot], sem_ref.at[slot])


def _adaln_kernel(cond_t_ref, w_hbm_ref, b_ref, w_in_ref, out_ref, w_in_bf_ref, s_ref, ring_ref, sem_ref):
    step, n_steps = pl.program_id(0), pl.num_programs(0)
    copy = functools.partial(_adaln_block_copy, w_hbm_ref, ring_ref, sem_ref)

    @pl.when(step == 0)
    def _():
        for blk in range(ADALN_RING - 1):
            copy(blk).start()
        ct = cond_t_ref[...]
        s = ct * _sigmoid(ct)
        for r in range(2):
            s_ref[r] = jnp.broadcast_to(s[:, r:r + 1], s_ref.shape[1:])

    @pl.when(step + ADALN_RING - 1 < n_steps)
    def _():
        copy(step + ADALN_RING - 1).start()

    w_in_bf_ref[...] = w_in_ref[...].astype(w_in_bf_ref.dtype)
    copy(step).wait()
    w_ref = ring_ref.at[step % ADALN_RING]
    d, tn = w_ref.shape
    rows_per_step = 8

    def body(i, acc):
        rows = pl.ds(pl.multiple_of(i * rows_per_step, rows_per_step), rows_per_step)
        w = w_ref[rows, :]
        return tuple(a + w * jnp.tile(s_ref[r, rows, :], (1, tn // LANES)) for r, a in enumerate(acc))

    zeros = jnp.zeros((rows_per_step, tn), F32)
    acc = lax.fori_loop(0, d // rows_per_step, body, (zeros, zeros), unroll=8)
    out_ref[...] = jnp.concatenate([jnp.sum(a, axis=0, keepdims=True) for a in acc], axis=0) + b_ref[...]


def _adaln(cond_t, w_ada, b_ada, w_in):
    d, n = w_ada.shape
    n_steps = 16
    tn, slab = n // n_steps, w_in.shape[0] // n_steps
    return pl.pallas_call(
        _adaln_kernel,
        grid=(n_steps,),
        in_specs=[pl.BlockSpec((d, 2), lambda j: (0, 0)),
                  pl.BlockSpec(memory_space=pl.ANY),
                  pl.BlockSpec((1, tn), lambda j: (0, j)),
                  pl.BlockSpec((slab, w_in.shape[1]), lambda j: (j, 0))],
        out_specs=[pl.BlockSpec((2, tn), lambda j: (0, j)),
                   pl.BlockSpec((slab, w_in.shape[1]), lambda j: (j, 0))],
        out_shape=[jax.ShapeDtypeStruct((2, n), F32), jax.ShapeDtypeStruct(w_in.shape, BF16)],
        scratch_shapes=[pltpu.VMEM((2, d, LANES), F32), pltpu.VMEM((ADALN_RING, d, tn), F32),
                        pltpu.SemaphoreType.DMA((ADALN_RING,))],
        compiler_params=_params(1),
        name="adaln",
    )(cond_t, w_ada, b_ada, w_in)


def _lower_bounds(lbl_ref):
    l0, l1 = lbl_ref[0], lbl_ref[1]
    m = jnp.maximum(l0, l1)
    e0, e1 = jnp.exp(l0 - m), jnp.exp(l1 - m)
    return e1 / (e0 + e1)


def _visible(backward):
    ti = lax.broadcasted_iota(jnp.int32, (SCAN_CHUNK, SCAN_CHUNK), 0)
    si = lax.broadcasted_iota(jnp.int32, (SCAN_CHUNK, SCAN_CHUNK), 1)
    return (si >= ti) if backward else (si <= ti)


def _gate_terms(z, lb):
    f = lb + (1.0 - lb) * _sigmoid(z)
    lf = jnp.log(f)
    hi = lf.astype(BF16)
    lo = (lf - hi.astype(F32)).astype(BF16)
    return 1.0 - f, hi, lo


def _chunk_log_decay(gate, backward):
    _, hi, lo = gate
    tri = _visible(backward).astype(BF16)
    c = SCAN_CHUNK
    return [jnp.dot(tri, hi[j * c:(j + 1) * c], preferred_element_type=F32)
            + jnp.dot(tri, lo[j * c:(j + 1) * c], preferred_element_type=F32)
            for j in range(hi.shape[0] // c)]


N_OPERANDS = 2


def _scan_block(r, backward):
    return N_BLOCKS - 1 - r if backward else r


def _is_later(i, level):
    return (i >> (level - 1)) & 1 == 1


def _level_edge(i, level):
    group = i >> level
    return (group * (1 << level) + (1 << (level - 1))) * SCORE_BLOCK - 1


def _emit_scan_operands(q, gate, bs, backward, out_ref, table_ref):
    k = gate[0]
    w = k.shape[1]
    row_of = lambda pos: SCAN_CHUNK - 1 - pos if backward else pos
    per_block = lambda vals: jnp.concatenate(
        [jnp.broadcast_to(vals[_scan_block(r, backward)], (SCORE_BLOCK, w)) for r in range(N_BLOCKS)], axis=0)
    for j, b in enumerate(bs):
        rs = slice(j * SCAN_CHUNK, (j + 1) * SCAN_CHUNK)
        at = lambda pos: b[row_of(pos):row_of(pos) + 1, :]
        b_last = at(SCAN_CHUNK - 1)
        if q is None:
            out_ref[rs, :] = (k[rs] * jnp.exp(b_last - b)).astype(BF16)
            table_ref[j] = jnp.broadcast_to(jnp.exp(b_last), table_ref.shape[1:])
            continue
        centre = [0.5 * (at(i * SCORE_BLOCK) + at((i + 1) * SCORE_BLOCK - 1)) for i in range(N_BLOCKS)]
        centre_rows = per_block(centre)
        q_c = q[rs] * jnp.exp(b - centre_rows)
        k_c = k[rs] * jnp.exp(centre_rows - b)
        out_ref[rs, 0:w] = q_c.astype(BF16)
        out_ref[rs, w:2 * w] = k_c.astype(BF16)
        rows = [jnp.exp(b_last)]
        for level in range(1, N_LEVELS + 1):
            to_edge = []
            for i in range(N_BLOCKS):
                edge = at(_level_edge(i, level))
                to_edge.append(jnp.exp(centre[i] - edge) if _is_later(i, level) else jnp.exp(edge - centre[i]))
            rows.extend(to_edge[_scan_block(r, backward)] for r in range(N_BLOCKS))
        rows.extend(jnp.exp(centre[_scan_block(r, backward)]) for r in range(N_BLOCKS))
        rows.extend(jnp.exp(b_last - centre[_scan_block(r, backward)]) for r in range(N_BLOCKS))
        rows.append(jnp.zeros((TABLE_ROWS - len(rows), w), F32))
        table_ref[j] = jnp.concatenate(rows, axis=0)


def _scan_operands(q, z, lb, backward, out_ref, decay_ref):
    gate = _gate_terms(z, lb)
    _emit_scan_operands(q, gate, _chunk_log_decay(gate, backward), backward, out_ref, decay_ref)


def _normed_input(x, gain_ref, sh_ref, sc_ref, row):
    scale = gain_ref[...] * (1.0 + sc_ref[row:row + 1, :])
    return (x * _rms_scale(x) * scale + sh_ref[row:row + 1, :]).astype(BF16)


N_CASTS = 3


def _inproj_raster_kernel(x_ref, sh_ref, sc_ref, gain_ref, wuv_ref, wg_ref, wq_ref, wi_ref, wf_ref, wb_ref,
                          ws_ref, bs_ref, vg_ref, lbl_ref, *rest):
    cast_ins, rest = rest[:N_CASTS], rest[N_CASTS:]
    a_ref, sg_ref, v_ref = rest[:3]
    fwd_refs, bwd_refs, cast_outs = rest[3:5], rest[5:7], rest[7:]
    hb = _normed_input(x_ref[...], gain_ref, sh_ref, sc_ref, 0)
    proj = lambda w_ref: jnp.dot(hb, w_ref[...], preferred_element_type=F32)

    q = proj(wq_ref)
    zf = proj(wf_ref)
    zb = proj(wb_ref)
    lb = _lower_bounds(lbl_ref)
    gate_f = _gate_terms(zf, lb[0:1, :HALF_W])
    u = jnp.dot(hb, wuv_ref[:, :GROUP_W], preferred_element_type=F32)
    b_f = _chunk_log_decay(gate_f, False)
    gate_b = _gate_terms(zb, lb[1:2, :HALF_W])
    v = jnp.dot(hb, wuv_ref[:, GROUP_W:], preferred_element_type=F32)
    _emit_scan_operands(q, gate_f, b_f, False, *fwd_refs)
    b_b = _chunk_log_decay(gate_b, True)
    g = proj(wg_ref)
    _emit_scan_operands(q, gate_b, b_b, True, *bwd_refs)
    for w_ref, wb16_ref in zip(cast_ins, cast_outs):
        wb16_ref[...] = w_ref[...].astype(wb16_ref.dtype)
    u = _gelu_tanh(u)
    v = _gelu_tanh(v)

    chunks = [slice(c * GMLP_CHUNK, (c + 1) * GMLP_CHUNK) for c in range(u.shape[0] // GMLP_CHUNK)]
    for h in range(HEADS):
        cs = slice(h * HEAD_DIM, (h + 1) * HEAD_DIM)
        vh = v[:, cs]
        vn = (vh * _rms_scale(vh) * vg_ref[h:h + 1, :]).astype(BF16)
        mixed = jnp.dot(ws_ref[h].astype(BF16), jnp.concatenate([vn[rs] for rs in chunks], axis=1),
                        preferred_element_type=F32)
        for c, rs in enumerate(chunks):
            a_ref[rs, cs] = (u[rs, cs] * (mixed[:, c * HEAD_DIM:(c + 1) * HEAD_DIM] + bs_ref[h])).astype(a_ref.dtype)

    vi = proj(wi_ref)
    sg_ref[...] = (g * _sigmoid(g)).astype(sg_ref.dtype)
    v_ref[...] = vi.astype(v_ref.dtype)


def _scan_operand_outputs(t, tm, width, index_map, decay_map):
    data = jax.ShapeDtypeStruct((t, N_OPERANDS * width), BF16)
    decay = jax.ShapeDtypeStruct((t // SCAN_CHUNK, TABLE_ROWS,width), F32)
    data_spec = pl.BlockSpec((tm, N_OPERANDS * width), index_map)
    decay_spec = pl.BlockSpec((tm // SCAN_CHUNK, TABLE_ROWS,width), decay_map)
    return [data, decay], [data_spec, decay_spec]


def _inproj_raster(x, mod, gain, w_bf, w_s, b_s_b, v_gain, lb_logits, casts):
    t, d = x.shape
    tm = 256
    n_steps = t // tm
    row = lambda i: (i, 0)
    const2 = lambda i: (0, 0)
    const3 = lambda i: (0, 0, 0)
    full = pl.BlockSpec((tm, GROUP_W), row)
    half = pl.BlockSpec((tm, HALF_W), row)
    resident = lambda width, blk: pl.BlockSpec((d, width), lambda i: (0, blk), pipeline_mode=pl.Buffered(1))
    cast_specs = [pl.BlockSpec((w.shape[0] // n_steps, w.shape[1]), row) for w in casts]
    op_shapes, op_specs = _scan_operand_outputs(t, tm, HALF_W, row, lambda i: (i, 0, 0))
    return pl.pallas_call(
        _inproj_raster_kernel,
        grid=(n_steps,),
        in_specs=[pl.BlockSpec((tm, d), row),
                  pl.BlockSpec((2, d), lambda i: (0, 0)),
                  pl.BlockSpec((2, d), lambda i: (0, 1)),
                  pl.BlockSpec((1, d), const2),
                  resident(2 * GROUP_W, 0), resident(GROUP_W, 6),
                  resident(HALF_W, 4), resident(HALF_W, 6), resident(HALF_W, 8), resident(HALF_W, 10),
                  pl.BlockSpec(w_s.shape, const3),
                  pl.BlockSpec(b_s_b.shape, const3),
                  pl.BlockSpec(v_gain.shape, const2),
                  pl.BlockSpec(lb_logits.shape, const3)] + cast_specs,
        out_specs=[full, full, half] + op_specs + op_specs + cast_specs,
        out_shape=([jax.ShapeDtypeStruct((t, GROUP_W), BF16)] * 2 + [jax.ShapeDtypeStruct((t, HALF_W), BF16)]
                   + op_shapes + op_shapes + [jax.ShapeDtypeStruct(w.shape, BF16) for w in casts]),
        compiler_params=_params(1),
        name="inproj_raster",
    )(x, mod, mod, gain, *([w_bf] * 6), w_s, b_s_b, v_gain, lb_logits, *casts)


COLS_PER_STEP = 4


def _gather_copies(x_hbm_ref, buf_ref, sem_ref, slot, step):
    rows = x_hbm_ref.shape[0]
    return [pltpu.make_async_copy(x_hbm_ref.at[:, step * COLS_PER_STEP + wi, :],
                                  buf_ref.at[slot, pl.ds(wi * rows, rows), :],
                                  sem_ref.at[slot, wi])
            for wi in range(COLS_PER_STEP)]


def _inproj_colmajor_kernel(x_hbm_ref, sh_ref, sc_ref, gain_ref, wq_ref, wi_ref, wf_ref, wb_ref, lbl_ref,
                            v_ref, *rest):
    fwd_refs, bwd_refs, (xbuf_ref, sem_ref) = rest[:2], rest[2:4], rest[4:]
    step, n_steps = pl.program_id(0), pl.num_programs(0)
    slot = step % 2
    copies = functools.partial(_gather_copies, x_hbm_ref, xbuf_ref, sem_ref)

    @pl.when(step == 0)
    def _():
        for cp in copies(slot, step):
            cp.start()

    @pl.when(step + 1 < n_steps)
    def _():
        for cp in copies(1 - slot, step + 1):
            cp.start()

    for cp in copies(slot, step):
        cp.wait()
    hb = _normed_input(xbuf_ref[slot], gain_ref, sh_ref, sc_ref, 0)
    proj = lambda w_ref: jnp.dot(hb, w_ref[...], preferred_element_type=F32)
    zf = proj(wf_ref)
    zb = proj(wb_ref)
    lb = _lower_bounds(lbl_ref)
    gate_f = _gate_terms(zf, lb[0:1, HALF_W:])
    q = proj(wq_ref)
    b_f = _chunk_log_decay(gate_f, False)
    gate_b = _gate_terms(zb, lb[1:2, HALF_W:])
    vi = proj(wi_ref)
    _emit_scan_operands(q, gate_f, b_f, False, *fwd_refs)
    b_b = _chunk_log_decay(gate_b, True)
    _emit_scan_operands(q, gate_b, b_b, True, *bwd_refs)
    v_ref[...] = vi.astype(v_ref.dtype)


def _inproj_colmajor(x, mod, gain, w_bf, lb_logits):
    t, d = x.shape
    rows = t // GRID_W
    tm = COLS_PER_STEP * rows
    row = lambda i: (i, 0)
    const2 = lambda i: (0, 0)
    w_spec = lambda g: pl.BlockSpec((d, HALF_W), lambda i: (0, 2 * g + 1), pipeline_mode=pl.Buffered(1))
    op_shapes, op_specs = _scan_operand_outputs(t, tm, HALF_W, row, lambda i: (i, 0, 0))
    return pl.pallas_call(
        _inproj_colmajor_kernel,
        grid=(GRID_W // COLS_PER_STEP,),
        in_specs=[pl.BlockSpec(memory_space=pl.ANY),
                  pl.BlockSpec((2, d), lambda i: (0, 0)),
                  pl.BlockSpec((2, d), lambda i: (0, 1)),
                  pl.BlockSpec((1, d), const2),
                  w_spec(2), w_spec(3), w_spec(4), w_spec(5),
                  pl.BlockSpec(lb_logits.shape, lambda i: (0, 0, 0))],
        out_specs=[pl.BlockSpec((tm, HALF_W), row)] + op_specs + op_specs,
        out_shape=[jax.ShapeDtypeStruct((t, HALF_W), BF16)] + op_shapes + op_shapes,
        scratch_shapes=[pltpu.VMEM((2, tm, d), F32), pltpu.SemaphoreType.DMA((2, COLS_PER_STEP))],
        compiler_params=_params(1),
        name="inproj_colmajor",
    )(x.reshape(rows, GRID_W, d), mod, mod, gain, w_bf, w_bf, w_bf, w_bf, lb_logits)


def _inproj_ctx_kernel(x_ref, sh_ref, sc_ref, gain_ref, w_ref, lbl_ref,
                       v_ref, kf_ref, df_ref, kb_ref, db_ref, hb_ref):
    step = pl.program_id(0)

    @pl.when(step == 0)
    def _():
        hb_ref[...] = _normed_input(x_ref[...], gain_ref, sh_ref, sc_ref, 1)

    z = jnp.dot(hb_ref[...], w_ref[...], preferred_element_type=F32)

    @pl.when(step == 0)
    def _():
        v_ref[...] = z.astype(v_ref.dtype)

    @pl.when(step == 1)
    def _():
        _scan_operands(None, z, _lower_bounds(lbl_ref)[0:1, :], False, kf_ref, df_ref)

    @pl.when(step == 2)
    def _():
        _scan_operands(None, z, _lower_bounds(lbl_ref)[1:2, :], True, kb_ref, db_ref)


def _inproj_ctx(ctx, mod, gain, w_bf, lb_logits):
    t, d = ctx.shape
    const2 = lambda i: (0, 0)
    const3 = lambda i: (0, 0, 0)
    data = jax.ShapeDtypeStruct((t, GROUP_W), BF16)
    decay = jax.ShapeDtypeStruct((t // SCAN_CHUNK, TABLE_ROWS,GROUP_W), F32)
    data_spec = pl.BlockSpec(data.shape, const2)
    decay_spec = pl.BlockSpec(decay.shape, const3)
    return pl.pallas_call(
        _inproj_ctx_kernel,
        grid=(3,),
        in_specs=[pl.BlockSpec((t, d), const2),
                  pl.BlockSpec((2, d), lambda i: (0, 0)),
                  pl.BlockSpec((2, d), lambda i: (0, 1)),
                  pl.BlockSpec((1, d), const2),
                  pl.BlockSpec((d, GROUP_W), lambda i: (0, 3 + i)),
                  pl.BlockSpec(lb_logits.shape, const3)],
        out_specs=[data_spec, data_spec, decay_spec, data_spec, decay_spec],
        out_shape=[data, data, decay, data, decay],
        scratch_shapes=[pltpu.VMEM((t, d), BF16)],
        compiler_params=_params(1),
        name="inproj_ctx",
    )(ctx, mod, mod, gain, w_bf, lb_logits)


def _block_factor(table, first):
    return jnp.concatenate([jnp.broadcast_to(table[first + r:first + r + 1, :], (SCORE_BLOCK, HEAD_DIM))
                            for r in range(N_BLOCKS)], axis=0).astype(BF16)


def _intra_chunk_scores(q_c, k_c, table, backward):
    nt = (((1,), (1,)), ((), ()))
    blk = SCORE_BLOCK
    ti = lax.broadcasted_iota(jnp.int32, (SCAN_CHUNK, SCAN_CHUNK), 0)
    si = lax.broadcasted_iota(jnp.int32, (SCAN_CHUNK, SCAN_CHUNK), 1)
    if backward:
        ti, si = SCAN_CHUNK - 1 - ti, SCAN_CHUNK - 1 - si
    bt, bs = ti // blk, si // blk
    scores = jnp.where((bt == bs) & (si <= ti), lax.dot_general(q_c, k_c, nt, preferred_element_type=F32), 0.0)
    for level in range(1, N_LEVELS + 1):
        to_edge = _block_factor(table, 1 + (level - 1) * N_BLOCKS)
        sees = ((bs >> level == bt >> level) & ((bt >> (level - 1)) & 1 == 1) & ((bs >> (level - 1)) & 1 == 0))
        s_level = lax.dot_general(q_c * to_edge, k_c * to_edge, nt, preferred_element_type=F32)
        scores = jnp.where(sees, s_level, scores)
    return scores.astype(BF16)


CHUNKS_PER_STEP = 4


def _sub_chunk_block(backward, j, n_sub):
    return n_sub - 1 - j if backward else j


def _sub_chunk_rows(backward, j, n_sub):
    p = _sub_chunk_block(backward, j, n_sub)
    return slice(p * SCAN_CHUNK, (p + 1) * SCAN_CHUNK)


def _scan_chunks(groups, state_ref, o_refs):
    units = [(g, h, slice(h * HEAD_DIM, (h + 1) * HEAD_DIM)) for g in range(4) for h in range(HALF_HEADS)]
    with_outputs = o_refs is not None
    n_sub = groups[0][1].shape[0] // SCAN_CHUNK

    def operand(g, j, n, cs):
        return groups[g][0][_sub_chunk_rows(g >= 2, j, n_sub), n * HALF_W + cs.start:n * HALF_W + cs.stop]

    scores = {}
    if with_outputs:
        for j in range(n_sub):
            for g, h, cs in units:
                table = groups[g][2][_sub_chunk_block(g >= 2, j, n_sub), :, cs]
                scores[j, g, h] = _intra_chunk_scores(operand(g, j, 0, cs), operand(g, j, 1, cs), table, g >= 2)
    for j in range(n_sub):
        outs = {}
        for g, h, cs in units:
            ops_ref, v_ref, table_ref = groups[g]
            rows = _sub_chunk_rows(g >= 2, j, n_sub)
            u = g * HALF_HEADS + h
            st = state_ref[u]
            vh = v_ref[rows, cs]
            table = table_ref[_sub_chunk_block(g >= 2, j, n_sub), :, cs]
            if with_outputs:
                q_st = operand(g, j, 0, cs) * _block_factor(table, 1 + N_LEVELS * N_BLOCKS)
                k_st = operand(g, j, 1, cs) * _block_factor(table, 1 + (N_LEVELS + 1) * N_BLOCKS)
                outs[g, h] = jnp.dot(jnp.concatenate([scores[j, g, h], q_st], axis=1),
                                     jnp.concatenate([vh, st.astype(BF16)], axis=0), preferred_element_type=F32)
            else:
                k_st = ops_ref[rows, cs]
            decay = jnp.broadcast_to(table[0:1, :], (HEAD_DIM, HEAD_DIM)).T
            state_ref[u] = st * decay + lax.dot_general(
                k_st, vh, (((0,), (0,)), ((), ())), preferred_element_type=F32)
        if with_outputs:
            for g in range(4):
                o_ref = o_refs[g][j]
                o_ref[...] = jnp.concatenate([outs[g, h] for h in range(HALF_HEADS)], axis=1).astype(o_ref.dtype)


def _scatter_copy(buf_ref, out_hbm_ref, sem_ref, slot, j, chunk):
    per_col = out_hbm_ref.shape[0] // SCAN_CHUNK
    r0 = (chunk % per_col) * SCAN_CHUNK
    dst = out_hbm_ref.at[pl.ds(r0, SCAN_CHUNK), chunk // per_col, :]
    return pltpu.make_async_copy(buf_ref.at[slot, j], dst, sem_ref.at[slot, j])


def _scan_kernel(*refs, emit_outputs):
    groups = [refs[3 * g:3 * g + 3] for g in range(4)]
    s0_ref, rest = refs[12], refs[13:]
    step, n_steps = pl.program_id(0), pl.num_programs(0)
    state_ref = rest[4] if emit_outputs else rest[0]

    @pl.when(step == 0)
    def _():
        state_ref[...] = s0_ref[...]

    if not emit_outputs:
        _scan_chunks(groups, state_ref, None)
        return

    o_refs, (bufs, sems) = rest[:4], (rest[5:7], rest[7:9])
    slot = step % 2
    n_sub = bufs[0].shape[1]
    subs = range(n_sub)
    _scan_chunks(groups, state_ref,
                 [[o_refs[0].at[_sub_chunk_rows(False, j, n_sub), :] for j in subs], [bufs[0].at[slot, j] for j in subs],
                  [o_refs[2].at[_sub_chunk_rows(True, j, n_sub), :] for j in subs], [bufs[1].at[slot, j] for j in subs]])
    n_chunks = n_steps * n_sub
    for d, backward in enumerate((False, True)):
        chunk_of = ((lambda s, j: n_chunks - 1 - (s * n_sub + j)) if backward
                    else (lambda s, j: s * n_sub + j))
        copy = functools.partial(_scatter_copy, bufs[d], o_refs[2 * d + 1], sems[d])
        for j in subs:
            copy(slot, j, chunk_of(step, j)).start()

        @pl.when(step > 0)
        def _():
            for j in subs:
                copy(1 - slot, j, chunk_of(step - 1, j)).wait()

        @pl.when(step == n_steps - 1)
        def _():
            for j in subs:
                copy(slot, j, chunk_of(step, j)).wait()


def _scan(groups, s0, n_steps, n_sub, emit_outputs, grid_rows, name):
    blk = (n_sub * SCAN_CHUNK, HALF_W)
    args, in_specs = [], []
    for arrays, backward, col in groups:
        args.extend(arrays)
        block_of = (lambda s: n_steps - 1 - s) if backward else (lambda s: s)
        data_map = lambda s, block_of=block_of, col=col: (block_of(s), col)
        table_map = lambda s, block_of=block_of, col=col: (block_of(s), 0, col)
        in_specs.append(pl.BlockSpec((blk[0], N_OPERANDS * HALF_W), lambda s, block_of=block_of:
                                     (block_of(s), 0)) if emit_outputs else pl.BlockSpec(blk, data_map))
        in_specs.append(pl.BlockSpec(blk, data_map))
        in_specs.append(pl.BlockSpec((n_sub, TABLE_ROWS, HALF_W), table_map))
    args.append(s0)
    state_spec = pl.BlockSpec(s0.shape, lambda s: (0, 0, 0))
    in_specs.append(state_spec)
    out_specs, out_shape, scratch = [], [], []
    if emit_outputs:
        raster = jax.ShapeDtypeStruct((grid_rows * GRID_W, HALF_W), BF16)
        scattered = jax.ShapeDtypeStruct((grid_rows, GRID_W, HALF_W), F32)
        in_hbm = pl.BlockSpec(memory_space=pl.ANY)
        out_shape = [raster, scattered, raster, scattered]
        out_specs = [pl.BlockSpec(blk, lambda s: (s, 0)), in_hbm,
                     pl.BlockSpec(blk, lambda s: (n_steps - 1 - s, 0)), in_hbm]
        staging = pltpu.VMEM((2, n_sub, SCAN_CHUNK, HALF_W), F32)
        scratch = [staging] * 2 + [pltpu.SemaphoreType.DMA((2, n_sub))] * 2
    out_shape.append(jax.ShapeDtypeStruct(s0.shape, F32))
    out_specs.append(state_spec)
    return pl.pallas_call(
        functools.partial(_scan_kernel, emit_outputs=emit_outputs),
        grid=(n_steps,),
        in_specs=in_specs,
        out_specs=out_specs,
        out_shape=out_shape,
        scratch_shapes=scratch,
        compiler_params=_params(1),
        name=name,
    )(*args)


def _scan_ctx(v, kf, df, kb, db):
    n_chunks = v.shape[0] // SCAN_CHUNK
    groups = [((kf, v, df), False, 0), ((kf, v, df), False, 1), ((kb, v, db), True, 0), ((kb, v, db), True, 1)]
    s0 = jnp.zeros((2 * HEADS, HEAD_DIM, HEAD_DIM), F32)
    (state,) = _scan(groups, s0, 1, n_chunks, False, 0, "scan_ctx")
    return state


def _scan_latent(raster, colmajor, s0):
    t = raster[0].shape[0]
    n_steps = t // (CHUNKS_PER_STEP * SCAN_CHUNK)
    groups = []
    for d, backward in enumerate((False, True)):
        for v, *operands in (raster, colmajor):
            packed, table = operands[2 * d:2 * d + 2]
            groups.append(((packed, v, table), backward, 0))
    o_fr, o_fc, o_br, o_bc, _ = _scan(groups, s0, n_steps, CHUNKS_PER_STEP, True, t // GRID_W, "scan_latent")
    return o_fr, o_fc.reshape(t, HALF_W), o_br, o_bc.reshape(t, HALF_W)


def _outproj_kernel(a_ref, ofr_ref, ofc_ref, obr_ref, obc_ref, sg_ref, x_ref, w_ref,
                    g1_ref, sh2_ref, sc2_ref, og_ref, n2g_ref, x1_ref, h2_ref):
    def readout(rs):
        o_r = ofr_ref[rs, :].astype(F32) + obr_ref[rs, :].astype(F32)
        o_c = ofc_ref[rs, :] + obc_ref[rs, :]
        ys = []
        for h in range(HEADS):
            o = o_r if h < HALF_HEADS else o_c
            oh = o[:, (h % HALF_HEADS) * HEAD_DIM:(h % HALF_HEADS + 1) * HEAD_DIM]
            cs = slice(h * HEAD_DIM, (h + 1) * HEAD_DIM)
            ys.append((oh * _rms_scale(oh) * og_ref[:, cs] * sg_ref[rs, cs].astype(F32)).astype(BF16))
        return jnp.concatenate(ys, axis=1)

    def finish(rs, proj):
        x1 = x_ref[rs, :] + g1_ref[0:1, :] * proj
        x1_ref[rs, :] = x1
        scale = n2g_ref[...] * (1.0 + sc2_ref[0:1, :])
        h2_ref[rs, :] = (x1 * _rms_scale(x1) * scale + sh2_ref[0:1, :]).astype(h2_ref.dtype)

    half = x_ref.shape[0] // 2
    top, bottom = slice(0, half), slice(half, 2 * half)
    gmlp_side = lambda rs: jnp.dot(a_ref[rs, :], w_ref[:GROUP_W, :], preferred_element_type=F32)
    hgrn_side = lambda y: jnp.dot(y, w_ref[GROUP_W:, :], preferred_element_type=F32)
    p_top = gmlp_side(top)
    y_top = readout(top)
    p_bottom = gmlp_side(bottom)
    y_bottom = readout(bottom)
    p_top = p_top + hgrn_side(y_top)
    p_bottom = p_bottom + hgrn_side(y_bottom)
    finish(top, p_top)
    finish(bottom, p_bottom)


def _outproj(a, o_fr, o_fc, o_br, o_bc, sg, x, w_bf, mod, o_gain, n2_gain):
    t, d = x.shape
    tm = 512
    row = lambda i: (i, 0)
    const2 = lambda i: (0, 0)
    half = pl.BlockSpec((tm, HALF_W), row)
    full = pl.BlockSpec((tm, GROUP_W), row)
    mod_col = lambda c: pl.BlockSpec((2, d), lambda i: (0, c))
    return pl.pallas_call(
        _outproj_kernel,
        grid=(t // tm,),
        in_specs=[full, half, half, half, half, full,
                  pl.BlockSpec((tm, d), row),
                  pl.BlockSpec(w_bf.shape, const2, pipeline_mode=pl.Buffered(1)),
                  mod_col(2), mod_col(3), mod_col(4),
                  pl.BlockSpec((1, GROUP_W), const2),
                  pl.BlockSpec((1, d), const2)],
        out_specs=[pl.BlockSpec((tm, d), row), pl.BlockSpec((tm, d), row)],
        out_shape=[jax.ShapeDtypeStruct((t, d), F32), jax.ShapeDtypeStruct((t, d), BF16)],
        compiler_params=_params(1),
        name="outproj",
    )(a, o_fr, o_fc, o_br, o_bc, sg, x, w_bf, mod, mod, mod, o_gain, n2_gain)


def _ffn_kernel(h_ref, x1_ref, w1_ref, w2_ref, g2_ref, fg_ref, out_ref):
    f = pl.program_id(1)

    @pl.when(f == 0)
    def _():
        out_ref[...] = jnp.zeros_like(out_ref)

    hid = jnp.dot(h_ref[...], w1_ref[...], preferred_element_type=F32)
    hid = jnp.square(jnp.maximum(hid, 0.0)).astype(BF16)
    out_ref[...] += jnp.dot(hid, w2_ref[...], preferred_element_type=F32)

    @pl.when(f == pl.num_programs(1) - 1)
    def _():
        x2 = x1_ref[...] + g2_ref[0:1, :] * out_ref[...]
        out_ref[...] = x2 * _rms_scale(x2) * fg_ref[...]


def _ffn(h2, x1, w1_bf, w2_bf, mod, final_gain):
    t, d = x1.shape
    d_ff = w1_bf.shape[1]
    tm, tf = 1024, 1024
    row = lambda i, f: (i, 0)
    return pl.pallas_call(
        _ffn_kernel,
        grid=(t // tm, d_ff // tf),
        in_specs=[pl.BlockSpec((tm, d), row),
                  pl.BlockSpec((tm, d), row, pipeline_mode=pl.Buffered(1)),
                  pl.BlockSpec((d, tf), lambda i, f: (0, f)),
                  pl.BlockSpec((tf, d), lambda i, f: (f, 0)),
                  pl.BlockSpec((2, d), lambda i, f: (0, 5)),
                  pl.BlockSpec((1, d), lambda i, f: (0, 0))],
        out_specs=pl.BlockSpec((tm, d), row, pipeline_mode=pl.Buffered(1)),
        out_shape=jax.ShapeDtypeStruct((t, d), F32),
        compiler_params=_params(2),
        name="ffn",
    )(h2, x1, w1_bf, w2_bf, mod, final_gain)


def kernel(x, c, ctx, c_ctx, w_ada, b_ada, norm1_gain, w_in, gmlp_w_s, gmlp_b_s, gmlp_v_gain,
           hgrn_lb_logits, hgrn_o_gain, w_out, norm2_gain, w_ff1, w_ff2, final_gain):
    bsz, t, d = x.shape
    assert bsz == 1 and w_ada.shape[0] == 1, "single sample, single layer"
    assert d == 2 * GROUP_W and w_in.shape[2] == 7 * GROUP_W, "head groups of 8 x 128 channels"
    assert t % (GRID_W * SCAN_CHUNK) == 0 and t % (CHUNKS_PER_STEP * SCAN_CHUNK) == 0, "latent grid shape"
    assert ctx.shape[1] % SCAN_CHUNK == 0, "context length must be whole scan chunks"

    cond_t = jnp.stack([c[0], c_ctx], axis=1)
    mod, w_in_bf = _adaln(cond_t, w_ada[0], b_ada, w_in[0])

    b_s_b = jnp.broadcast_to(gmlp_b_s[0][:, :, None], (HEADS, GMLP_CHUNK, HEAD_DIM))
    a, sg, *rest = _inproj_raster(x[0], mod, norm1_gain, w_in_bf, gmlp_w_s[0], b_s_b, gmlp_v_gain[0],
                                  hgrn_lb_logits, [w_out[0], w_ff1[0], w_ff2[0]])
    raster, (w_out_bf, w1_bf, w2_bf) = rest[:-N_CASTS], rest[-N_CASTS:]
    colmajor = _inproj_colmajor(x[0], mod, norm1_gain, w_in_bf, hgrn_lb_logits)
    ctx_ops = _inproj_ctx(ctx[0], mod, norm1_gain, w_in_bf, hgrn_lb_logits)

    s0 = _scan_ctx(*ctx_ops)
    o_fr, o_fc, o_br, o_bc = _scan_latent(raster, colmajor, s0)

    x1, h2 = _outproj(a, o_fr, o_fc, o_br, o_bc, sg, x[0], w_out_bf, mod,
                      hgrn_o_gain.reshape(1, GROUP_W), norm2_gain)
    out = _ffn(h2, x1, w1_bf, w2_bf, mod, final_gain.reshape(1, d))
    return out[None]
```
